```python
import numpy as np
import jax
import jax.numpy as jnp
from jax import lax

D_MODEL = 1024
BATCH = 4
SEQ = 4096
DEPTH = 1

HEAD_DIM = 64
Q_BLOCK = 128
ROPE_THETA = 10000.0
RMS_EPS = 1e-5
NEG_INF = -1e30
FORCE_SCORE = 1e4
NSA_HEADS = 8
NSA_GROUPS = 2
NSA_REP = NSA_HEADS // NSA_GROUPS
NSA_BRANCHES = 3
CMP_LEN = 32
CMP_STRIDE = 16
SEL_BLOCK = 64
SEL_TOPK = 16
WINDOW = 512
FOX_HEADS = 8
MEM_HEADS = 4
MEM_HEAD_DIM = 128
MEM_LEN = 256
N_MIXERS = 3
N_EXPERTS = 32
TOP_K = 4
D_EXPERT = D_MODEL
SWIGLU_LIMIT = 7.0
SWIGLU_ALPHA = 1.702
MOE_BLOCK = 128

NSA_Q_W = NSA_HEADS * HEAD_DIM
NSA_KV_W = NSA_GROUPS * HEAD_DIM
FOX_W = FOX_HEADS * HEAD_DIM
MEM_W = MEM_HEADS * MEM_HEAD_DIM
IN_SPLITS = (
    NSA_Q_W,
    NSA_KV_W, NSA_KV_W,
    NSA_KV_W, NSA_KV_W,
    NSA_KV_W, NSA_KV_W,
    NSA_BRANCHES * NSA_HEADS,
    FOX_W, FOX_W, FOX_W,
    FOX_HEADS,
    MEM_W,
    N_MIXERS * D_MODEL,
)
D_IN = sum(IN_SPLITS)
IN_OFFSETS = tuple(int(o) for o in np.cumsum(IN_SPLITS)[:-1])

kernel_name = 'hybrid_nsa_fox_memory_moe_block'


def rms_norm(x, g):
    x32 = x.astype(jnp.float32)
    y = x32 * lax.rsqrt(jnp.mean(x32 * x32, axis=-1, keepdims=True) + RMS_EPS)
    return (y * g.astype(jnp.float32)).astype(x.dtype)


def rope_tables(seq, dim, dtype):
    inv = ROPE_THETA ** (-jnp.arange(0, dim, 2, dtype=jnp.float32) / dim)
    ang = jnp.arange(seq, dtype=jnp.float32)[:, None] * inv[None, :]
    return jnp.cos(ang).astype(dtype), jnp.sin(ang).astype(dtype)


def apply_rope(x, cos, sin):
    x1, x2 = jnp.split(x, 2, axis=-1)
    c = cos[:, None, :]
    s = sin[:, None, :]
    return jnp.concatenate([x1 * c - x2 * s, x2 * c + x1 * s], axis=-1)


def masked_softmax(s, mask):
    s = jnp.where(mask, s.astype(jnp.float32), NEG_INF)
    m = jnp.max(s, axis=-1, keepdims=True)
    e = jnp.where(mask, jnp.exp(s - m), 0.0)
    return e / jnp.maximum(jnp.sum(e, axis=-1, keepdims=True), 1e-30)


def compress_blocks(kv, pe, w1, w2):
    b, s, g, d = kv.shape
    kb = kv.reshape(b, s // CMP_STRIDE, CMP_STRIDE, g, d)
    blocks = jnp.concatenate([kb[:, :-1], kb[:, 1:]], axis=2)
    z = jnp.einsum('bnlgd,lde->bnge', blocks + pe[:, None, :], w1)
    return jax.nn.silu(z) @ w2


def nsa_mixer(q, kc, vc, ks, vs, kw, vw, gate_logits, pe_k, w1_k, w2_k, pe_v, w1_v, w2_v):
    b, s, h, d = q.shape
    scale = d ** -0.5
    nq = s // Q_BLOCK
    qg = q.reshape(b, s, NSA_GROUPS, NSA_REP, d)
    tpos = jnp.arange(s)

    kcmp = compress_blocks(kc, pe_k, w1_k, w2_k)
    vcmp = compress_blocks(vc, pe_v, w1_v, w2_v)
    n_cmp = kcmp.shape[1]
    cstart = jnp.arange(n_cmp) * CMP_STRIDE
    cmask = (cstart + CMP_LEN - 1)[None, :] <= tpos[:, None]
    s_c = jnp.einsum('bsgrd,bngd->bgrsn', qg, kcmp).astype(jnp.float32) * scale
    p_c = masked_softmax(s_c, cmask)
    o_cmp = jnp.einsum('bgrsn,bngd->bsgrd', p_c.astype(vcmp.dtype), vcmp)

    n_sel_blk = s // SEL_BLOCK
    n_top = min(SEL_TOPK, n_sel_blk)
    blk = jnp.arange(n_sel_blk)
    overlap = ((cstart[:, None] < (blk[None, :] + 1) * SEL_BLOCK)
               & (cstart[:, None] + CMP_LEN > blk[None, :] * SEL_BLOCK)).astype(jnp.float32)
    imp = jnp.einsum('bgrsn,nj->bgsj', p_c, overlap)
    cur = (tpos // SEL_BLOCK)[:, None]
    forced = (blk[None, :] == 0) | (blk[None, :] == cur) | (blk[None, :] == cur - 1)
    future = blk[None, :] * SEL_BLOCK > tpos[:, None]
    imp = jnp.where(forced, FORCE_SCORE, jnp.where(future, -FORCE_SCORE, imp))
    _, sel = lax.top_k(imp, n_top)

    kblk = ks.reshape(b, n_sel_blk, SEL_BLOCK, NSA_GROUPS, d).transpose(0, 3, 1, 2, 4)
    vblk = vs.reshape(b, n_sel_blk, SEL_BLOCK, NSA_GROUPS, d).transpose(0, 3, 1, 2, 4)
    bi = jnp.arange(b)[:, None, None, None]
    gi = jnp.arange(NSA_GROUPS)[None, :, None, None]
    offs = jnp.arange(SEL_BLOCK)
    n_keys = n_top * SEL_BLOCK
    q_chunks = qg.reshape(b, nq, Q_BLOCK, NSA_GROUPS, NSA_REP, d).transpose(1, 0, 2, 3, 4, 5)
    sel_chunks = sel.reshape(b, NSA_GROUPS, nq, Q_BLOCK, n_top).transpose(2, 0, 1, 3, 4)

    def sel_chunk(args):
        qi, si, ci = args
        kk = kblk[bi, gi, si].reshape(b, NSA_GROUPS, Q_BLOCK, n_keys, d)
        vv = vblk[bi, gi, si].reshape(b, NSA_GROUPS, Q_BLOCK, n_keys, d)
        kpos = (si[..., None] * SEL_BLOCK + offs).reshape(b, NSA_GROUPS, Q_BLOCK, n_keys)
        qpos = ci * Q_BLOCK + jnp.arange(Q_BLOCK)
        mask = (kpos <= qpos[None, None, :, None])[:, :, None]
        sc = jnp.einsum('btgrd,bgtkd->bgrtk', qi, kk).astype(jnp.float32) * scale
        p = masked_softmax(sc, mask)
        return jnp.einsum('bgrtk,bgtkd->btgrd', p.astype(vv.dtype), vv)

    o_slc = lax.map(sel_chunk, (q_chunks, sel_chunks, jnp.arange(nq)))
    o_slc = o_slc.transpose(1, 0, 2, 3, 4, 5).reshape(b, s, NSA_GROUPS, NSA_REP, d)

    n_prev = WINDOW // Q_BLOCK
    pad = ((0, 0), (WINDOW, 0), (0, 0), (0, 0))
    kwb = jnp.pad(kw, pad).reshape(b, nq + n_prev, Q_BLOCK, NSA_GROUPS, d)
    vwb = jnp.pad(vw, pad).reshape(b, nq + n_prev, Q_BLOCK, NSA_GROUPS, d)
    kband = jnp.concatenate([kwb[:, i:i + nq] for i in range(n_prev + 1)], axis=2)
    vband = jnp.concatenate([vwb[:, i:i + nq] for i in range(n_prev + 1)], axis=2)
    qb = qg.reshape(b, nq, Q_BLOCK, NSA_GROUPS, NSA_REP, d)
    qpos = jnp.arange(nq)[:, None] * Q_BLOCK + jnp.arange(Q_BLOCK)[None, :]
    kpos = jnp.arange(nq)[:, None] * Q_BLOCK - WINDOW + jnp.arange((n_prev + 1) * Q_BLOCK)[None, :]
    diff = qpos[:, :, None] - kpos[:, None, :]
    wmask = (diff >= 0) & (diff < WINDOW) & (kpos[:, None, :] >= 0)
    sw = jnp.einsum('bctgrd,bckgd->bcgrtk', qb, kband).astype(jnp.float32) * scale
    pw = masked_softmax(sw, wmask[None, :, None, None])
    o_win = jnp.einsum('bcgrtk,bckgd->bctgrd', pw.astype(vband.dtype), vband)
    o_win = o_win.reshape(b, s, NSA_GROUPS, NSA_REP, d)

    g = jax.nn.sigmoid(gate_logits.astype(jnp.float32)).astype(q.dtype)
    g = g.reshape(b, s, NSA_GROUPS, NSA_REP, NSA_BRANCHES)
    o = g[..., 0:1] * o_cmp + g[..., 1:2] * o_slc + g[..., 2:3] * o_win
    return o.reshape(b, s, h * d)


def fox_mixer(q, k, v, f_logits, b_forget):
    b, s, h, d = q.shape
    scale = d ** -0.5
    nq = s // Q_BLOCK
    logf = jax.nn.log_sigmoid(f_logits.astype(jnp.float32) + b_forget.astype(jnp.float32))
    cum = jnp.cumsum(logf, axis=1).transpose(0, 2, 1)
    q_chunks = q.reshape(b, nq, Q_BLOCK, h, d).transpose(1, 0, 2, 3, 4)
    c_chunks = cum.reshape(b, h, nq, Q_BLOCK).transpose(2, 0, 1, 3)
    kpos = jnp.arange(s)

    def block(args):
        qi, ci, idx = args
        sc = (jnp.einsum('bthd,bshd->bhts', qi, k).astype(jnp.float32) * scale
              + ci[..., None] - cum[:, :, None, :])
        qpos = idx * Q_BLOCK + jnp.arange(Q_BLOCK)
        p = masked_softmax(sc, kpos[None, :] <= qpos[:, None])
        return jnp.einsum('bhts,bshd->bthd', p.astype(v.dtype), v)

    o = lax.map(block, (q_chunks, c_chunks, jnp.arange(nq)))
    return o.transpose(1, 0, 2, 3, 4).reshape(b, s, h * d)


def memory_mixer(q, mem_h, w_mem_kv):
    b, s, h, d = q.shape
    m = mem_h.shape[1]
    k, v = jnp.split(mem_h @ w_mem_kv, 2, axis=-1)
    k = k.reshape(b, m, h, d)
    v = v.reshape(b, m, h, d)
    sc = jnp.einsum('bshd,bmhd->bhsm', q, k).astype(jnp.float32) * (d ** -0.5)
    p = jax.nn.softmax(sc, axis=-1)
    o = jnp.einsum('bhsm,bmhd->bshd', p.astype(v.dtype), v)
    return o.reshape(b, s, h * d)


def moe_ffn(h, w_router, b_router, w_gate_up, b_gate_up, w_down, b_down):
    b, s, d = h.shape
    t = b * s
    hf = h.reshape(t, d)
    logits = (hf @ w_router + b_router).astype(jnp.float32)
    top_val, top_idx = lax.top_k(logits, TOP_K)
    gate = jax.nn.softmax(top_val, axis=-1)
    n_assign = t * TOP_K
    e_flat = top_idx.reshape(n_assign)
    tok_flat = jnp.repeat(jnp.arange(t, dtype=jnp.int32), TOP_K)
    g_flat = gate.reshape(n_assign)
    order = jnp.argsort(e_flat, stable=True)
    e_sorted = e_flat[order]
    counts = jnp.bincount(e_flat, length=N_EXPERTS)
    starts = jnp.cumsum(counts) - counts
    padded = (counts + MOE_BLOCK - 1) // MOE_BLOCK * MOE_BLOCK
    pad_ends = jnp.cumsum(padded)
    pad_starts = pad_ends - padded
    dest = pad_starts[e_sorted] + jnp.arange(n_assign) - starts[e_sorted]
    n_blocks = -(-(n_assign + N_EXPERTS * (MOE_BLOCK - 1)) // MOE_BLOCK)
    n_slots = n_blocks * MOE_BLOCK
    slot_tok = jnp.full((n_slots,), t, jnp.int32).at[dest].set(tok_flat[order])
    slot_gate = jnp.zeros((n_slots,), jnp.float32).at[dest].set(g_flat[order])
    block_exp = jnp.minimum(
        jnp.searchsorted(pad_ends, jnp.arange(n_blocks) * MOE_BLOCK, side='right'), N_EXPERTS - 1)
    h_pad = jnp.concatenate([hf, jnp.zeros((1, d), hf.dtype)], axis=0)

    def expert_block(args):
        e, toks = args
        xt = h_pad[toks]
        gu = xt @ w_gate_up[e] + b_gate_up[e]
        g = jnp.minimum(gu[:, 0::2], SWIGLU_LIMIT)
        u = jnp.clip(gu[:, 1::2], -SWIGLU_LIMIT, SWIGLU_LIMIT)
        act = (u + 1.0) * g * jax.nn.sigmoid(SWIGLU_ALPHA * g)
        return act @ w_down[e] + b_down[e]

    out = lax.map(expert_block, (block_exp, slot_tok.reshape(n_blocks, MOE_BLOCK)))
    out = out.reshape(n_slots, d) * slot_gate[:, None].astype(out.dtype)
    y = jax.ops.segment_sum(out, slot_tok, num_segments=t + 1)[:t]
    return y.reshape(b, s, d)


def setup_inputs(seed: int = 0) -> dict:
    key = jax.random.key(seed)
    ks = jax.random.split(key, 32)
    L = DEPTH
    F = D_EXPERT

    def nrm(k, shape, scale):
        return jax.random.normal(k, shape, jnp.float32) * scale

    def gain(k, shape):
        return 1.0 + 0.02 * jax.random.normal(k, shape, jnp.float32)

    return {
        'x': nrm(ks[0], (BATCH, SEQ, D_MODEL), 1.0),
        'mem': nrm(ks[1], (BATCH, MEM_LEN, D_MODEL), 1.0),
        'g_mix': gain(ks[2], (L, D_MODEL)),
        'w_in': nrm(ks[3], (L, D_MODEL, D_IN), D_MODEL ** -0.5),
        'b_forget': jax.random.uniform(ks[4], (L, FOX_HEADS), jnp.float32, 1.0, 6.0),
        'b_merge': nrm(ks[5], (L, N_MIXERS * D_MODEL), 0.02),
        'nsa_pe_k': nrm(ks[6], (L, CMP_LEN, HEAD_DIM), 0.02),
        'nsa_w1_k': nrm(ks[7], (L, CMP_LEN, HEAD_DIM, HEAD_DIM), (CMP_LEN * HEAD_DIM) ** -0.5),
        'nsa_w2_k': nrm(ks[8], (L, HEAD_DIM, HEAD_DIM), HEAD_DIM ** -0.5),
        'nsa_pe_v': nrm(ks[9], (L, CMP_LEN, HEAD_DIM), 0.02),
        'nsa_w1_v': nrm(ks[10], (L, CMP_LEN, HEAD_DIM, HEAD_DIM), (CMP_LEN * HEAD_DIM) ** -0.5),
        'nsa_w2_v': nrm(ks[11], (L, HEAD_DIM, HEAD_DIM), HEAD_DIM ** -0.5),
        'g_mem': gain(ks[12], (L, D_MODEL)),
        'w_mem_kv': nrm(ks[13], (L, D_MODEL, 2 * MEM_W), D_MODEL ** -0.5),
        'w_branch_nsa': nrm(ks[14], (L, NSA_Q_W, D_MODEL), NSA_Q_W ** -0.5),
        'w_branch_fox': nrm(ks[15], (L, FOX_W, D_MODEL), FOX_W ** -0.5),
        'w_branch_mem': nrm(ks[16], (L, MEM_W, D_MODEL), MEM_W ** -0.5),
        'w_out': nrm(ks[17], (L, D_MODEL, D_MODEL), D_MODEL ** -0.5),
        'g_ffn': gain(ks[18], (L, D_MODEL)),
        'w_router': nrm(ks[19], (L, D_MODEL, N_EXPERTS), D_MODEL ** -0.5),
        'b_router': nrm(ks[20], (L, N_EXPERTS), 0.01),
        'w_gate_up': nrm(ks[21], (L, N_EXPERTS, D_MODEL, 2 * F), D_MODEL ** -0.5),
        'b_gate_up': nrm(ks[22], (L, N_EXPERTS, 2 * F), 0.02),
        'w_down': nrm(ks[23], (L, N_EXPERTS, F, D_MODEL), F ** -0.5),
        'b_down': nrm(ks[24], (L, N_EXPERTS, D_MODEL), 0.02),
        'g_final': gain(ks[25], (D_MODEL,)),
    }


def reference(x, mem, g_mix, w_in, b_forget, b_merge, nsa_pe_k, nsa_w1_k, nsa_w2_k, nsa_pe_v,
              nsa_w1_v, nsa_w2_v, g_mem, w_mem_kv, w_branch_nsa, w_branch_fox, w_branch_mem, w_out,
              g_ffn, w_router, b_router, w_gate_up, b_gate_up, w_down, b_down, g_final):
    b, s, _ = x.shape
    cos, sin = rope_tables(s, HEAD_DIM, x.dtype)

    def heads(t, n, d):
        return t.reshape(b, s, n, d)

    for l in range(DEPTH):
        h = rms_norm(x, g_mix[l])
        parts = jnp.split(h @ w_in[l], IN_OFFSETS, axis=-1)
        (q_a, k_c, v_c, k_s, v_s, k_w, v_w, nsa_gl,
         q_f, k_f, v_f, f_l, q_m, merge_l) = parts

        o_nsa = nsa_mixer(
            apply_rope(heads(q_a, NSA_HEADS, HEAD_DIM), cos, sin),
            apply_rope(heads(k_c, NSA_GROUPS, HEAD_DIM), cos, sin), heads(v_c, NSA_GROUPS, HEAD_DIM),
            apply_rope(heads(k_s, NSA_GROUPS, HEAD_DIM), cos, sin), heads(v_s, NSA_GROUPS, HEAD_DIM),
            apply_rope(heads(k_w, NSA_GROUPS, HEAD_DIM), cos, sin), heads(v_w, NSA_GROUPS, HEAD_DIM),
            nsa_gl, nsa_pe_k[l], nsa_w1_k[l], nsa_w2_k[l], nsa_pe_v[l], nsa_w1_v[l], nsa_w2_v[l])
        o_fox = fox_mixer(heads(q_f, FOX_HEADS, HEAD_DIM), heads(k_f, FOX_HEADS, HEAD_DIM),
                          heads(v_f, FOX_HEADS, HEAD_DIM), f_l, b_forget[l])
        o_mem = memory_mixer(heads(q_m, MEM_HEADS, MEM_HEAD_DIM), rms_norm(mem, g_mem[l]), w_mem_kv[l])

        gates = jax.nn.sigmoid((merge_l + b_merge[l]).astype(jnp.float32)).astype(x.dtype)
        g_nsa, g_fox, g_mem_br = jnp.split(gates, N_MIXERS, axis=-1)
        merged = (g_nsa * (o_nsa @ w_branch_nsa[l])
                  + g_fox * (o_fox @ w_branch_fox[l])
                  + g_mem_br * (o_mem @ w_branch_mem[l]))
        x = x + merged @ w_out[l]

        x = x + moe_ffn(rms_norm(x, g_ffn[l]), w_router[l], b_router[l], w_gate_up[l],
                        b_gate_up[l], w_down[l], b_down[l])

    return rms_norm(x, g_final)
```

```python
import functools

import jax
import jax.numpy as jnp
import numpy as np
from jax import lax
from jax.experimental import pallas as pl
from jax.experimental.pallas import tpu as pltpu

F32 = jnp.float32
BF16 = jnp.bfloat16

D_MODEL = 1024
HEAD_DIM = 64
ROPE_THETA = 10000.0
RMS_EPS = 1e-5
NEG_INF = -1e30
FORCE_SCORE = 1e4
NSA_HEADS = 8
NSA_GROUPS = 2
NSA_REP = NSA_HEADS // NSA_GROUPS
CMP_LEN = 32
CMP_STRIDE = 16
SEL_BLOCK = 64
SEL_TOPK = 16
WINDOW = 512
FOX_HEADS = 8
MEM_HEADS = 4
MEM_HEAD_DIM = 128
N_EXPERTS = 32
TOP_K = 4
SWIGLU_LIMIT = 7.0
SWIGLU_ALPHA = 1.702

LANES = 128
VMEM_LIMIT = 48 * 1024 * 1024

C_MERGE = 0
C_QA = 3072
C_QF = 3584
C_KF = 4096
C_VF = 4608
C_QM = 5120
C_NSAKV = 5632
C_SMALL = 6400
N_PROJ = 6656
GL_OFF = 0
FL_OFF = 24

MOE_ROWS = 256


def _cparams(sem, vmem=VMEM_LIMIT):
    return pltpu.CompilerParams(dimension_semantics=sem, vmem_limit_bytes=vmem)


def _dot(a, b):
    return jnp.dot(a, b, preferred_element_type=F32)


def _dot_nt(a, b):
    return lax.dot_general(a, b, (((1,), (1,)), ((), ())), preferred_element_type=F32)


def _rope(x, cos, sin_signed):
    w = x.shape[-1]
    lane = lax.broadcasted_iota(jnp.int32, x.shape, x.ndim - 1)
    first = (lane & (HEAD_DIM - 1)) < (HEAD_DIM // 2)
    rot = jnp.where(first, pltpu.roll(x, w - HEAD_DIM // 2, x.ndim - 1),
                    pltpu.roll(x, HEAD_DIM // 2, x.ndim - 1))
    return x * cos + rot * sin_signed


def _split3(x):
    hi = x.astype(BF16)
    r1 = x - hi.astype(F32)
    mid = r1.astype(BF16)
    lo = (r1 - mid.astype(F32)).astype(BF16)
    return hi, mid, lo


def _inproj_kernel(x_ref, g_ref, w_ref, o_ref, hn_ref):
    @pl.when(pl.program_id(1) == 0)
    def _():
        x = x_ref[...]
        ms = jnp.mean(x * x, axis=-1, keepdims=True)
        hn_ref[...] = (x * lax.rsqrt(ms + RMS_EPS) * g_ref[...]).astype(BF16)

    o_ref[...] = _dot(hn_ref[...], w_ref[...])


def _inproj(x2, g, w_bf16, tm=1024, tn=512):
    t, d = x2.shape
    n = w_bf16.shape[1]
    return pl.pallas_call(
        _inproj_kernel,
        out_shape=jax.ShapeDtypeStruct((t, n), F32),
        grid=(t // tm, n // tn),
        in_specs=[pl.BlockSpec((tm, d), lambda i, j: (i, 0)),
                  pl.BlockSpec((1, d), lambda i, j: (0, 0)),
                  pl.BlockSpec((d, tn), lambda i, j: (0, j))],
        out_specs=pl.BlockSpec((tm, tn), lambda i, j: (i, j)),
        scratch_shapes=[pltpu.VMEM((tm, d), BF16)],
        compiler_params=_cparams(("parallel", "arbitrary")),
        name="inproj",
    )(x2, g, w_bf16)


def _cum_kernel(s_ref, b_ref, c_ref, ct_ref, *, blk):
    s = s_ref.shape[1]
    z = s_ref[0] + b_ref[...]
    logf = jnp.minimum(z, 0.0) - jnp.log1p(jnp.exp(-jnp.abs(z)))
    r = lax.broadcasted_iota(jnp.int32, (blk, blk), 0)
    c = lax.broadcasted_iota(jnp.int32, (blk, blk), 1)
    tri = jnp.where(c <= r, 1.0, 0.0).astype(BF16)
    carry = jnp.zeros((1, LANES), F32)
    for i in range(s // blk):
        hi, mid, lo = _split3(logf[i * blk:(i + 1) * blk])
        loc = (_dot(tri, hi) + _dot(tri, mid)) + _dot(tri, lo)
        out = loc + carry
        c_ref[0, i * blk:(i + 1) * blk, :] = out
        carry = out[blk - 1:blk, :]
    ct_ref[0] = c_ref[0].T


def _fox_cum(proj3, b_row, blk=256):
    b, s, _ = proj3.shape
    return pl.pallas_call(
        functools.partial(_cum_kernel, blk=blk),
        out_shape=(jax.ShapeDtypeStruct((b, s, LANES), F32),
                   jax.ShapeDtypeStruct((b, LANES, s), F32)),
        grid=(b,),
        in_specs=[pl.BlockSpec((1, s, LANES), lambda i: (i, 0, C_SMALL // LANES)),
                  pl.BlockSpec((1, LANES), lambda i: (0, 0))],
        out_specs=(pl.BlockSpec((1, s, LANES), lambda i: (i, 0, 0)),
                   pl.BlockSpec((1, LANES, s), lambda i: (i, 0, 0))),
        compiler_params=_cparams(("parallel",)),
        name="fox_cum",
    )(proj3, b_row)


def _cmp_kernel(ak_ref, av_ref, cos_ref, sin_ref, pek_ref, pev_ref, w1k_ref, w1v_ref,
                w2k_ref, w2v_ref, kc_ref, vc_ref):
    nb = ak_ref.shape[2]
    half = ak_ref.shape[3]

    def mlp(a, pe_ref, w1_ref, w2_ref):
        pa = _dot((a + pe_ref[0:1, :]).astype(BF16), w1_ref[0:half, :])
        pb = _dot((a + pe_ref[1:2, :]).astype(BF16), w1_ref[half:2 * half, :])
        z = pa + pltpu.roll(pb, nb - 1, 0)
        h = z * jax.nn.sigmoid(z)
        return _dot(h.astype(BF16), w2_ref[...])

    kc_ref[0, 0] = mlp(_rope(ak_ref[0, 0], cos_ref[...], sin_ref[...]), pek_ref, w1k_ref, w2k_ref)
    vc_ref[0, 0] = mlp(av_ref[0, 0], pev_ref, w1v_ref, w2v_ref)


def _nsa_compress(ak, av, cos_a, sin_a, pek, pev, w1k, w1v, w2k, w2v):
    b, g, nb, wide = ak.shape
    blk4 = pl.BlockSpec((1, 1, nb, wide), lambda i, j: (i, j, 0, 0))
    full = lambda shp: pl.BlockSpec(shp, lambda i, j: (0,) * len(shp))
    out = pl.BlockSpec((1, 1, nb, HEAD_DIM), lambda i, j: (i, j, 0, 0))
    return pl.pallas_call(
        _cmp_kernel,
        out_shape=(jax.ShapeDtypeStruct((b, g, nb, HEAD_DIM), F32),) * 2,
        grid=(b, g),
        in_specs=[blk4, blk4, full(cos_a.shape), full(sin_a.shape), full(pek.shape), full(pev.shape),
                  full(w1k.shape), full(w1v.shape), full(w2k.shape), full(w2v.shape)],
        out_specs=(out, out),
        compiler_params=_cparams(("parallel", "parallel")),
        name="nsa_compress",
    )(ak, av, cos_a, sin_a, pek, pev, w1k, w1v, w2k, w2v)


def _nsa_sel_kernel(q_ref, cos_ref, sin_ref, kc_ref, vc_ref, sm_ref, oc_ref, sel_ref, *, tq, n_sel):
    qi = pl.program_id(1)
    nb = kc_ref.shape[2]
    scale = HEAD_DIM ** -0.5
    cos = cos_ref[...]
    sin = sin_ref[...]
    tpos = qi * tq + lax.broadcasted_iota(jnp.int32, (tq, nb), 0)
    ncol = lax.broadcasted_iota(jnp.int32, (tq, nb), 1)
    cmask = (ncol * CMP_STRIDE + (CMP_LEN - 1) <= tpos) & (ncol < nb - 1)
    on = lax.broadcasted_iota(jnp.int32, (nb, n_sel), 0) * CMP_STRIDE
    oj = lax.broadcasted_iota(jnp.int32, (nb, n_sel), 1)
    overlap = jnp.where((on < (oj + 1) * SEL_BLOCK) & (on + CMP_LEN > oj * SEL_BLOCK), 1.0, 0.0).astype(BF16)
    trow = qi * tq + lax.broadcasted_iota(jnp.int32, (tq, n_sel), 0)
    blk = lax.broadcasted_iota(jnp.int32, (tq, n_sel), 1)
    cur = trow // SEL_BLOCK
    forced = (blk == 0) | (blk == cur) | (blk == cur - 1)
    future = blk * SEL_BLOCK > trow
    gates = jax.nn.sigmoid(sm_ref[0])
    for g in range(NSA_GROUPS):
        kc = kc_ref[0, g].astype(BF16)
        vc = vc_ref[0, g].astype(BF16)
        psum = jnp.zeros((tq, nb), F32)
        for r in range(NSA_REP):
            h = g * NSA_REP + r
            hp = h // 2
            q2 = _rope(q_ref[0, :, hp * LANES:(hp + 1) * LANES], cos, sin)
            qh = q2[:, (h % 2) * HEAD_DIM:(h % 2 + 1) * HEAD_DIM].astype(BF16)
            s = jnp.where(cmask, _dot_nt(qh, kc) * scale, NEG_INF)
            m = jnp.max(s, axis=-1, keepdims=True)
            e = jnp.where(cmask, jnp.exp(s - m), 0.0)
            p = e / jnp.maximum(jnp.sum(e, axis=-1, keepdims=True), 1e-30)
            psum = psum + p
            gate = gates[:, GL_OFF + 3 * h:GL_OFF + 3 * h + 1]
            oc_ref[0, :, h * HEAD_DIM:(h + 1) * HEAD_DIM] = gate * _dot(p.astype(BF16), vc)
        hi, mid, lo = _split3(psum)
        imp = (_dot(hi, overlap) + _dot(mid, overlap)) + _dot(lo, overlap)
        imp = jnp.where(forced, FORCE_SCORE, jnp.where(future, -FORCE_SCORE, imp))
        rank = jnp.zeros((tq, n_sel), F32)
        for i in range(n_sel):
            ci = imp[:, i:i + 1]
            ahead = (ci > imp) | ((ci == imp) & (blk > i))
            rank = rank + jnp.where(ahead, 1.0, 0.0)
        sel_ref[0, :, g * n_sel:(g + 1) * n_sel] = jnp.where(rank < float(min(SEL_TOPK, n_sel)), 1.0, 0.0)


def _nsa_select(proj3, cos2, sin2, kcmp, vcmp, tq=256):
    b, s, _ = proj3.shape
    nb = kcmp.shape[2]
    n_sel = s // SEL_BLOCK
    wq = NSA_HEADS * HEAD_DIM
    return pl.pallas_call(
        functools.partial(_nsa_sel_kernel, tq=tq, n_sel=n_sel),
        out_shape=(jax.ShapeDtypeStruct((b, s, wq), F32),
                   jax.ShapeDtypeStruct((b, s, NSA_GROUPS * n_sel), F32)),
        grid=(b, s // tq),
        in_specs=[pl.BlockSpec((1, tq, wq), lambda i, j: (i, j, C_QA // wq)),
                  pl.BlockSpec((tq, LANES), lambda i, j: (j, 0)),
                  pl.BlockSpec((tq, LANES), lambda i, j: (j, 0)),
                  pl.BlockSpec((1, NSA_GROUPS, nb, HEAD_DIM), lambda i, j: (i, 0, 0, 0)),
                  pl.BlockSpec((1, NSA_GROUPS, nb, HEAD_DIM), lambda i, j: (i, 0, 0, 0)),
                  pl.BlockSpec((1, tq, LANES), lambda i, j: (i, j, C_SMALL // LANES))],
        out_specs=(pl.BlockSpec((1, tq, wq), lambda i, j: (i, j, 0)),
                   pl.BlockSpec((1, tq, NSA_GROUPS * n_sel), lambda i, j: (i, j, 0))),
        compiler_params=_cparams(("parallel", "parallel")),
        name="nsa_select",
    )(proj3, cos2, sin2, kcmp, vcmp, proj3)


def _flash_step(s, mask, v, m_ref, l_ref, acc_ref, idx):
    s = jnp.where(mask, s, NEG_INF)
    m_old = m_ref[idx]
    m_new = jnp.maximum(m_old, jnp.max(s, axis=-1, keepdims=True))
    alpha = jnp.exp(m_old - m_new)
    p = jnp.where(mask, jnp.exp(s - m_new), 0.0)
    l_ref[idx] = alpha * l_ref[idx] + jnp.sum(p, axis=-1, keepdims=True)
    acc_ref[idx] = alpha * acc_ref[idx] + _dot(p.astype(BF16), v)
    m_ref[idx] = m_new


def _flash_out(l_ref, acc_ref, idx):
    return acc_ref[idx] / jnp.maximum(l_ref[idx], 1e-30)


def _nsa_attn_kernel(q_ref, cq_ref, sq_ref, ck_ref, sk_ref, ks_ref, vs_ref, kw_ref, vw_ref,
                     sel_ref, oc_ref, sm_ref, o_ref, qs_ref, m_ref, l_ref, acc_ref, *, tq, tk, n_sel):
    qi = pl.program_id(1)
    kj = pl.program_id(2)
    nh = NSA_HEADS
    n_prev = WINDOW // tk

    @pl.when(kj == 0)
    def _():
        for hp in range(nh // 2):
            q2 = _rope(q_ref[0, :, hp * LANES:(hp + 1) * LANES], cq_ref[...], sq_ref[...])
            qs_ref[:, hp * LANES:(hp + 1) * LANES] = (q2 * (HEAD_DIM ** -0.5)).astype(BF16)
        m_ref[...] = jnp.full(m_ref.shape, NEG_INF, F32)
        l_ref[...] = jnp.zeros(l_ref.shape, F32)
        acc_ref[...] = jnp.zeros(acc_ref.shape, F32)

    last = (qi * tq + tq - 1) // tk

    @pl.when(kj <= last)
    def _():
        qpos = qi * tq + lax.broadcasted_iota(jnp.int32, (tq, tk), 0)
        kpos = kj * tk + lax.broadcasted_iota(jnp.int32, (tq, tk), 1)
        causal = kpos <= qpos
        ks = _rope(ks_ref[0], ck_ref[...], sk_ref[...]).astype(BF16)
        vs = vs_ref[0].astype(BF16)
        er = lax.broadcasted_iota(jnp.int32, (n_sel, tk), 0)
        ec = kj * tk + lax.broadcasted_iota(jnp.int32, (n_sel, tk), 1)
        expand = jnp.where(er == ec // SEL_BLOCK, 1.0, 0.0).astype(BF16)
        for g in range(NSA_GROUPS):
            chosen = _dot(sel_ref[0, :, g * n_sel:(g + 1) * n_sel].astype(BF16), expand) > 0.5
            mask = chosen & causal
            kg = ks[:, g * HEAD_DIM:(g + 1) * HEAD_DIM]
            vg = vs[:, g * HEAD_DIM:(g + 1) * HEAD_DIM]
            for r in range(NSA_REP):
                h = g * NSA_REP + r
                s = _dot_nt(qs_ref[:, h * HEAD_DIM:(h + 1) * HEAD_DIM], kg)
                _flash_step(s, mask, vg, m_ref, l_ref, acc_ref, h)

    @pl.when((kj <= last) & (kj + n_prev >= (qi * tq) // tk))
    def _():
        qpos = qi * tq + lax.broadcasted_iota(jnp.int32, (tq, tk), 0)
        kpos = kj * tk + lax.broadcasted_iota(jnp.int32, (tq, tk), 1)
        diff = qpos - kpos
        mask = (diff >= 0) & (diff < WINDOW)
        kw = _rope(kw_ref[0], ck_ref[...], sk_ref[...]).astype(BF16)
        vw = vw_ref[0].astype(BF16)
        for g in range(NSA_GROUPS):
            kg = kw[:, g * HEAD_DIM:(g + 1) * HEAD_DIM]
            vg = vw[:, g * HEAD_DIM:(g + 1) * HEAD_DIM]
            for r in range(NSA_REP):
                h = g * NSA_REP + r
                s = _dot_nt(qs_ref[:, h * HEAD_DIM:(h + 1) * HEAD_DIM], kg)
                _flash_step(s, mask, vg, m_ref, l_ref, acc_ref, nh + h)

    @pl.when(kj == last)
    def _():
        gates = jax.nn.sigmoid(sm_ref[0])
        for h in range(nh):
            g_sel = gates[:, GL_OFF + 3 * h + 1:GL_OFF + 3 * h + 2]
            g_win = gates[:, GL_OFF + 3 * h + 2:GL_OFF + 3 * h + 3]
            o_ref[0, :, h * HEAD_DIM:(h + 1) * HEAD_DIM] = (
                oc_ref[0, :, h * HEAD_DIM:(h + 1) * HEAD_DIM]
                + g_sel * _flash_out(l_ref, acc_ref, h)
                + g_win * _flash_out(l_ref, acc_ref, nh + h))


def _nsa_attend(proj3, cos2, sin2, sel, oc, tq=256, tk=256):
    b, s, _ = proj3.shape
    n_sel = s // SEL_BLOCK
    wq = NSA_HEADS * HEAD_DIM
    kvb = C_NSAKV // LANES
    n_prev = WINDOW // tk

    def kv_sel(i, j, k):
        return jnp.minimum(k, (j * tq + tq - 1) // tk)

    def kv_win(i, j, k):
        return jnp.maximum(kv_sel(i, j, k), jnp.maximum((j * tq) // tk - n_prev, 0))

    return pl.pallas_call(
        functools.partial(_nsa_attn_kernel, tq=tq, tk=tk, n_sel=n_sel),
        out_shape=jax.ShapeDtypeStruct((b, s, wq), F32),
        grid=(b, s // tq, s // tk),
        in_specs=[pl.BlockSpec((1, tq, wq), lambda i, j, k: (i, j, C_QA // wq)),
                  pl.BlockSpec((tq, LANES), lambda i, j, k: (j, 0)),
                  pl.BlockSpec((tq, LANES), lambda i, j, k: (j, 0)),
                  pl.BlockSpec((tk, LANES), lambda i, j, k: (kv_sel(i, j, k), 0)),
                  pl.BlockSpec((tk, LANES), lambda i, j, k: (kv_sel(i, j, k), 0)),
                  pl.BlockSpec((1, tk, LANES), lambda i, j, k: (i, kv_sel(i, j, k), kvb + 2)),
                  pl.BlockSpec((1, tk, LANES), lambda i, j, k: (i, kv_sel(i, j, k), kvb + 3)),
                  pl.BlockSpec((1, tk, LANES), lambda i, j, k: (i, kv_win(i, j, k), kvb + 4)),
                  pl.BlockSpec((1, tk, LANES), lambda i, j, k: (i, kv_win(i, j, k), kvb + 5)),
                  pl.BlockSpec((1, tq, NSA_GROUPS * n_sel), lambda i, j, k: (i, j, 0)),
                  pl.BlockSpec((1, tq, wq), lambda i, j, k: (i, j, 0)),
                  pl.BlockSpec((1, tq, LANES), lambda i, j, k: (i, j, C_SMALL // LANES))],
        out_specs=pl.BlockSpec((1, tq, wq), lambda i, j, k: (i, j, 0)),
        scratch_shapes=[pltpu.VMEM((tq, wq), BF16),
                        pltpu.VMEM((2 * NSA_HEADS, tq, 1), F32),
                        pltpu.VMEM((2 * NSA_HEADS, tq, 1), F32),
                        pltpu.VMEM((2 * NSA_HEADS, tq, HEAD_DIM), F32)],
        compiler_params=_cparams(("parallel", "parallel", "arbitrary")),
        name="nsa_attend",
    )(proj3, cos2, sin2, cos2, sin2, proj3, proj3, proj3, proj3, sel, oc, proj3)


def _fox_kernel(q_ref, k_ref, v_ref, cq_ref, ck_ref, o_ref, qs_ref, m_ref, l_ref, acc_ref, *, tq, tk):
    qi = pl.program_id(1)
    kj = pl.program_id(2)

    @pl.when(kj == 0)
    def _():
        qs_ref[...] = (q_ref[0] * (HEAD_DIM ** -0.5)).astype(BF16)
        m_ref[...] = jnp.full(m_ref.shape, NEG_INF, F32)
        l_ref[...] = jnp.zeros(l_ref.shape, F32)
        acc_ref[...] = jnp.zeros(acc_ref.shape, F32)

    last = (qi * tq + tq - 1) // tk

    @pl.when(kj <= last)
    def _():
        qpos = qi * tq + lax.broadcasted_iota(jnp.int32, (tq, tk), 0)
        kpos = kj * tk + lax.broadcasted_iota(jnp.int32, (tq, tk), 1)
        mask = kpos <= qpos
        k = k_ref[0].astype(BF16)
        v = v_ref[0].astype(BF16)
        cq = cq_ref[0]
        ck = ck_ref[0]
        for h in range(FOX_HEADS):
            s = _dot_nt(qs_ref[:, h * HEAD_DIM:(h + 1) * HEAD_DIM], k[:, h * HEAD_DIM:(h + 1) * HEAD_DIM])
            s = s + cq[:, FL_OFF + h:FL_OFF + h + 1] - ck[FL_OFF + h:FL_OFF + h + 1, :]
            _flash_step(s, mask, v[:, h * HEAD_DIM:(h + 1) * HEAD_DIM], m_ref, l_ref, acc_ref, h)

    @pl.when(kj == last)
    def _():
        for h in range(FOX_HEADS):
            o_ref[0, :, h * HEAD_DIM:(h + 1) * HEAD_DIM] = _flash_out(l_ref, acc_ref, h)


def _fox_attend(proj3, cum, cum_t, tq=256, tk=512):
    b, s, _ = proj3.shape
    w = FOX_HEADS * HEAD_DIM

    def kv(i, j, k):
        return jnp.minimum(k, (j * tq + tq - 1) // tk)

    return pl.pallas_call(
        functools.partial(_fox_kernel, tq=tq, tk=tk),
        out_shape=jax.ShapeDtypeStruct((b, s, w), F32),
        grid=(b, s // tq, s // tk),
        in_specs=[pl.BlockSpec((1, tq, w), lambda i, j, k: (i, j, C_QF // w)),
                  pl.BlockSpec((1, tk, w), lambda i, j, k: (i, kv(i, j, k), C_KF // w)),
                  pl.BlockSpec((1, tk, w), lambda i, j, k: (i, kv(i, j, k), C_VF // w)),
                  pl.BlockSpec((1, tq, LANES), lambda i, j, k: (i, j, 0)),
                  pl.BlockSpec((1, LANES, tk), lambda i, j, k: (i, 0, kv(i, j, k)))],
        out_specs=pl.BlockSpec((1, tq, w), lambda i, j, k: (i, j, 0)),
        scratch_shapes=[pltpu.VMEM((tq, w), BF16),
                        pltpu.VMEM((FOX_HEADS, tq, 1), F32),
                        pltpu.VMEM((FOX_HEADS, tq, 1), F32),
                        pltpu.VMEM((FOX_HEADS, tq, HEAD_DIM), F32)],
        compiler_params=_cparams(("parallel", "parallel", "arbitrary")),
        name="fox_attend",
    )(proj3, proj3, proj3, cum, cum_t)


def _memkv_kernel(m_ref, g_ref, w_ref, o_ref):
    x = m_ref[0]
    ms = jnp.mean(x * x, axis=-1, keepdims=True)
    hn = (x * lax.rsqrt(ms + RMS_EPS) * g_ref[...]).astype(BF16)
    o_ref[0] = _dot(hn, w_ref[...]).astype(BF16)


def _mem_kv(mem, g, w_bf16):
    b, m, d = mem.shape
    n = w_bf16.shape[1]
    return pl.pallas_call(
        _memkv_kernel,
        out_shape=jax.ShapeDtypeStruct((b, m, n), BF16),
        grid=(b,),
        in_specs=[pl.BlockSpec((1, m, d), lambda i: (i, 0, 0)),
                  pl.BlockSpec((1, d), lambda i: (0, 0)),
                  pl.BlockSpec((d, n), lambda i: (0, 0))],
        out_specs=pl.BlockSpec((1, m, n), lambda i: (i, 0, 0)),
        compiler_params=_cparams(("parallel",)),
        name="mem_kv",
    )(mem, g, w_bf16)


def _memattn_kernel(q_ref, kv_ref, o_ref):
    w = MEM_HEADS * MEM_HEAD_DIM
    for h in range(MEM_HEADS):
        lo, hi = h * MEM_HEAD_DIM, (h + 1) * MEM_HEAD_DIM
        s = _dot_nt(q_ref[0, :, lo:hi].astype(BF16), kv_ref[0, :, lo:hi]) * (MEM_HEAD_DIM ** -0.5)
        e = jnp.exp(s - jnp.max(s, axis=-1, keepdims=True))
        p = e / jnp.sum(e, axis=-1, keepdims=True)
        o_ref[0, :, lo:hi] = _dot(p.astype(BF16), kv_ref[0, :, w + lo:w + hi])


def _mem_attend(proj3, kv, tq=512):
    b, s, _ = proj3.shape
    m = kv.shape[1]
    w = MEM_HEADS * MEM_HEAD_DIM
    return pl.pallas_call(
        _memattn_kernel,
        out_shape=jax.ShapeDtypeStruct((b, s, w), F32),
        grid=(b, s // tq),
        in_specs=[pl.BlockSpec((1, tq, w), lambda i, j: (i, j, C_QM // w)),
                  pl.BlockSpec((1, m, 2 * w), lambda i, j: (i, 0, 0))],
        out_specs=pl.BlockSpec((1, tq, w), lambda i, j: (i, j, 0)),
        compiler_params=_cparams(("parallel", "parallel")),
        name="mem_attend",
    )(proj3, kv)


def _merge_kernel(on_ref, of_ref, om_ref, ln_ref, lf_ref, lm_ref, bm_ref, wn_ref, wf_ref, wm_ref,
                  wo_ref, x_ref, g_ref, wr_ref, br_ref, x1_ref, hn_ref, idx_ref, gate_ref):
    d = x_ref.shape[1]

    def branch(o_ref, l_ref, w_ref, k):
        gate = jax.nn.sigmoid(l_ref[...] + bm_ref[:, k * d:(k + 1) * d])
        return gate * _dot(o_ref[...].astype(BF16), w_ref[...])

    merged = branch(on_ref, ln_ref, wn_ref, 0) + branch(of_ref, lf_ref, wf_ref, 1) + branch(om_ref, lm_ref, wm_ref, 2)
    x1 = x_ref[...] + _dot(merged.astype(BF16), wo_ref[...])
    x1_ref[...] = x1
    ms = jnp.mean(x1 * x1, axis=-1, keepdims=True)
    hn = x1 * lax.rsqrt(ms + RMS_EPS) * g_ref[...]
    hn_ref[...] = hn
    h_hi = hn.astype(BF16)
    h_lo = (hn - h_hi.astype(F32)).astype(BF16)
    wr = wr_ref[...]
    w_hi = wr.astype(BF16)
    w_lo = (wr - w_hi.astype(F32)).astype(BF16)
    logits = (_dot(h_hi, w_hi) + (_dot(h_lo, w_hi) + _dot(h_hi, w_lo))) + br_ref[...]
    tm, ne = logits.shape
    lane_e = lax.broadcasted_iota(jnp.int32, (tm, ne), 1)
    lane = lax.broadcasted_iota(jnp.int32, (tm, LANES), 1)
    idx_slab = jnp.zeros((tm, LANES), jnp.int32)
    val_slab = jnp.zeros((tm, LANES), F32)
    work = logits
    vals = []
    for k in range(TOP_K):
        m = jnp.max(work, axis=-1, keepdims=True)
        idx = jnp.min(jnp.where(work == m, lane_e, ne), axis=-1, keepdims=True)
        work = jnp.where(lane_e == idx, -jnp.inf, work)
        idx_slab = jnp.where(lane == k, idx, idx_slab)
        vals.append(m)
    es = [jnp.exp(v - vals[0]) for v in vals]
    tot = es[0]
    for e in es[1:]:
        tot = tot + e
    for k in range(TOP_K):
        val_slab = jnp.where(lane == k, es[k] / tot, val_slab)
    idx_ref[...] = idx_slab
    gate_ref[...] = val_slab


def _merge_route(o_nsa, o_fox, o_mem, proj, b_merge, wn, wf, wm, wo, x2, g_ffn, w_router, b_router, tm=256):
    t, d = x2.shape
    wb = o_nsa.shape[1]
    row = lambda w: pl.BlockSpec((tm, w), lambda i: (i, 0))
    full = lambda a: pl.BlockSpec(a.shape, lambda i: (0,) * a.ndim)
    return pl.pallas_call(
        _merge_kernel,
        out_shape=(jax.ShapeDtypeStruct((t, d), F32), jax.ShapeDtypeStruct((t, d), F32),
                   jax.ShapeDtypeStruct((t, LANES), jnp.int32), jax.ShapeDtypeStruct((t, LANES), F32)),
        grid=(t // tm,),
        in_specs=[row(wb), row(wb), row(wb),
                  pl.BlockSpec((tm, d), lambda i: (i, 0)),
                  pl.BlockSpec((tm, d), lambda i: (i, 1)),
                  pl.BlockSpec((tm, d), lambda i: (i, 2)),
                  full(b_merge), full(wn), full(wf), full(wm), full(wo),
                  row(d), full(g_ffn), full(w_router), full(b_router)],
        out_specs=(row(d), row(d), row(LANES), row(LANES)),
        compiler_params=_cparams(("parallel",)),
        name="merge_route",
    )(o_nsa, o_fox, o_mem, proj, proj, proj, b_merge, wn, wf, wm, wo, x2, g_ffn, w_router, b_router)


def _rank_kernel(idx_ref, rank_ref, cnt_ref, carry_ref, *, tm):
    @pl.when(pl.program_id(0) == 0)
    def _():
        carry_ref[...] = jnp.zeros(carry_ref.shape, F32)

    idx = idx_ref[...]
    lane_e = lax.broadcasted_iota(jnp.int32, (tm, N_EXPERTS), 1)
    hots = [jnp.where(idx[:, k:k + 1] == lane_e, 1.0, 0.0) for k in range(TOP_K)]
    cnt = hots[0]
    for hk in hots[1:]:
        cnt = cnt + hk
    r = lax.broadcasted_iota(jnp.int32, (tm, tm), 0)
    c = lax.broadcasted_iota(jnp.int32, (tm, tm), 1)
    strict = jnp.where(c < r, 1.0, 0.0).astype(BF16)
    before = _dot(strict, cnt.astype(BF16)) + carry_ref[...]
    lane = lax.broadcasted_iota(jnp.int32, (tm, LANES), 1)
    slab = jnp.zeros((tm, LANES), F32)
    for k in range(TOP_K):
        slab = jnp.where(lane == k, jnp.sum(hots[k] * before, axis=-1, keepdims=True), slab)
    rank_ref[...] = slab.astype(jnp.int32)
    total = carry_ref[...] + jnp.sum(cnt, axis=0, keepdims=True)
    carry_ref[...] = total
    cnt_ref[...] = jnp.broadcast_to(total, cnt_ref.shape).astype(jnp.int32)


def _moe_rank(idx_slab, tm=256):
    t = idx_slab.shape[0]
    return pl.pallas_call(
        functools.partial(_rank_kernel, tm=tm),
        out_shape=(jax.ShapeDtypeStruct((t, LANES), jnp.int32),
                   jax.ShapeDtypeStruct((8, N_EXPERTS), jnp.int32)),
        grid=(t // tm,),
        in_specs=[pl.BlockSpec((tm, LANES), lambda i: (i, 0))],
        out_specs=(pl.BlockSpec((tm, LANES), lambda i: (i, 0)),
                   pl.BlockSpec((8, N_EXPERTS), lambda i: (0, 0))),
        scratch_shapes=[pltpu.VMEM((1, N_EXPERTS), F32)],
        compiler_params=_cparams(("arbitrary",)),
        name="moe_rank",
    )(idx_slab)


def _dispatch_kernel(dest_ref, fill_ref, hn_ref, xs_ref, zero_ref, sem, *, tm, fm, unroll):
    i = pl.program_id(0)
    n = pl.num_programs(0)

    @pl.when(i == 0)
    def _():
        zero_ref[...] = jnp.zeros(zero_ref.shape, zero_ref.dtype)

    def tok_copy(tok, slot, s):
        return pltpu.make_async_copy(hn_ref.at[pl.ds(tok, 1)], xs_ref.at[pl.ds(slot, 1)], sem.at[s])

    def pad_copy(slot, s):
        return pltpu.make_async_copy(zero_ref, xs_ref.at[pl.ds(slot, 1)], sem.at[s])

    def wait_step(s):
        def w_tok(r, c):
            tok_copy(0, 0, s).wait()
            return c

        lax.fori_loop(0, tm * TOP_K, w_tok, 0, unroll=unroll)

        def w_pad(r, c):
            pad_copy(0, s).wait()
            return c

        lax.fori_loop(0, fm, w_pad, 0, unroll=unroll)

    cur = i % 2

    def issue_tok(r, c):
        tok = i * tm + r
        for k in range(TOP_K):
            tok_copy(tok, dest_ref[tok * TOP_K + k], cur).start()
        return c

    lax.fori_loop(0, tm, issue_tok, 0, unroll=unroll)

    def issue_pad(r, c):
        pad_copy(fill_ref[i * fm + r], cur).start()
        return c

    lax.fori_loop(0, fm, issue_pad, 0, unroll=unroll)

    @pl.when(i > 0)
    def _():
        wait_step(1 - cur)

    @pl.when(i == n - 1)
    def _():
        wait_step(cur)


def _moe_dispatch(dest_flat, fill_slots, hn, n_slots, tm=256):
    t, d = hn.shape
    steps = t // tm
    fm = fill_slots.shape[0] // steps
    assert fm * steps == fill_slots.shape[0]
    return pl.pallas_call(
        functools.partial(_dispatch_kernel, tm=tm, fm=fm, unroll=8),
        out_shape=jax.ShapeDtypeStruct((n_slots, d), hn.dtype),
        grid_spec=pltpu.PrefetchScalarGridSpec(
            num_scalar_prefetch=2,
            grid=(steps,),
            in_specs=[pl.BlockSpec(memory_space=pl.ANY)],
            out_specs=pl.BlockSpec(memory_space=pl.ANY),
            scratch_shapes=[pltpu.VMEM((1, d), hn.dtype), pltpu.SemaphoreType.DMA((2,))]),
        compiler_params=_cparams(("arbitrary",)),
        name="moe_dispatch",
    )(dest_flat, fill_slots, hn)


def _expert_kernel(be_ref, nu_ref, x_ref, wg_ref, wu_ref, bg_ref, bu_ref, wd_ref, bd_ref, o_ref):
    @pl.when(pl.program_id(0) < nu_ref[0])
    def _():
        x = x_ref[...].astype(BF16)
        g = jnp.minimum(_dot(x, wg_ref[0]) + bg_ref[0], SWIGLU_LIMIT)
        u = jnp.clip(_dot(x, wu_ref[0]) + bu_ref[0], -SWIGLU_LIMIT, SWIGLU_LIMIT)
        act = (u + 1.0) * g * jax.nn.sigmoid(SWIGLU_ALPHA * g)
        o_ref[...] = _dot(act.astype(BF16), wd_ref[0]) + bd_ref[0]

    @pl.when(pl.program_id(0) >= nu_ref[0])
    def _():
        o_ref[...] = jnp.zeros(o_ref.shape, o_ref.dtype)


def _moe_experts(block_exp, n_used, xs, wg, wu, bg, bu, wd, bd):
    n_slots, d = xs.shape
    f = wg.shape[2]
    nblk = n_slots // MOE_ROWS
    wspec = lambda shp: pl.BlockSpec((1,) + shp, lambda i, be, nu: (be[i], 0, 0))
    return pl.pallas_call(
        _expert_kernel,
        out_shape=jax.ShapeDtypeStruct((n_slots, d), F32),
        grid_spec=pltpu.PrefetchScalarGridSpec(
            num_scalar_prefetch=2,
            grid=(nblk,),
            in_specs=[pl.BlockSpec((MOE_ROWS, d), lambda i, be, nu: (i, 0)),
                      wspec((d, f)), wspec((d, f)), wspec((1, f)), wspec((1, f)),
                      wspec((f, d)), wspec((1, d))],
            out_specs=pl.BlockSpec((MOE_ROWS, d), lambda i, be, nu: (i, 0))),
        compiler_params=_cparams(("arbitrary",)),
        name="moe_experts",
    )(block_exp, n_used, xs, wg, wu, bg, bu, wd, bd)


def _combine_kernel(dest_ref, gate_ref, x1_ref, g_ref, ys_ref, o_ref, buf_ref, sem, *, tm, unroll):
    i = pl.program_id(0)
    n = pl.num_programs(0)

    def row_copy(slot_row, k, r, s):
        return pltpu.make_async_copy(ys_ref.at[pl.ds(slot_row, 1)], buf_ref.at[s, k, pl.ds(r, 1)], sem.at[s])

    def issue(step, s):
        base = step * (tm * TOP_K)

        def body(r, c):
            for k in range(TOP_K):
                row_copy(dest_ref[base + r * TOP_K + k], k, r, s).start()
            return c

        lax.fori_loop(0, tm, body, 0, unroll=unroll)

    @pl.when(i == 0)
    def _():
        issue(0, 0)

    @pl.when(i + 1 < n)
    def _():
        issue(i + 1, (i + 1) % 2)

    cur = i % 2

    def wait(r, c):
        row_copy(0, 0, 0, cur).wait()
        return c

    lax.fori_loop(0, tm * TOP_K, wait, 0, unroll=unroll)

    gate = gate_ref[...]
    y = x1_ref[...]
    for k in range(TOP_K):
        y = y + gate[:, k:k + 1] * buf_ref[cur, k]
    ms = jnp.mean(y * y, axis=-1, keepdims=True)
    o_ref[...] = y * lax.rsqrt(ms + RMS_EPS) * g_ref[...]


def _moe_combine(dest_flat, gate_slab, x1, g_final, ys, tm=128):
    t, d = x1.shape
    return pl.pallas_call(
        functools.partial(_combine_kernel, tm=tm, unroll=8),
        out_shape=jax.ShapeDtypeStruct((t, d), F32),
        grid_spec=pltpu.PrefetchScalarGridSpec(
            num_scalar_prefetch=1,
            grid=(t // tm,),
            in_specs=[pl.BlockSpec((tm, LANES), lambda i, dst: (i, 0)),
                      pl.BlockSpec((tm, d), lambda i, dst: (i, 0)),
                      pl.BlockSpec((1, d), lambda i, dst: (0, 0)),
                      pl.BlockSpec(memory_space=pl.ANY)],
            out_specs=pl.BlockSpec((tm, d), lambda i, dst: (i, 0)),
            scratch_shapes=[pltpu.VMEM((2, TOP_K, tm, d), F32), pltpu.SemaphoreType.DMA((2,))]),
        compiler_params=_cparams(("arbitrary",)),
        name="moe_combine",
    )(dest_flat, gate_slab, x1, g_final, ys)


def _rope_tables(s):
    inv = ROPE_THETA ** (-jnp.arange(0, HEAD_DIM, 2, dtype=F32) / HEAD_DIM)
    ang = jnp.arange(s, dtype=F32)[:, None] * inv[None, :]
    cos, sin = jnp.cos(ang), jnp.sin(ang)
    return jnp.concatenate([cos, cos], axis=-1), jnp.concatenate([-sin, sin], axis=-1)


def _permute_w_in(w):
    d = w.shape[0]
    pad = N_PROJ - (C_SMALL + 32)
    cols = [w[:, 3360:6432], w[:, 0:512], w[:, 1304:1816], w[:, 1816:2328], w[:, 2328:2840],
            w[:, 2848:3360], w[:, 512:1280], w[:, 1280:1304], w[:, 2840:2848], jnp.zeros((d, pad), w.dtype)]
    return jnp.concatenate(cols, axis=1).astype(BF16)


def _layer(x, mem, g_mix, w_in, b_forget, b_merge, pe_k, w1_k, w2_k, pe_v, w1_v, w2_v, g_mem, w_mem_kv,
           w_br_nsa, w_br_fox, w_br_mem, w_out, g_ffn, w_router, b_router, w_gate_up, b_gate_up,
           w_down, b_down, g_final):
    b, s, d = x.shape
    t = b * s
    x2 = x.reshape(t, d)
    cos64, sin64 = _rope_tables(s)
    cos2 = jnp.concatenate([cos64, cos64], axis=-1)
    sin2 = jnp.concatenate([sin64, sin64], axis=-1)

    proj = _inproj(x2, g_mix.reshape(1, d), _permute_w_in(w_in), tm=min(1024, t))
    proj3 = proj.reshape(b, s, N_PROJ)
    b_row = jnp.zeros((1, LANES), F32).at[0, FL_OFF:FL_OFF + FOX_HEADS].set(b_forget)
    cum, cum_t = _fox_cum(proj3, b_row)

    nb = s // CMP_STRIDE
    wide = CMP_STRIDE * HEAD_DIM

    def to_blocks(c0):
        a = proj3[:, :, c0:c0 + NSA_GROUPS * HEAD_DIM].reshape(b, nb, CMP_STRIDE, NSA_GROUPS, HEAD_DIM)
        return a.transpose(0, 3, 1, 2, 4).reshape(b, NSA_GROUPS, nb, wide)

    kcmp, vcmp = _nsa_compress(
        to_blocks(C_NSAKV), to_blocks(C_NSAKV + LANES), cos64.reshape(nb, wide), sin64.reshape(nb, wide),
        pe_k.reshape(2, wide), pe_v.reshape(2, wide),
        w1_k.reshape(CMP_LEN * HEAD_DIM, HEAD_DIM).astype(BF16), w1_v.reshape(CMP_LEN * HEAD_DIM, HEAD_DIM).astype(BF16),
        w2_k.astype(BF16), w2_v.astype(BF16))
    oc, sel = _nsa_select(proj3, cos2, sin2, kcmp, vcmp)
    o_nsa = _nsa_attend(proj3, cos2, sin2, sel, oc)
    o_fox = _fox_attend(proj3, cum, cum_t)
    mem_kv = _mem_kv(mem, g_mem.reshape(1, d), w_mem_kv.astype(BF16))
    o_mem = _mem_attend(proj3, mem_kv)

    x1, hn, idx_slab, gate_slab = _merge_route(
        o_nsa.reshape(t, -1), o_fox.reshape(t, -1), o_mem.reshape(t, -1), proj, b_merge.reshape(1, -1),
        w_br_nsa.astype(BF16), w_br_fox.astype(BF16), w_br_mem.astype(BF16), w_out.astype(BF16),
        x2, g_ffn.reshape(1, d), w_router, b_router.reshape(1, -1))

    rank_slab, cnt8 = _moe_rank(idx_slab)
    counts = cnt8[0]
    padded = (counts + MOE_ROWS - 1) // MOE_ROWS * MOE_ROWS
    pad_end = jnp.cumsum(padded)
    pad_start = pad_end - padded
    n_assign = t * TOP_K
    nblk = -(-(n_assign + N_EXPERTS * (MOE_ROWS - 1)) // MOE_ROWS)
    n_slots = nblk * MOE_ROWS
    top_idx = idx_slab[:, :TOP_K]
    dest = (pad_start[top_idx] + rank_slab[:, :TOP_K]).reshape(-1).astype(jnp.int32)
    n_fill = n_slots - n_assign
    seg_end = pad_start + counts
    gap = padded - counts
    gap_end = jnp.cumsum(gap)
    j = jnp.arange(n_fill, dtype=jnp.int32)
    e_of = jnp.minimum(jnp.searchsorted(gap_end, j, side='right'), N_EXPERTS - 1)
    in_gap = j < gap_end[-1]
    fill = jnp.where(in_gap, seg_end[e_of] + (j - (gap_end - gap)[e_of]), pad_end[-1] + (j - gap_end[-1]))
    block_exp = jnp.minimum(jnp.searchsorted(pad_end, jnp.arange(nblk) * MOE_ROWS, side='right'),
                            N_EXPERTS - 1).astype(jnp.int32)
    n_used = (pad_end[-1] // MOE_ROWS).astype(jnp.int32).reshape(1)

    xs = _moe_dispatch(dest, fill.astype(jnp.int32), hn, n_slots)
    f = w_down.shape[1]
    wgu = w_gate_up.reshape(N_EXPERTS, d, f, 2)
    bgu = b_gate_up.reshape(N_EXPERTS, 1, f, 2)
    ys = _moe_experts(block_exp, n_used, xs, wgu[..., 0].astype(BF16), wgu[..., 1].astype(BF16),
                      bgu[..., 0], bgu[..., 1], w_down.astype(BF16), b_down.reshape(N_EXPERTS, 1, d))
    out = _moe_combine(dest, gate_slab, x1, g_final.reshape(1, d), ys)
    return out.reshape(b, s, d)


def kernel(x, mem, g_mix, w_in, b_forget, b_merge, nsa_pe_k, nsa_w1_k, nsa_w2_k, nsa_pe_v, nsa_w1_v, nsa_w2_v, g_mem, w_mem_kv, w_branch_nsa, w_branch_fox, w_branch_mem, w_out, g_ffn, w_router, b_router, w_gate_up, b_gate_up, w_down, b_down, g_final):
    assert g_mix.shape[0] == 1, "single-layer trunk"
    return _layer(x, mem, g_mix[0], w_in[0], b_forget[0], b_merge[0], nsa_pe_k[0], nsa_w1_k[0], nsa_w2_k[0],
                  nsa_pe_v[0], nsa_w1_v[0], nsa_w2_v[0], g_mem[0], w_mem_kv[0], w_branch_nsa[0],
                  w_branch_fox[0], w_branch_mem[0], w_out[0], g_ffn[0], w_router[0], b_router[0],
                  w_gate_up[0], b_gate_up[0], w_down[0], b_down[0], g_final)
```

```python
import functools

import jax
import jax.numpy as jnp
import numpy as np
from jax import lax
from jax.experimental import pallas as pl
from jax.experimental.pallas import tpu as pltpu

F32 = jnp.float32
BF16 = jnp.bfloat16

D_MODEL = 1024
HEAD_DIM = 64
ROPE_THETA = 10000.0
RMS_EPS = 1e-5
NEG_INF = -1e30
FORCE_SCORE = 1e4
NSA_HEADS = 8
NSA_GROUPS = 2
NSA_REP = NSA_HEADS // NSA_GROUPS
CMP_LEN = 32
CMP_STRIDE = 16
SEL_BLOCK = 64
SEL_TOPK = 16
WINDOW = 512
FOX_HEADS = 8
MEM_HEADS = 4
MEM_HEAD_DIM = 128
N_EXPERTS = 32
TOP_K = 4
SWIGLU_LIMIT = 7.0
SWIGLU_ALPHA = 1.702

LANES = 128
VMEM_LIMIT = 48 * 1024 * 1024

C_MERGE = 0
C_QA = 3072
C_QF = 3584
C_KF = 4096
C_VF = 4608
C_QM = 5120
C_NSAKV = 5632
C_SMALL = 6400
N_PROJ = 6656
GL_OFF = 0
FL_OFF = 24

MOE_ROWS = 256


def _cparams(sem, vmem=VMEM_LIMIT):
    return pltpu.CompilerParams(dimension_semantics=sem, vmem_limit_bytes=vmem)


def _dot(a, b):
    return jnp.dot(a, b, preferred_element_type=F32)


def _dot_nt(a, b):
    return lax.dot_general(a, b, (((1,), (1,)), ((), ())), preferred_element_type=F32)


def _rope(x, cos, sin_signed):
    w = x.shape[-1]
    lane = lax.broadcasted_iota(jnp.int32, x.shape, x.ndim - 1)
    first = (lane & (HEAD_DIM - 1)) < (HEAD_DIM // 2)
    rot = jnp.where(first, pltpu.roll(x, w - HEAD_DIM // 2, x.ndim - 1),
                    pltpu.roll(x, HEAD_DIM // 2, x.ndim - 1))
    return x * cos + rot * sin_signed


def _split3(x):
    hi = x.astype(BF16)
    r1 = x - hi.astype(F32)
    mid = r1.astype(BF16)
    lo = (r1 - mid.astype(F32)).astype(BF16)
    return hi, mid, lo


def _inproj_kernel(x_ref, g_ref, w_ref, o_ref, hn_ref):
    @pl.when(pl.program_id(1) == 0)
    def _():
        x = x_ref[...]
        ms = jnp.mean(x * x, axis=-1, keepdims=True)
        hn_ref[...] = (x * lax.rsqrt(ms + RMS_EPS) * g_ref[...]).astype(BF16)

    o_ref[...] = _dot(hn_ref[...], w_ref[...])


def _inproj(x2, g, w_bf16, tm=1024, tn=512):
    t, d = x2.shape
    n = w_bf16.shape[1]
    return pl.pallas_call(
        _inproj_kernel,
        out_shape=jax.ShapeDtypeStruct((t, n), F32),
        grid=(t // tm, n // tn),
        in_specs=[pl.BlockSpec((tm, d), lambda i, j: (i, 0)),
                  pl.BlockSpec((1, d), lambda i, j: (0, 0)),
                  pl.BlockSpec((d, tn), lambda i, j: (0, j))],
        out_specs=pl.BlockSpec((tm, tn), lambda i, j: (i, j)),
        scratch_shapes=[pltpu.VMEM((tm, d), BF16)],
        compiler_params=_cparams(("parallel", "arbitrary")),
        name="inproj",
    )(x2, g, w_bf16)


def _cum_kernel(s_ref, b_ref, c_ref, ct_ref, *, blk):
    s = s_ref.shape[1]
    z = s_ref[0] + b_ref[...]
    logf = jnp.minimum(z, 0.0) - jnp.log1p(jnp.exp(-jnp.abs(z)))
    r = lax.broadcasted_iota(jnp.int32, (blk, blk), 0)
    c = lax.broadcasted_iota(jnp.int32, (blk, blk), 1)
    tri = jnp.where(c <= r, 1.0, 0.0).astype(BF16)
    carry = jnp.zeros((1, LANES), F32)
    for i in range(s // blk):
        hi, mid, lo = _split3(logf[i * blk:(i + 1) * blk])
        loc = (_dot(tri, hi) + _dot(tri, mid)) + _dot(tri, lo)
        out = loc + carry
        c_ref[0, i * blk:(i + 1) * blk, :] = out
        carry = out[blk - 1:blk, :]
    ct_ref[0] = c_ref[0].T


def _fox_cum(proj3, b_row, blk=256):
    b, s, _ = proj3.shape
    return pl.pallas_call(
        functools.partial(_cum_kernel, blk=blk),
        out_shape=(jax.ShapeDtypeStruct((b, s, LANES), F32),
                   jax.ShapeDtypeStruct((b, LANES, s), F32)),
        grid=(b,),
        in_specs=[pl.BlockSpec((1, s, LANES), lambda i: (i, 0, C_SMALL // LANES)),
                  pl.BlockSpec((1, LANES), lambda i: (0, 0))],
        out_specs=(pl.BlockSpec((1, s, LANES), lambda i: (i, 0, 0)),
                   pl.BlockSpec((1, LANES, s), lambda i: (i, 0, 0))),
        compiler_params=_cparams(("parallel",)),
        name="fox_cum",
    )(proj3, b_row)


def _cmp_kernel(ak_ref, av_ref, cos_ref, sin_ref, pek_ref, pev_ref, w1k_ref, w1v_ref,
                w2k_ref, w2v_ref, kc_ref, vc_ref):
    nb = ak_ref.shape[2]
    half = ak_ref.shape[3]

    def mlp(a, pe_ref, w1_ref, w2_ref):
        pa = _dot((a + pe_ref[0:1, :]).astype(BF16), w1_ref[0:half, :])
        pb = _dot((a + pe_ref[1:2, :]).astype(BF16), w1_ref[half:2 * half, :])
        z = pa + pltpu.roll(pb, nb - 1, 0)
        h = z * jax.nn.sigmoid(z)
        return _dot(h.astype(BF16), w2_ref[...])

    kc_ref[0, 0] = mlp(_rope(ak_ref[0, 0], cos_ref[...], sin_ref[...]), pek_ref, w1k_ref, w2k_ref)
    vc_ref[0, 0] = mlp(av_ref[0, 0], pev_ref, w1v_ref, w2v_ref)


def _nsa_compress(ak, av, cos_a, sin_a, pek, pev, w1k, w1v, w2k, w2v):
    b, g, nb, wide = ak.shape
    blk4 = pl.BlockSpec((1, 1, nb, wide), lambda i, j: (i, j, 0, 0))
    full = lambda shp: pl.BlockSpec(shp, lambda i, j: (0,) * len(shp))
    out = pl.BlockSpec((1, 1, nb, HEAD_DIM), lambda i, j: (i, j, 0, 0))
    return pl.pallas_call(
        _cmp_kernel,
        out_shape=(jax.ShapeDtypeStruct((b, g, nb, HEAD_DIM), F32),) * 2,
        grid=(b, g),
        in_specs=[blk4, blk4, full(cos_a.shape), full(sin_a.shape), full(pek.shape), full(pev.shape),
                  full(w1k.shape), full(w1v.shape), full(w2k.shape), full(w2v.shape)],
        out_specs=(out, out),
        compiler_params=_cparams(("parallel", "parallel")),
        name="nsa_compress",
    )(ak, av, cos_a, sin_a, pek, pev, w1k, w1v, w2k, w2v)


def _nsa_sel_kernel(q_ref, cos_ref, sin_ref, kc_ref, vc_ref, sm_ref, oc_ref, sel_ref, *, tq, n_sel):
    qi = pl.program_id(1)
    nb = kc_ref.shape[2]
    scale = HEAD_DIM ** -0.5
    cos = cos_ref[...]
    sin = sin_ref[...]
    tpos = qi * tq + lax.broadcasted_iota(jnp.int32, (tq, nb), 0)
    ncol = lax.broadcasted_iota(jnp.int32, (tq, nb), 1)
    cmask = (ncol * CMP_STRIDE + (CMP_LEN - 1) <= tpos) & (ncol < nb - 1)
    on = lax.broadcasted_iota(jnp.int32, (nb, n_sel), 0) * CMP_STRIDE
    oj = lax.broadcasted_iota(jnp.int32, (nb, n_sel), 1)
    overlap = jnp.where((on < (oj + 1) * SEL_BLOCK) & (on + CMP_LEN > oj * SEL_BLOCK), 1.0, 0.0).astype(BF16)
    trow = qi * tq + lax.broadcasted_iota(jnp.int32, (tq, n_sel), 0)
    blk = lax.broadcasted_iota(jnp.int32, (tq, n_sel), 1)
    cur = trow // SEL_BLOCK
    forced = (blk == 0) | (blk == cur) | (blk == cur - 1)
    future = blk * SEL_BLOCK > trow
    gates = jax.nn.sigmoid(sm_ref[0])
    for g in range(NSA_GROUPS):
        kc = kc_ref[0, g].astype(BF16)
        vc = vc_ref[0, g].astype(BF16)
        psum = jnp.zeros((tq, nb), F32)
        for r in range(NSA_REP):
            h = g * NSA_REP + r
            hp = h // 2
            q2 = _rope(q_ref[0, :, hp * LANES:(hp + 1) * LANES], cos, sin)
            qh = q2[:, (h % 2) * HEAD_DIM:(h % 2 + 1) * HEAD_DIM].astype(BF16)
            s = jnp.where(cmask, _dot_nt(qh, kc) * scale, NEG_INF)
            m = jnp.max(s, axis=-1, keepdims=True)
            e = jnp.where(cmask, jnp.exp(s - m), 0.0)
            p = e / jnp.maximum(jnp.sum(e, axis=-1, keepdims=True), 1e-30)
            psum = psum + p
            gate = gates[:, GL_OFF + 3 * h:GL_OFF + 3 * h + 1]
            oc_ref[0, :, h * HEAD_DIM:(h + 1) * HEAD_DIM] = gate * _dot(p.astype(BF16), vc)
        hi, mid, lo = _split3(psum)
        imp = (_dot(hi, overlap) + _dot(mid, overlap)) + _dot(lo, overlap)
        imp = jnp.where(forced, FORCE_SCORE, jnp.where(future, -FORCE_SCORE, imp))
        rank = jnp.zeros((tq, n_sel), F32)
        for i in range(n_sel):
            ci = imp[:, i:i + 1]
            ahead = (ci > imp) | ((ci == imp) & (blk > i))
            rank = rank + jnp.where(ahead, 1.0, 0.0)
        sel_ref[0, :, g * n_sel:(g + 1) * n_sel] = jnp.where(rank < float(min(SEL_TOPK, n_sel)), 1.0, 0.0)


def _nsa_select(proj3, cos2, sin2, kcmp, vcmp, tq=256):
    b, s, _ = proj3.shape
    nb = kcmp.shape[2]
    n_sel = s // SEL_BLOCK
    wq = NSA_HEADS * HEAD_DIM
    return pl.pallas_call(
        functools.partial(_nsa_sel_kernel, tq=tq, n_sel=n_sel),
        out_shape=(jax.ShapeDtypeStruct((b, s, wq), F32),
                   jax.ShapeDtypeStruct((b, s, NSA_GROUPS * n_sel), F32)),
        grid=(b, s // tq),
        in_specs=[pl.BlockSpec((1, tq, wq), lambda i, j: (i, j, C_QA // wq)),
                  pl.BlockSpec((tq, LANES), lambda i, j: (j, 0)),
                  pl.BlockSpec((tq, LANES), lambda i, j: (j, 0)),
                  pl.BlockSpec((1, NSA_GROUPS, nb, HEAD_DIM), lambda i, j: (i, 0, 0, 0)),
                  pl.BlockSpec((1, NSA_GROUPS, nb, HEAD_DIM), lambda i, j: (i, 0, 0, 0)),
                  pl.BlockSpec((1, tq, LANES), lambda i, j: (i, j, C_SMALL // LANES))],
        out_specs=(pl.BlockSpec((1, tq, wq), lambda i, j: (i, j, 0)),
                   pl.BlockSpec((1, tq, NSA_GROUPS * n_sel), lambda i, j: (i, j, 0))),
        compiler_params=_cparams(("parallel", "parallel")),
        name="nsa_select",
    )(proj3, cos2, sin2, kcmp, vcmp, proj3)


def _flash_step(s, mask, v, m_ref, l_ref, acc_ref, idx):
    s = jnp.where(mask, s, NEG_INF)
    m_old = m_ref[idx]
    m_new = jnp.maximum(m_old, jnp.max(s, axis=-1, keepdims=True))
    alpha = jnp.exp(m_old - m_new)
    p = jnp.where(mask, jnp.exp(s - m_new), 0.0)
    l_ref[idx] = alpha * l_ref[idx] + jnp.sum(p, axis=-1, keepdims=True)
    acc_ref[idx] = alpha * acc_ref[idx] + _dot(p.astype(BF16), v)
    m_ref[idx] = m_new


def _flash_out(l_ref, acc_ref, idx):
    return acc_ref[idx] / jnp.maximum(l_ref[idx], 1e-30)


def _nsa_attn_kernel(q_ref, cq_ref, sq_ref, ck_ref, sk_ref, ks_ref, vs_ref, kw_ref, vw_ref,
                     sel_ref, oc_ref, sm_ref, o_ref, qs_ref, m_ref, l_ref, acc_ref, *, tq, tk, n_sel):
    qi = pl.program_id(1)
    kj = pl.program_id(2)
    nh = NSA_HEADS
    n_prev = WINDOW // tk

    @pl.when(kj == 0)
    def _():
        for hp in range(nh // 2):
            q2 = _rope(q_ref[0, :, hp * LANES:(hp + 1) * LANES], cq_ref[...], sq_ref[...])
            qs_ref[:, hp * LANES:(hp + 1) * LANES] = (q2 * (HEAD_DIM ** -0.5)).astype(BF16)
        m_ref[...] = jnp.full(m_ref.shape, NEG_INF, F32)
        l_ref[...] = jnp.zeros(l_ref.shape, F32)
        acc_ref[...] = jnp.zeros(acc_ref.shape, F32)

    last = (qi * tq + tq - 1) // tk

    @pl.when(kj <= last)
    def _():
        qpos = qi * tq + lax.broadcasted_iota(jnp.int32, (tq, tk), 0)
        kpos = kj * tk + lax.broadcasted_iota(jnp.int32, (tq, tk), 1)
        causal = kpos <= qpos
        ks = _rope(ks_ref[0], ck_ref[...], sk_ref[...]).astype(BF16)
        vs = vs_ref[0].astype(BF16)
        er = lax.broadcasted_iota(jnp.int32, (n_sel, tk), 0)
        ec = kj * tk + lax.broadcasted_iota(jnp.int32, (n_sel, tk), 1)
        expand = jnp.where(er == ec // SEL_BLOCK, 1.0, 0.0).astype(BF16)
        for g in range(NSA_GROUPS):
            chosen = _dot(sel_ref[0, :, g * n_sel:(g + 1) * n_sel].astype(BF16), expand) > 0.5
            mask = chosen & causal
            kg = ks[:, g * HEAD_DIM:(g + 1) * HEAD_DIM]
            vg = vs[:, g * HEAD_DIM:(g + 1) * HEAD_DIM]
            for r in range(NSA_REP):
                h = g * NSA_REP + r
                s = _dot_nt(qs_ref[:, h * HEAD_DIM:(h + 1) * HEAD_DIM], kg)
                _flash_step(s, mask, vg, m_ref, l_ref, acc_ref, h)

    @pl.when((kj <= last) & (kj + n_prev >= (qi * tq) // tk))
    def _():
        qpos = qi * tq + lax.broadcasted_iota(jnp.int32, (tq, tk), 0)
        kpos = kj * tk + lax.broadcasted_iota(jnp.int32, (tq, tk), 1)
        diff = qpos - kpos
        mask = (diff >= 0) & (diff < WINDOW)
        kw = _rope(kw_ref[0], ck_ref[...], sk_ref[...]).astype(BF16)
        vw = vw_ref[0].astype(BF16)
        for g in range(NSA_GROUPS):
            kg = kw[:, g * HEAD_DIM:(g + 1) * HEAD_DIM]
            vg = vw[:, g * HEAD_DIM:(g + 1) * HEAD_DIM]
            for r in range(NSA_REP):
                h = g * NSA_REP + r
                s = _dot_nt(qs_ref[:, h * HEAD_DIM:(h + 1) * HEAD_DIM], kg)
                _flash_step(s, mask, vg, m_ref, l_ref, acc_ref, nh + h)

    @pl.when(kj == last)
    def _():
        gates = jax.nn.sigmoid(sm_ref[0])
        for h in range(nh):
            g_sel = gates[:, GL_OFF + 3 * h + 1:GL_OFF + 3 * h + 2]
            g_win = gates[:, GL_OFF + 3 * h + 2:GL_OFF + 3 * h + 3]
            o_ref[0, :, h * HEAD_DIM:(h + 1) * HEAD_DIM] = (
                oc_ref[0, :, h * HEAD_DIM:(h + 1) * HEAD_DIM]
                + g_sel * _flash_out(l_ref, acc_ref, h)
                + g_win * _flash_out(l_ref, acc_ref, nh + h))


def _nsa_attend(proj3, cos2, sin2, sel, oc, tq=256, tk=256):
    b, s, _ = proj3.shape
    n_sel = s // SEL_BLOCK
    wq = NSA_HEADS * HEAD_DIM
    kvb = C_NSAKV // LANES
    n_prev = WINDOW // tk

    def kv_sel(i, j, k):
        return jnp.minimum(k, (j * tq + tq - 1) // tk)

    def kv_win(i, j, k):
        return jnp.maximum(kv_sel(i, j, k), jnp.maximum((j * tq) // tk - n_prev, 0))

    return pl.pallas_call(
        functools.partial(_nsa_attn_kernel, tq=tq, tk=tk, n_sel=n_sel),
        out_shape=jax.ShapeDtypeStruct((b, s, wq), F32),
        grid=(b, s // tq, s // tk),
        in_specs=[pl.BlockSpec((1, tq, wq), lambda i, j, k: (i, j, C_QA // wq)),
                  pl.BlockSpec((tq, LANES), lambda i, j, k: (j, 0)),
                  pl.BlockSpec((tq, LANES), lambda i, j, k: (j, 0)),
                  pl.BlockSpec((tk, LANES), lambda i, j, k: (kv_sel(i, j, k), 0)),
                  pl.BlockSpec((tk, LANES), lambda i, j, k: (kv_sel(i, j, k), 0)),
                  pl.BlockSpec((1, tk, LANES), lambda i, j, k: (i, kv_sel(i, j, k), kvb + 2)),
                  pl.BlockSpec((1, tk, LANES), lambda i, j, k: (i, kv_sel(i, j, k), kvb + 3)),
                  pl.BlockSpec((1, tk, LANES), lambda i, j, k: (i, kv_win(i, j, k), kvb + 4)),
                  pl.BlockSpec((1, tk, LANES), lambda i, j, k: (i, kv_win(i, j, k), kvb + 5)),
                  pl.BlockSpec((1, tq, NSA_GROUPS * n_sel), lambda i, j, k: (i, j, 0)),
                  pl.BlockSpec((1, tq, wq), lambda i, j, k: (i, j, 0)),
                  pl.BlockSpec((1, tq, LANES), lambda i, j, k: (i, j, C_SMALL // LANES))],
        out_specs=pl.BlockSpec((1, tq, wq), lambda i, j, k: (i, j, 0)),
        scratch_shapes=[pltpu.VMEM((tq, wq), BF16),
                        pltpu.VMEM((2 * NSA_HEADS, tq, 1), F32),
                        pltpu.VMEM((2 * NSA_HEADS, tq, 1), F32),
                        pltpu.VMEM((2 * NSA_HEADS, tq, HEAD_DIM), F32)],
        compiler_params=_cparams(("parallel", "parallel", "arbitrary")),
        name="nsa_attend",
    )(proj3, cos2, sin2, cos2, sin2, proj3, proj3, proj3, proj3, sel, oc, proj3)


def _fox_kernel(q_ref, k_ref, v_ref, cq_ref, ck_ref, o_ref, qs_ref, m_ref, l_ref, acc_ref, *, tq, tk):
    qi = pl.program_id(1)
    kj = pl.program_id(2)

    @pl.when(kj == 0)
    def _():
        qs_ref[...] = (q_ref[0] * (HEAD_DIM ** -0.5)).astype(BF16)
        m_ref[...] = jnp.full(m_ref.shape, NEG_INF, F32)
        l_ref[...] = jnp.zeros(l_ref.shape, F32)
        acc_ref[...] = jnp.zeros(acc_ref.shape, F32)

    last = (qi * tq + tq - 1) // tk

    @pl.when(kj <= last)
    def _():
        qpos = qi * tq + lax.broadcasted_iota(jnp.int32, (tq, tk), 0)
        kpos = kj * tk + lax.broadcasted_iota(jnp.int32, (tq, tk), 1)
        mask = kpos <= qpos
        k = k_ref[0].astype(BF16)
        v = v_ref[0].astype(BF16)
        cq = cq_ref[0]
        ck = ck_ref[0]
        for h in range(FOX_HEADS):
            s = _dot_nt(qs_ref[:, h * HEAD_DIM:(h + 1) * HEAD_DIM], k[:, h * HEAD_DIM:(h + 1) * HEAD_DIM])
            s = s + cq[:, FL_OFF + h:FL_OFF + h + 1] - ck[FL_OFF + h:FL_OFF + h + 1, :]
            _flash_step(s, mask, v[:, h * HEAD_DIM:(h + 1) * HEAD_DIM], m_ref, l_ref, acc_ref, h)

    @pl.when(kj == last)
    def _():
        for h in range(FOX_HEADS):
            o_ref[0, :, h * HEAD_DIM:(h + 1) * HEAD_DIM] = _flash_out(l_ref, acc_ref, h)


def _fox_attend(proj3, cum, cum_t, tq=256, tk=512):
    b, s, _ = proj3.shape
    w = FOX_HEADS * HEAD_DIM

    def kv(i, j, k):
        return jnp.minimum(k, (j * tq + tq - 1) // tk)

    return pl.pallas_call(
        functools.partial(_fox_kernel, tq=tq, tk=tk),
        out_shape=jax.ShapeDtypeStruct((b, s, w), F32),
        grid=(b, s // tq, s // tk),
        in_specs=[pl.BlockSpec((1, tq, w), lambda i, j, k: (i, j, C_QF // w)),
                  pl.BlockSpec((1, tk, w), lambda i, j, k: (i, kv(i, j, k), C_KF // w)),
                  pl.BlockSpec((1, tk, w), lambda i, j, k: (i, kv(i, j, k), C_VF // w)),
                  pl.BlockSpec((1, tq, LANES), lambda i, j, k: (i, j, 0)),
                  pl.BlockSpec((1, LANES, tk), lambda i, j, k: (i, 0, kv(i, j, k)))],
        out_specs=pl.BlockSpec((1, tq, w), lambda i, j, k: (i, j, 0)),
        scratch_shapes=[pltpu.VMEM((tq, w), BF16),
                        pltpu.VMEM((FOX_HEADS, tq, 1), F32),
                        pltpu.VMEM((FOX_HEADS, tq, 1), F32),
                        pltpu.VMEM((FOX_HEADS, tq, HEAD_DIM), F32)],
        compiler_params=_cparams(("parallel", "parallel", "arbitrary")),
        name="fox_attend",
    )(proj3, proj3, proj3, cum, cum_t)


def _memkv_kernel(m_ref, g_ref, w_ref, o_ref):
    x = m_ref[0]
    ms = jnp.mean(x * x, axis=-1, keepdims=True)
    hn = (x * lax.rsqrt(ms + RMS_EPS) * g_ref[...]).astype(BF16)
    o_ref[0] = _dot(hn, w_ref[...]).astype(BF16)


def _mem_kv(mem, g, w_bf16):
    b, m, d = mem.shape
    n = w_bf16.shape[1]
    return pl.pallas_call(
        _memkv_kernel,
        out_shape=jax.ShapeDtypeStruct((b, m, n), BF16),
        grid=(b,),
        in_specs=[pl.BlockSpec((1, m, d), lambda i: (i, 0, 0)),
                  pl.BlockSpec((1, d), lambda i: (0, 0)),
                  pl.BlockSpec((d, n), lambda i: (0, 0))],
        out_specs=pl.BlockSpec((1, m, n), lambda i: (i, 0, 0)),
        compiler_params=_cparams(("parallel",)),
        name="mem_kv",
    )(mem, g, w_bf16)


def _memattn_kernel(q_ref, kv_ref, o_ref):
    w = MEM_HEADS * MEM_HEAD_DIM
    for h in range(MEM_HEADS):
        lo, hi = h * MEM_HEAD_DIM, (h + 1) * MEM_HEAD_DIM
        s = _dot_nt(q_ref[0, :, lo:hi].astype(BF16), kv_ref[0, :, lo:hi]) * (MEM_HEAD_DIM ** -0.5)
        e = jnp.exp(s - jnp.max(s, axis=-1, keepdims=True))
        p = e / jnp.sum(e, axis=-1, keepdims=True)
        o_ref[0, :, lo:hi] = _dot(p.astype(BF16), kv_ref[0, :, w + lo:w + hi])


def _mem_attend(proj3, kv, tq=512):
    b, s, _ = proj3.shape
    m = kv.shape[1]
    w = MEM_HEADS * MEM_HEAD_DIM
    return pl.pallas_call(
        _memattn_kernel,
        out_shape=jax.ShapeDtypeStruct((b, s, w), F32),
        grid=(b, s // tq),
        in_specs=[pl.BlockSpec((1, tq, w), lambda i, j: (i, j, C_QM // w)),
                  pl.BlockSpec((1, m, 2 * w), lambda i, j: (i, 0, 0))],
        out_specs=pl.BlockSpec((1, tq, w), lambda i, j: (i, j, 0)),
        compiler_params=_cparams(("parallel", "parallel")),
        name="mem_attend",
    )(proj3, kv)


def _merge_kernel(on_ref, of_ref, om_ref, ln_ref, lf_ref, lm_ref, bm_ref, wn_ref, wf_ref, wm_ref,
                  wo_ref, x_ref, g_ref, wr_ref, br_ref, x1_ref, hn_ref, idx_ref, gate_ref):
    d = x_ref.shape[1]

    def branch(o_ref, l_ref, w_ref, k):
        gate = jax.nn.sigmoid(l_ref[...] + bm_ref[:, k * d:(k + 1) * d])
        return gate * _dot(o_ref[...].astype(BF16), w_ref[...])

    merged = branch(on_ref, ln_ref, wn_ref, 0) + branch(of_ref, lf_ref, wf_ref, 1) + branch(om_ref, lm_ref, wm_ref, 2)
    x1 = x_ref[...] + _dot(merged.astype(BF16), wo_ref[...])
    x1_ref[...] = x1
    ms = jnp.mean(x1 * x1, axis=-1, keepdims=True)
    hn = x1 * lax.rsqrt(ms + RMS_EPS) * g_ref[...]
    hn_ref[...] = hn
    h_hi = hn.astype(BF16)
    h_lo = (hn - h_hi.astype(F32)).astype(BF16)
    wr = wr_ref[...]
    w_hi = wr.astype(BF16)
    w_lo = (wr - w_hi.astype(F32)).astype(BF16)
    logits = (_dot(h_hi, w_hi) + (_dot(h_lo, w_hi) + _dot(h_hi, w_lo))) + br_ref[...]
    tm, ne = logits.shape
    lane_e = lax.broadcasted_iota(jnp.int32, (tm, ne), 1)
    lane = lax.broadcasted_iota(jnp.int32, (tm, LANES), 1)
    idx_slab = jnp.zeros((tm, LANES), jnp.int32)
    val_slab = jnp.zeros((tm, LANES), F32)
    work = logits
    vals = []
    for k in range(TOP_K):
        m = jnp.max(work, axis=-1, keepdims=True)
        idx = jnp.min(jnp.where(work == m, lane_e, ne), axis=-1, keepdims=True)
        work = jnp.where(lane_e == idx, -jnp.inf, work)
        idx_slab = jnp.where(lane == k, idx, idx_slab)
        vals.append(m)
    es = [jnp.exp(v - vals[0]) for v in vals]
    tot = es[0]
    for e in es[1:]:
        tot = tot + e
    for k in range(TOP_K):
        val_slab = jnp.where(lane == k, es[k] / tot, val_slab)
    idx_ref[...] = idx_slab
    gate_ref[...] = val_slab


def _merge_route(o_nsa, o_fox, o_mem, proj, b_merge, wn, wf, wm, wo, x2, g_ffn, w_router, b_router, tm=256):
    t, d = x2.shape
    wb = o_nsa.shape[1]
    row = lambda w: pl.BlockSpec((tm, w), lambda i: (i, 0))
    full = lambda a: pl.BlockSpec(a.shape, lambda i: (0,) * a.ndim)
    return pl.pallas_call(
        _merge_kernel,
        out_shape=(jax.ShapeDtypeStruct((t, d), F32), jax.ShapeDtypeStruct((t, d), F32),
                   jax.ShapeDtypeStruct((t, LANES), jnp.int32), jax.ShapeDtypeStruct((t, LANES), F32)),
        grid=(t // tm,),
        in_specs=[row(wb), row(wb), row(wb),
                  pl.BlockSpec((tm, d), lambda i: (i, 0)),
                  pl.BlockSpec((tm, d), lambda i: (i, 1)),
                  pl.BlockSpec((tm, d), lambda i: (i, 2)),
                  full(b_merge), full(wn), full(wf), full(wm), full(wo),
                  row(d), full(g_ffn), full(w_router), full(b_router)],
        out_specs=(row(d), row(d), row(LANES), row(LANES)),
        compiler_params=_cparams(("parallel",)),
        name="merge_route",
    )(o_nsa, o_fox, o_mem, proj, proj, proj, b_merge, wn, wf, wm, wo, x2, g_ffn, w_router, b_router)


def _rank_kernel(idx_ref, rank_ref, cnt_ref, carry_ref, *, tm):
    @pl.when(pl.program_id(0) == 0)
    def _():
        carry_ref[...] = jnp.zeros(carry_ref.shape, F32)

    idx = idx_ref[...]
    lane_e = lax.broadcasted_iota(jnp.int32, (tm, N_EXPERTS), 1)
    hots = [jnp.where(idx[:, k:k + 1] == lane_e, 1.0, 0.0) for k in range(TOP_K)]
    cnt = hots[0]
    for hk in hots[1:]:
        cnt = cnt + hk
    r = lax.broadcasted_iota(jnp.int32, (tm, tm), 0)
    c = lax.broadcasted_iota(jnp.int32, (tm, tm), 1)
    strict = jnp.where(c < r, 1.0, 0.0).astype(BF16)
    before = _dot(strict, cnt.astype(BF16)) + carry_ref[...]
    lane = lax.broadcasted_iota(jnp.int32, (tm, LANES), 1)
    slab = jnp.zeros((tm, LANES), F32)
    for k in range(TOP_K):
        slab = jnp.where(lane == k, jnp.sum(hots[k] * before, axis=-1, keepdims=True), slab)
    rank_ref[...] = slab.astype(jnp.int32)
    total = carry_ref[...] + jnp.sum(cnt, axis=0, keepdims=True)
    carry_ref[...] = total
    cnt_ref[...] = jnp.broadcast_to(total, cnt_ref.shape).astype(jnp.int32)


def _moe_rank(idx_slab, tm=256):
    t = idx_slab.shape[0]
    return pl.pallas_call(
        functools.partial(_rank_kernel, tm=tm),
        out_shape=(jax.ShapeDtypeStruct((t, LANES), jnp.int32),
                   jax.ShapeDtypeStruct((8, N_EXPERTS), jnp.int32)),
        grid=(t // tm,),
        in_specs=[pl.BlockSpec((tm, LANES), lambda i: (i, 0))],
        out_specs=(pl.BlockSpec((tm, LANES), lambda i: (i, 0)),
                   pl.BlockSpec((8, N_EXPERTS), lambda i: (0, 0))),
        scratch_shapes=[pltpu.VMEM((1, N_EXPERTS), F32)],
        compiler_params=_cparams(("arbitrary",)),
        name="moe_rank",
    )(idx_slab)


def _expert_kernel(be_ref, nu_ref, tok_ref, hn_ref, wg_ref, wu_ref, bg_ref, bu_ref, wd_ref, bd_ref, o_ref,
                   buf_ref, sem, *, unroll):
    i = pl.program_id(0)
    n_used = nu_ref[0]

    def row_copy(tok, r, s):
        return pltpu.make_async_copy(hn_ref.at[pl.ds(tok, 1)], buf_ref.at[s, pl.ds(r, 1)], sem.at[s])

    def issue(step, s):
        base = step * MOE_ROWS

        def body(r, c):
            row_copy(tok_ref[base + r], r, s).start()
            return c

        lax.fori_loop(0, MOE_ROWS, body, 0, unroll=unroll)

    @pl.when((i == 0) & (n_used > 0))
    def _():
        issue(0, 0)

    @pl.when(i + 1 < n_used)
    def _():
        issue(i + 1, (i + 1) % 2)

    @pl.when(i < n_used)
    def _():
        cur = i % 2

        def wait(r, c):
            row_copy(0, 0, cur).wait()
            return c

        lax.fori_loop(0, MOE_ROWS, wait, 0, unroll=unroll)
        x = buf_ref[cur].astype(BF16)
        g = jnp.minimum(_dot(x, wg_ref[0]) + bg_ref[0], SWIGLU_LIMIT)
        u = jnp.clip(_dot(x, wu_ref[0]) + bu_ref[0], -SWIGLU_LIMIT, SWIGLU_LIMIT)
        act = (u + 1.0) * g * jax.nn.sigmoid(SWIGLU_ALPHA * g)
        o_ref[...] = _dot(act.astype(BF16), wd_ref[0]) + bd_ref[0]

    @pl.when(i >= n_used)
    def _():
        o_ref[...] = jnp.zeros(o_ref.shape, o_ref.dtype)


def _moe_experts(block_exp, n_used, slot_tok, hn, wg, wu, bg, bu, wd, bd):
    n_slots = slot_tok.shape[0]
    d = hn.shape[1]
    f = wg.shape[2]
    nblk = n_slots // MOE_ROWS
    wspec = lambda shp: pl.BlockSpec((1,) + shp, lambda i, be, nu, tk: (be[i], 0, 0))
    return pl.pallas_call(
        functools.partial(_expert_kernel, unroll=8),
        out_shape=jax.ShapeDtypeStruct((n_slots, d), F32),
        grid_spec=pltpu.PrefetchScalarGridSpec(
            num_scalar_prefetch=3,
            grid=(nblk,),
            in_specs=[pl.BlockSpec(memory_space=pl.ANY),
                      wspec((d, f)), wspec((d, f)), wspec((1, f)), wspec((1, f)),
                      wspec((f, d)), wspec((1, d))],
            out_specs=pl.BlockSpec((MOE_ROWS, d), lambda i, be, nu, tk: (i, 0)),
            scratch_shapes=[pltpu.VMEM((2, MOE_ROWS, d), hn.dtype), pltpu.SemaphoreType.DMA((2,))]),
        compiler_params=_cparams(("arbitrary",)),
        name="moe_experts",
    )(block_exp, n_used, slot_tok, hn, wg, wu, bg, bu, wd, bd)


def _combine_kernel(dest_ref, gate_ref, x1_ref, g_ref, ys_ref, o_ref, buf_ref, sem, *, tm, unroll):
    i = pl.program_id(0)
    n = pl.num_programs(0)

    def row_copy(slot_row, k, r, s):
        return pltpu.make_async_copy(ys_ref.at[pl.ds(slot_row, 1)], buf_ref.at[s, k, pl.ds(r, 1)], sem.at[s])

    def issue(step, s):
        base = step * (tm * TOP_K)

        def body(r, c):
            for k in range(TOP_K):
                row_copy(dest_ref[base + r * TOP_K + k], k, r, s).start()
            return c

        lax.fori_loop(0, tm, body, 0, unroll=unroll)

    @pl.when(i == 0)
    def _():
        issue(0, 0)

    @pl.when(i + 1 < n)
    def _():
        issue(i + 1, (i + 1) % 2)

    cur = i % 2

    def wait(r, c):
        row_copy(0, 0, 0, cur).wait()
        return c

    lax.fori_loop(0, tm * TOP_K, wait, 0, unroll=unroll)

    gate = gate_ref[...]
    y = x1_ref[...]
    for k in range(TOP_K):
        y = y + gate[:, k:k + 1] * buf_ref[cur, k]
    ms = jnp.mean(y * y, axis=-1, keepdims=True)
    o_ref[...] = y * lax.rsqrt(ms + RMS_EPS) * g_ref[...]


def _moe_combine(dest_flat, gate_slab, x1, g_final, ys, tm=128):
    t, d = x1.shape
    return pl.pallas_call(
        functools.partial(_combine_kernel, tm=tm, unroll=8),
        out_shape=jax.ShapeDtypeStruct((t, d), F32),
        grid_spec=pltpu.PrefetchScalarGridSpec(
            num_scalar_prefetch=1,
            grid=(t // tm,),
            in_specs=[pl.BlockSpec((tm, LANES), lambda i, dst: (i, 0)),
                      pl.BlockSpec((tm, d), lambda i, dst: (i, 0)),
                      pl.BlockSpec((1, d), lambda i, dst: (0, 0)),
                      pl.BlockSpec(memory_space=pl.ANY)],
            out_specs=pl.BlockSpec((tm, d), lambda i, dst: (i, 0)),
            scratch_shapes=[pltpu.VMEM((2, TOP_K, tm, d), F32), pltpu.SemaphoreType.DMA((2,))]),
        compiler_params=_cparams(("arbitrary",)),
        name="moe_combine",
    )(dest_flat, gate_slab, x1, g_final, ys)


def _rope_tables(s):
    inv = ROPE_THETA ** (-jnp.arange(0, HEAD_DIM, 2, dtype=F32) / HEAD_DIM)
    ang = jnp.arange(s, dtype=F32)[:, None] * inv[None, :]
    cos, sin = jnp.cos(ang), jnp.sin(ang)
    return jnp.concatenate([cos, cos], axis=-1), jnp.concatenate([-sin, sin], axis=-1)


def _permute_w_in(w):
    d = w.shape[0]
    pad = N_PROJ - (C_SMALL + 32)
    cols = [w[:, 3360:6432], w[:, 0:512], w[:, 1304:1816], w[:, 1816:2328], w[:, 2328:2840],
            w[:, 2848:3360], w[:, 512:1280], w[:, 1280:1304], w[:, 2840:2848], jnp.zeros((d, pad), w.dtype)]
    return jnp.concatenate(cols, axis=1).astype(BF16)


def _layer(x, mem, g_mix, w_in, b_forget, b_merge, pe_k, w1_k, w2_k, pe_v, w1_v, w2_v, g_mem, w_mem_kv,
           w_br_nsa, w_br_fox, w_br_mem, w_out, g_ffn, w_router, b_router, w_gate_up, b_gate_up,
           w_down, b_down, g_final):
    b, s, d = x.shape
    t = b * s
    x2 = x.reshape(t, d)
    cos64, sin64 = _rope_tables(s)
    cos2 = jnp.concatenate([cos64, cos64], axis=-1)
    sin2 = jnp.concatenate([sin64, sin64], axis=-1)

    proj = _inproj(x2, g_mix.reshape(1, d), _permute_w_in(w_in), tm=min(1024, t))
    proj3 = proj.reshape(b, s, N_PROJ)
    b_row = jnp.zeros((1, LANES), F32).at[0, FL_OFF:FL_OFF + FOX_HEADS].set(b_forget)
    cum, cum_t = _fox_cum(proj3, b_row)

    nb = s // CMP_STRIDE
    wide = CMP_STRIDE * HEAD_DIM

    def to_blocks(c0):
        a = proj3[:, :, c0:c0 + NSA_GROUPS * HEAD_DIM].reshape(b, nb, CMP_STRIDE, NSA_GROUPS, HEAD_DIM)
        return a.transpose(0, 3, 1, 2, 4).reshape(b, NSA_GROUPS, nb, wide)

    kcmp, vcmp = _nsa_compress(
        to_blocks(C_NSAKV), to_blocks(C_NSAKV + LANES), cos64.reshape(nb, wide), sin64.reshape(nb, wide),
        pe_k.reshape(2, wide), pe_v.reshape(2, wide),
        w1_k.reshape(CMP_LEN * HEAD_DIM, HEAD_DIM).astype(BF16), w1_v.reshape(CMP_LEN * HEAD_DIM, HEAD_DIM).astype(BF16),
        w2_k.astype(BF16), w2_v.astype(BF16))
    oc, sel = _nsa_select(proj3, cos2, sin2, kcmp, vcmp)
    o_nsa = _nsa_attend(proj3, cos2, sin2, sel, oc)
    o_fox = _fox_attend(proj3, cum, cum_t)
    mem_kv = _mem_kv(mem, g_mem.reshape(1, d), w_mem_kv.astype(BF16))
    o_mem = _mem_attend(proj3, mem_kv)

    x1, hn, idx_slab, gate_slab = _merge_route(
        o_nsa.reshape(t, -1), o_fox.reshape(t, -1), o_mem.reshape(t, -1), proj, b_merge.reshape(1, -1),
        w_br_nsa.astype(BF16), w_br_fox.astype(BF16), w_br_mem.astype(BF16), w_out.astype(BF16),
        x2, g_ffn.reshape(1, d), w_router, b_router.reshape(1, -1))

    rank_slab, cnt8 = _moe_rank(idx_slab)
    counts = cnt8[0]
    padded = (counts + MOE_ROWS - 1) // MOE_ROWS * MOE_ROWS
    pad_end = jnp.cumsum(padded)
    pad_start = pad_end - padded
    n_assign = t * TOP_K
    nblk = -(-(n_assign + N_EXPERTS * (MOE_ROWS - 1)) // MOE_ROWS)
    n_slots = nblk * MOE_ROWS
    top_idx = idx_slab[:, :TOP_K]
    dest = (pad_start[top_idx] + rank_slab[:, :TOP_K]).reshape(-1).astype(jnp.int32)
    tok_flat = jnp.repeat(jnp.arange(t, dtype=jnp.int32), TOP_K)
    slot_tok = jnp.zeros((n_slots,), jnp.int32).at[dest].set(tok_flat)
    blk_start = jnp.arange(nblk, dtype=jnp.int32) * MOE_ROWS
    block_exp = jnp.minimum(jnp.sum(blk_start[:, None] >= pad_end[None, :], axis=1),
                            N_EXPERTS - 1).astype(jnp.int32)
    n_used = (pad_end[-1] // MOE_ROWS).astype(jnp.int32).reshape(1)

    f = w_down.shape[1]
    wgu = w_gate_up.reshape(N_EXPERTS, d, f, 2)
    bgu = b_gate_up.reshape(N_EXPERTS, 1, f, 2)
    ys = _moe_experts(block_exp, n_used, slot_tok, hn, wgu[..., 0].astype(BF16), wgu[..., 1].astype(BF16),
                      bgu[..., 0], bgu[..., 1], w_down.astype(BF16), b_down.reshape(N_EXPERTS, 1, d))
    out = _moe_combine(dest, gate_slab, x1, g_final.reshape(1, d), ys)
    return out.reshape(b, s, d)


def kernel(x, mem, g_mix, w_in, b_forget, b_merge, nsa_pe_k, nsa_w1_k, nsa_w2_k, nsa_pe_v, nsa_w1_v, nsa_w2_v, g_mem, w_mem_kv, w_branch_nsa, w_branch_fox, w_branch_mem, w_out, g_ffn, w_router, b_router, w_gate_up, b_gate_up, w_down, b_down, g_final):
    assert g_mix.shape[0] == 1, "single-layer trunk"
    return _layer(x, mem, g_mix[0], w_in[0], b_forget[0], b_merge[0], nsa_pe_k[0], nsa_w1_k[0], nsa_w2_k[0],
                  nsa_pe_v[0], nsa_w1_v[0], nsa_w2_v[0], g_mem[0], w_mem_kv[0], w_branch_nsa[0],
                  w_branch_fox[0], w_branch_mem[0], w_out[0], g_ffn[0], w_router[0], b_router[0],
                  w_gate_up[0], b_gate_up[0], w_down[0], b_down[0], g_final)
```

```python
import functools

import jax
import jax.numpy as jnp
import numpy as np
from jax import lax
from jax.experimental import pallas as pl
from jax.experimental.pallas import tpu as pltpu

F32 = jnp.float32
BF16 = jnp.bfloat16

D_MODEL = 1024
HEAD_DIM = 64
ROPE_THETA = 10000.0
RMS_EPS = 1e-5
NEG_INF = -1e30
FORCE_SCORE = 1e4
NSA_HEADS = 8
NSA_GROUPS = 2
NSA_REP = NSA_HEADS // NSA_GROUPS
CMP_LEN = 32
CMP_STRIDE = 16
SEL_BLOCK = 64
SEL_TOPK = 16
WINDOW = 512
FOX_HEADS = 8
MEM_HEADS = 4
MEM_HEAD_DIM = 128
N_EXPERTS = 32
TOP_K = 4
SWIGLU_LIMIT = 7.0
SWIGLU_ALPHA = 1.702

LANES = 128
VMEM_LIMIT = 48 * 1024 * 1024

C_MERGE = 0
C_QA = 3072
C_QF = 3584
C_KF = 4096
C_VF = 4608
C_QM = 5120
C_NSAKV = 5632
C_SMALL = 6400
N_PROJ = 6656
GL_OFF = 0
FL_OFF = 24

MOE_ROWS = 256
NSA_TILE = 256
FOX_TILE = 512


def _cparams(sem, vmem=VMEM_LIMIT):
    return pltpu.CompilerParams(dimension_semantics=sem, vmem_limit_bytes=vmem)


def _dot(a, b):
    return jnp.dot(a, b, preferred_element_type=F32)


def _dot_nt(a, b):
    return lax.dot_general(a, b, (((1,), (1,)), ((), ())), preferred_element_type=F32)


def _rope(x, cos, sin_signed):
    w = x.shape[-1]
    lane = lax.broadcasted_iota(jnp.int32, x.shape, x.ndim - 1)
    first = (lane & (HEAD_DIM - 1)) < (HEAD_DIM // 2)
    rot = jnp.where(first, pltpu.roll(x, w - HEAD_DIM // 2, x.ndim - 1),
                    pltpu.roll(x, HEAD_DIM // 2, x.ndim - 1))
    return x * cos + rot * sin_signed


def _split3(x):
    hi = x.astype(BF16)
    r1 = x - hi.astype(F32)
    mid = r1.astype(BF16)
    lo = (r1 - mid.astype(F32)).astype(BF16)
    return hi, mid, lo


def _inproj_kernel(x_ref, g_ref, w_ref, o_ref, hn_ref):
    @pl.when(pl.program_id(1) == 0)
    def _():
        x = x_ref[...]
        ms = jnp.mean(x * x, axis=-1, keepdims=True)
        hn_ref[...] = (x * lax.rsqrt(ms + RMS_EPS) * g_ref[...]).astype(BF16)

    o_ref[...] = _dot(hn_ref[...], w_ref[...])


def _inproj(x2, g, w_bf16, tm=1024, tn=512):
    t, d = x2.shape
    n = w_bf16.shape[1]
    return pl.pallas_call(
        _inproj_kernel,
        out_shape=jax.ShapeDtypeStruct((t, n), F32),
        grid=(t // tm, n // tn),
        in_specs=[pl.BlockSpec((tm, d), lambda i, j: (i, 0)),
                  pl.BlockSpec((1, d), lambda i, j: (0, 0)),
                  pl.BlockSpec((d, tn), lambda i, j: (0, j))],
        out_specs=pl.BlockSpec((tm, tn), lambda i, j: (i, j)),
        scratch_shapes=[pltpu.VMEM((tm, d), BF16)],
        compiler_params=_cparams(("parallel", "arbitrary")),
        name="inproj",
    )(x2, g, w_bf16)


def _cum_kernel(s_ref, b_ref, c_ref, ct_ref, *, blk):
    s = s_ref.shape[1]
    z = s_ref[0] + b_ref[...]
    logf = jnp.minimum(z, 0.0) - jnp.log1p(jnp.exp(-jnp.abs(z)))
    r = lax.broadcasted_iota(jnp.int32, (blk, blk), 0)
    c = lax.broadcasted_iota(jnp.int32, (blk, blk), 1)
    tri = jnp.where(c <= r, 1.0, 0.0).astype(BF16)
    carry = jnp.zeros((1, LANES), F32)
    for i in range(s // blk):
        hi, mid, lo = _split3(logf[i * blk:(i + 1) * blk])
        loc = (_dot(tri, hi) + _dot(tri, mid)) + _dot(tri, lo)
        out = loc + carry
        c_ref[0, i * blk:(i + 1) * blk, :] = out
        carry = out[blk - 1:blk, :]
    ct_ref[0] = c_ref[0].T


def _fox_cum(proj3, b_row, blk=256):
    b, s, _ = proj3.shape
    return pl.pallas_call(
        functools.partial(_cum_kernel, blk=blk),
        out_shape=(jax.ShapeDtypeStruct((b, s, LANES), F32),
                   jax.ShapeDtypeStruct((b, LANES, s), F32)),
        grid=(b,),
        in_specs=[pl.BlockSpec((1, s, LANES), lambda i: (i, 0, C_SMALL // LANES)),
                  pl.BlockSpec((1, LANES), lambda i: (0, 0))],
        out_specs=(pl.BlockSpec((1, s, LANES), lambda i: (i, 0, 0)),
                   pl.BlockSpec((1, LANES, s), lambda i: (i, 0, 0))),
        compiler_params=_cparams(("parallel",)),
        name="fox_cum",
    )(proj3, b_row)


def _cmp_kernel(ak_ref, av_ref, cos_ref, sin_ref, pek_ref, pev_ref, w1k_ref, w1v_ref,
                w2k_ref, w2v_ref, kc_ref, vct_ref, vbuf_ref):
    nb = ak_ref.shape[2]
    half = ak_ref.shape[3]

    def mlp(a, pe_ref, w1_ref, w2_ref):
        pa = _dot((a + pe_ref[0:1, :]).astype(BF16), w1_ref[0:half, :])
        pb = _dot((a + pe_ref[1:2, :]).astype(BF16), w1_ref[half:2 * half, :])
        z = pa + pltpu.roll(pb, nb - 1, 0)
        h = z * jax.nn.sigmoid(z)
        return _dot(h.astype(BF16), w2_ref[...])

    for g in range(NSA_GROUPS):
        lo, hi = g * HEAD_DIM, (g + 1) * HEAD_DIM
        kc_ref[0, :, lo:hi] = mlp(_rope(ak_ref[0, g], cos_ref[...], sin_ref[...]),
                                  pek_ref, w1k_ref, w2k_ref).astype(BF16)
        vbuf_ref[:, lo:hi] = mlp(av_ref[0, g], pev_ref, w1v_ref, w2v_ref)
    vct_ref[0] = vbuf_ref[...].T.astype(BF16)


def _nsa_compress(ak, av, cos_a, sin_a, pek, pev, w1k, w1v, w2k, w2v):
    b, g, nb, wide = ak.shape
    blk4 = pl.BlockSpec((1, g, nb, wide), lambda i: (i, 0, 0, 0))
    full = lambda shp: pl.BlockSpec(shp, lambda i: (0,) * len(shp))
    return pl.pallas_call(
        _cmp_kernel,
        out_shape=(jax.ShapeDtypeStruct((b, nb, g * HEAD_DIM), BF16),
                   jax.ShapeDtypeStruct((b, g * HEAD_DIM, nb), BF16)),
        grid=(b,),
        in_specs=[blk4, blk4, full(cos_a.shape), full(sin_a.shape), full(pek.shape), full(pev.shape),
                  full(w1k.shape), full(w1v.shape), full(w2k.shape), full(w2v.shape)],
        out_specs=(pl.BlockSpec((1, nb, g * HEAD_DIM), lambda i: (i, 0, 0)),
                   pl.BlockSpec((1, g * HEAD_DIM, nb), lambda i: (i, 0, 0))),
        scratch_shapes=[pltpu.VMEM((nb, g * HEAD_DIM), F32)],
        compiler_params=_cparams(("parallel",)),
        name="nsa_compress",
    )(ak, av, cos_a, sin_a, pek, pev, w1k, w1v, w2k, w2v)


def _prep_kernel(kf_ref, vf_ref, ks_ref, vs_ref, kw_ref, vw_ref, cos_ref, sin_ref,
                 okf_ref, ovf_ref, oks_ref, ovs_ref, okw_ref, ovw_ref):
    okf_ref[0] = kf_ref[0].astype(BF16)
    ovf_ref[0, 0] = vf_ref[0].T.astype(BF16)
    oks_ref[0] = _rope(ks_ref[0], cos_ref[...], sin_ref[...]).astype(BF16)
    okw_ref[0] = _rope(kw_ref[0], cos_ref[...], sin_ref[...]).astype(BF16)
    tn = ovs_ref.shape[3]
    for i in range(ovs_ref.shape[1]):
        ovs_ref[0, i] = vs_ref[0, i * tn:(i + 1) * tn, :].T.astype(BF16)
        ovw_ref[0, i] = vw_ref[0, i * tn:(i + 1) * tn, :].T.astype(BF16)


def _attn_prep(proj3, cos2, sin2, tk, tn):
    b, s, _ = proj3.shape
    wf = FOX_HEADS * HEAD_DIM
    kvb = C_NSAKV // LANES
    nk = s // tk
    sub = tk // tn
    col = lambda w, c: pl.BlockSpec((1, tk, w), lambda i, j: (i, j, c))
    rows = lambda w: pl.BlockSpec((1, tk, w), lambda i, j: (i, j, 0))
    ntile = pl.BlockSpec((1, sub, LANES, tn), lambda i, j: (i, j, 0, 0))
    return pl.pallas_call(
        _prep_kernel,
        out_shape=(jax.ShapeDtypeStruct((b, s, wf), BF16), jax.ShapeDtypeStruct((b, nk, wf, tk), BF16),
                   jax.ShapeDtypeStruct((b, s, LANES), BF16), jax.ShapeDtypeStruct((b, nk * sub, LANES, tn), BF16),
                   jax.ShapeDtypeStruct((b, s, LANES), BF16), jax.ShapeDtypeStruct((b, nk * sub, LANES, tn), BF16)),
        grid=(b, nk),
        in_specs=[col(wf, C_KF // wf), col(wf, C_VF // wf), col(LANES, kvb + 2), col(LANES, kvb + 3),
                  col(LANES, kvb + 4), col(LANES, kvb + 5),
                  pl.BlockSpec((tk, LANES), lambda i, j: (j, 0)), pl.BlockSpec((tk, LANES), lambda i, j: (j, 0))],
        out_specs=(rows(wf), pl.BlockSpec((1, 1, wf, tk), lambda i, j: (i, j, 0, 0)),
                   rows(LANES), ntile, rows(LANES), ntile),
        compiler_params=_cparams(("parallel", "parallel")),
        name="attn_prep",
    )(proj3, proj3, proj3, proj3, proj3, proj3, cos2, sin2)


def _online_update(s, vt, m_ref, l_ref, acc_ref, idx, mask=None):
    m_old = m_ref[idx]
    m_new = jnp.maximum(m_old, jnp.max(s, axis=0, keepdims=True))
    alpha = jnp.exp(m_old - m_new)
    p = jnp.exp(s - m_new)
    if mask is not None:
        p = jnp.where(mask, p, 0.0)
    l_ref[idx] = alpha * l_ref[idx] + jnp.sum(p, axis=0, keepdims=True)
    acc_ref[idx] = alpha * acc_ref[idx] + _dot(vt, p.astype(BF16))
    m_ref[idx] = m_new


def _padded_qt(q_ref, cos, sin, qt_ref, tq, heads_per_slot):
    qt_ref[...] = jnp.zeros(qt_ref.shape, qt_ref.dtype)
    n_heads = q_ref.shape[2] // HEAD_DIM
    for hp in range(n_heads // 2):
        q2 = q_ref[0, :, hp * LANES:(hp + 1) * LANES]
        if cos is not None:
            q2 = _rope(q2, cos, sin)
        qt = (q2 * (HEAD_DIM ** -0.5)).T.astype(BF16)
        for sub in range(2):
            h = 2 * hp + sub
            slot, r = h // heads_per_slot, h % heads_per_slot
            half = (slot % 2) if heads_per_slot > 1 else sub
            qt_ref[slot, half * HEAD_DIM:(half + 1) * HEAD_DIM, r * tq:(r + 1) * tq] = (
                qt[sub * HEAD_DIM:(sub + 1) * HEAD_DIM, :])


def _nsa_kernel(q_ref, cos_ref, sin_ref, kc_ref, vct_ref, ks_ref, vst_ref, kw_ref, vwt_ref, sm_ref, o_ref,
                qt_ref, selt_ref, oct_ref, m_ref, l_ref, acc_ref, *, tq, n_sel):
    qi = pl.program_id(1)
    tk = tq
    G, R = NSA_GROUPS, NSA_REP
    nb = kc_ref.shape[1]
    n_prev = WINDOW // tk
    _padded_qt(q_ref, cos_ref[...], sin_ref[...], qt_ref, tq, R)
    m_ref[...] = jnp.full(m_ref.shape, NEG_INF, F32)
    l_ref[...] = jnp.zeros(l_ref.shape, F32)
    acc_ref[...] = jnp.zeros(acc_ref.shape, F32)

    tpos = qi * tq + (lax.broadcasted_iota(jnp.int32, (nb, R * tq), 1) & (tq - 1))
    nrow = lax.broadcasted_iota(jnp.int32, (nb, R * tq), 0)
    cmask = (nrow * CMP_STRIDE + (CMP_LEN - 1) <= tpos) & (nrow < nb - 1)
    oj = lax.broadcasted_iota(jnp.int32, (n_sel, nb), 0)
    on = lax.broadcasted_iota(jnp.int32, (n_sel, nb), 1) * CMP_STRIDE
    overlap_t = jnp.where((on < (oj + 1) * SEL_BLOCK) & (on + CMP_LEN > oj * SEL_BLOCK), 1.0, 0.0).astype(BF16)
    blk = lax.broadcasted_iota(jnp.int32, (n_sel, tq), 0)
    trow = qi * tq + lax.broadcasted_iota(jnp.int32, (n_sel, tq), 1)
    cur = trow // SEL_BLOCK
    forced = (blk == 0) | (blk == cur) | (blk == cur - 1)
    future = blk * SEL_BLOCK > trow
    kc = kc_ref[0]
    for g in range(G):
        s = jnp.where(cmask, _dot(kc, qt_ref[g]), NEG_INF)
        mx = jnp.max(s, axis=0, keepdims=True)
        e = jnp.where(cmask, jnp.exp(s - mx), 0.0)
        p = e / jnp.maximum(jnp.sum(e, axis=0, keepdims=True), 1e-30)
        oct_ref[g] = _dot(vct_ref[0, g * HEAD_DIM:(g + 1) * HEAD_DIM, :], p.astype(BF16))
        psum = p[:, 0:tq]
        for r in range(1, R):
            psum = psum + p[:, r * tq:(r + 1) * tq]
        hi, mid, lo = _split3(psum)
        imp = (_dot(overlap_t, hi) + _dot(overlap_t, mid)) + _dot(overlap_t, lo)
        imp = jnp.where(forced, FORCE_SCORE, jnp.where(future, -FORCE_SCORE, imp))
        rank = jnp.zeros((n_sel, tq), F32)
        for i in range(n_sel):
            ri = imp[i:i + 1, :]
            ahead = (ri > imp) | ((ri == imp) & (blk > i))
            rank = rank + jnp.where(ahead, 1.0, 0.0)
        picked = jnp.where(rank < float(min(SEL_TOPK, n_sel)), 1.0, 0.0).astype(BF16)
        for r in range(R):
            selt_ref[g, :, r * tq:(r + 1) * tq] = picked

    qpos = qi * tq + (lax.broadcasted_iota(jnp.int32, (tk, R * tq), 1) & (tq - 1))
    krow = lax.broadcasted_iota(jnp.int32, (tk, R * tq), 0)
    er = lax.broadcasted_iota(jnp.int32, (tk, n_sel), 0)
    ec = lax.broadcasted_iota(jnp.int32, (tk, n_sel), 1)

    def sel_tile(j, diagonal):
        k = ks_ref[0, pl.ds(pl.multiple_of(j * tk, tk), tk), :]
        expand_t = jnp.where((j * tk + er) // SEL_BLOCK == ec, 1.0, 0.0).astype(BF16)
        for g in range(G):
            chosen = _dot(expand_t, selt_ref[g]) > 0.5
            if diagonal:
                chosen = chosen & (j * tk + krow <= qpos)
            s = jnp.where(chosen, _dot(k, qt_ref[g]), NEG_INF)
            _online_update(s, vst_ref[0, j, g * HEAD_DIM:(g + 1) * HEAD_DIM, :], m_ref, l_ref, acc_ref, g)

    def sel_body(j, c):
        sel_tile(j, False)
        return c

    lax.fori_loop(0, qi, sel_body, 0)
    sel_tile(qi, True)

    for back in range(n_prev, -1, -1):
        @pl.when(qi - back >= 0)
        def _(back=back):
            j = qi - back
            k = kw_ref[0, pl.ds(pl.multiple_of(j * tk, tk), tk), :]
            diff = qpos - (j * tk + krow)
            if back == 0:
                mask = diff >= 0
            elif back == n_prev:
                mask = diff < WINDOW
            else:
                mask = None
            for g in range(G):
                s = _dot(k, qt_ref[g])
                if mask is not None:
                    s = jnp.where(mask, s, NEG_INF)
                _online_update(s, vwt_ref[0, j, g * HEAD_DIM:(g + 1) * HEAD_DIM, :], m_ref, l_ref, acc_ref,
                               G + g, mask if back == n_prev else None)

    gates_t = jax.nn.sigmoid(sm_ref[0]).T
    rows = []
    for h in range(NSA_HEADS):
        g, r = h // R, h % R
        cols = slice(r * tq, (r + 1) * tq)
        gate = lambda br: gates_t[GL_OFF + 3 * h + br:GL_OFF + 3 * h + br + 1, :]
        o_sel = acc_ref[g, :, cols] / jnp.maximum(l_ref[g, :, cols], 1e-30)
        o_win = acc_ref[G + g, :, cols] / jnp.maximum(l_ref[G + g, :, cols], 1e-30)
        rows.append(gate(0) * oct_ref[g, :, cols] + gate(1) * o_sel + gate(2) * o_win)
    o_ref[0] = jnp.concatenate(rows, axis=0).T


def _nsa_attend(proj3, cos2, sin2, kcmp, vcmp_t, ks, vst, kw, vwt, tq):
    b, s, _ = proj3.shape
    n_sel = s // SEL_BLOCK
    wq = NSA_HEADS * HEAD_DIM
    whole = lambda a: pl.BlockSpec((1,) + a.shape[1:], lambda i, j: (i,) + (0,) * (a.ndim - 1))
    return pl.pallas_call(
        functools.partial(_nsa_kernel, tq=tq, n_sel=n_sel),
        out_shape=jax.ShapeDtypeStruct((b, s, wq), F32),
        grid=(b, s // tq),
        in_specs=[pl.BlockSpec((1, tq, wq), lambda i, j: (i, j, C_QA // wq)),
                  pl.BlockSpec((tq, LANES), lambda i, j: (j, 0)),
                  pl.BlockSpec((tq, LANES), lambda i, j: (j, 0)),
                  whole(kcmp), whole(vcmp_t), whole(ks), whole(vst), whole(kw), whole(vwt),
                  pl.BlockSpec((1, tq, LANES), lambda i, j: (i, j, C_SMALL // LANES))],
        out_specs=pl.BlockSpec((1, tq, wq), lambda i, j: (i, j, 0)),
        scratch_shapes=[pltpu.VMEM((NSA_GROUPS, LANES, NSA_REP * tq), BF16),
                        pltpu.VMEM((NSA_GROUPS, n_sel, NSA_REP * tq), BF16),
                        pltpu.VMEM((NSA_GROUPS, HEAD_DIM, NSA_REP * tq), F32),
                        pltpu.VMEM((2 * NSA_GROUPS, 1, NSA_REP * tq), F32),
                        pltpu.VMEM((2 * NSA_GROUPS, 1, NSA_REP * tq), F32),
                        pltpu.VMEM((2 * NSA_GROUPS, HEAD_DIM, NSA_REP * tq), F32)],
        compiler_params=_cparams(("parallel", "parallel")),
        name="nsa_attend",
    )(proj3, cos2, sin2, kcmp, vcmp_t, ks, vst, kw, vwt, proj3)


def _fox_kernel(q_ref, k_ref, vt_ref, cq_ref, ck_ref, o_ref, qt_ref, m_ref, l_ref, acc_ref, *, tq):
    qi = pl.program_id(1)
    tk = tq
    _padded_qt(q_ref, None, None, qt_ref, tq, 1)
    m_ref[...] = jnp.full(m_ref.shape, NEG_INF, F32)
    l_ref[...] = jnp.zeros(l_ref.shape, F32)
    acc_ref[...] = jnp.zeros(acc_ref.shape, F32)
    cq = cq_ref[0]
    causal = lax.broadcasted_iota(jnp.int32, (tk, tq), 0) <= lax.broadcasted_iota(jnp.int32, (tk, tq), 1)

    def tile(j, diagonal):
        rows = pl.ds(pl.multiple_of(j * tk, tk), tk)
        ck = ck_ref[0, rows, :]
        for h in range(FOX_HEADS):
            hp = h // 2
            s = _dot(k_ref[0, rows, hp * LANES:(hp + 1) * LANES], qt_ref[h])
            s = (s - ck[:, FL_OFF + h:FL_OFF + h + 1]) + cq[FL_OFF + h:FL_OFF + h + 1, :]
            if diagonal:
                s = jnp.where(causal, s, NEG_INF)
            _online_update(s, vt_ref[0, j, h * HEAD_DIM:(h + 1) * HEAD_DIM, :], m_ref, l_ref, acc_ref, h)

    def body(j, c):
        tile(j, False)
        return c

    lax.fori_loop(0, qi, body, 0)
    tile(qi, True)
    rows = [acc_ref[h] / jnp.maximum(l_ref[h], 1e-30) for h in range(FOX_HEADS)]
    o_ref[0] = jnp.concatenate(rows, axis=0).T


def _fox_attend(proj3, kf, vft, cum, cum_t, tq):
    b, s, _ = proj3.shape
    w = FOX_HEADS * HEAD_DIM
    whole = lambda a: pl.BlockSpec((1,) + a.shape[1:], lambda i, j: (i,) + (0,) * (a.ndim - 1))
    return pl.pallas_call(
        functools.partial(_fox_kernel, tq=tq),
        out_shape=jax.ShapeDtypeStruct((b, s, w), F32),
        grid=(b, s // tq),
        in_specs=[pl.BlockSpec((1, tq, w), lambda i, j: (i, j, C_QF // w)),
                  whole(kf), whole(vft),
                  pl.BlockSpec((1, LANES, tq), lambda i, j: (i, 0, j)),
                  whole(cum)],
        out_specs=pl.BlockSpec((1, tq, w), lambda i, j: (i, j, 0)),
        scratch_shapes=[pltpu.VMEM((FOX_HEADS, LANES, tq), BF16),
                        pltpu.VMEM((FOX_HEADS, 1, tq), F32),
                        pltpu.VMEM((FOX_HEADS, 1, tq), F32),
                        pltpu.VMEM((FOX_HEADS, HEAD_DIM, tq), F32)],
        compiler_params=_cparams(("parallel", "parallel")),
        name="fox_attend",
    )(proj3, kf, vft, cum_t, cum)


def _memkv_kernel(m_ref, g_ref, w_ref, o_ref):
    x = m_ref[0]
    ms = jnp.mean(x * x, axis=-1, keepdims=True)
    hn = (x * lax.rsqrt(ms + RMS_EPS) * g_ref[...]).astype(BF16)
    o_ref[0] = _dot(hn, w_ref[...]).astype(BF16)


def _mem_kv(mem, g, w_bf16):
    b, m, d = mem.shape
    n = w_bf16.shape[1]
    return pl.pallas_call(
        _memkv_kernel,
        out_shape=jax.ShapeDtypeStruct((b, m, n), BF16),
        grid=(b,),
        in_specs=[pl.BlockSpec((1, m, d), lambda i: (i, 0, 0)),
                  pl.BlockSpec((1, d), lambda i: (0, 0)),
                  pl.BlockSpec((d, n), lambda i: (0, 0))],
        out_specs=pl.BlockSpec((1, m, n), lambda i: (i, 0, 0)),
        compiler_params=_cparams(("parallel",)),
        name="mem_kv",
    )(mem, g, w_bf16)


def _memattn_kernel(q_ref, kv_ref, o_ref):
    w = MEM_HEADS * MEM_HEAD_DIM
    for h in range(MEM_HEADS):
        lo, hi = h * MEM_HEAD_DIM, (h + 1) * MEM_HEAD_DIM
        s = _dot_nt(q_ref[0, :, lo:hi].astype(BF16), kv_ref[0, :, lo:hi]) * (MEM_HEAD_DIM ** -0.5)
        e = jnp.exp(s - jnp.max(s, axis=-1, keepdims=True))
        p = e / jnp.sum(e, axis=-1, keepdims=True)
        o_ref[0, :, lo:hi] = _dot(p.astype(BF16), kv_ref[0, :, w + lo:w + hi])


def _mem_attend(proj3, kv, tq=512):
    b, s, _ = proj3.shape
    m = kv.shape[1]
    w = MEM_HEADS * MEM_HEAD_DIM
    return pl.pallas_call(
        _memattn_kernel,
        out_shape=jax.ShapeDtypeStruct((b, s, w), F32),
        grid=(b, s // tq),
        in_specs=[pl.BlockSpec((1, tq, w), lambda i, j: (i, j, C_QM // w)),
                  pl.BlockSpec((1, m, 2 * w), lambda i, j: (i, 0, 0))],
        out_specs=pl.BlockSpec((1, tq, w), lambda i, j: (i, j, 0)),
        compiler_params=_cparams(("parallel", "parallel")),
        name="mem_attend",
    )(proj3, kv)


def _merge_kernel(on_ref, of_ref, om_ref, ln_ref, lf_ref, lm_ref, bm_ref, wn_ref, wf_ref, wm_ref,
                  wo_ref, x_ref, g_ref, wr_ref, br_ref, x1_ref, hn_ref, idx_ref, gate_ref):
    d = x_ref.shape[1]

    def branch(o_ref, l_ref, w_ref, k):
        gate = jax.nn.sigmoid(l_ref[...] + bm_ref[:, k * d:(k + 1) * d])
        return gate * _dot(o_ref[...].astype(BF16), w_ref[...])

    merged = branch(on_ref, ln_ref, wn_ref, 0) + branch(of_ref, lf_ref, wf_ref, 1) + branch(om_ref, lm_ref, wm_ref, 2)
    x1 = x_ref[...] + _dot(merged.astype(BF16), wo_ref[...])
    x1_ref[...] = x1
    ms = jnp.mean(x1 * x1, axis=-1, keepdims=True)
    hn = x1 * lax.rsqrt(ms + RMS_EPS) * g_ref[...]
    hn_ref[...] = hn
    h_hi = hn.astype(BF16)
    h_lo = (hn - h_hi.astype(F32)).astype(BF16)
    wr = wr_ref[...]
    w_hi = wr.astype(BF16)
    w_lo = (wr - w_hi.astype(F32)).astype(BF16)
    logits = (_dot(h_hi, w_hi) + (_dot(h_lo, w_hi) + _dot(h_hi, w_lo))) + br_ref[...]
    tm, ne = logits.shape
    lane_e = lax.broadcasted_iota(jnp.int32, (tm, ne), 1)
    lane = lax.broadcasted_iota(jnp.int32, (tm, LANES), 1)
    idx_slab = jnp.zeros((tm, LANES), jnp.int32)
    val_slab = jnp.zeros((tm, LANES), F32)
    work = logits
    vals = []
    for k in range(TOP_K):
        m = jnp.max(work, axis=-1, keepdims=True)
        idx = jnp.min(jnp.where(work == m, lane_e, ne), axis=-1, keepdims=True)
        work = jnp.where(lane_e == idx, -jnp.inf, work)
        idx_slab = jnp.where(lane == k, idx, idx_slab)
        vals.append(m)
    es = [jnp.exp(v - vals[0]) for v in vals]
    tot = es[0]
    for e in es[1:]:
        tot = tot + e
    for k in range(TOP_K):
        val_slab = jnp.where(lane == k, es[k] / tot, val_slab)
    idx_ref[...] = idx_slab
    gate_ref[...] = val_slab


def _merge_route(o_nsa, o_fox, o_mem, proj, b_merge, wn, wf, wm, wo, x2, g_ffn, w_router, b_router, tm=256):
    t, d = x2.shape
    wb = o_nsa.shape[1]
    row = lambda w: pl.BlockSpec((tm, w), lambda i: (i, 0))
    full = lambda a: pl.BlockSpec(a.shape, lambda i: (0,) * a.ndim)
    return pl.pallas_call(
        _merge_kernel,
        out_shape=(jax.ShapeDtypeStruct((t, d), F32), jax.ShapeDtypeStruct((t, d), F32),
                   jax.ShapeDtypeStruct((t, LANES), jnp.int32), jax.ShapeDtypeStruct((t, LANES), F32)),
        grid=(t // tm,),
        in_specs=[row(wb), row(wb), row(wb),
                  pl.BlockSpec((tm, d), lambda i: (i, 0)),
                  pl.BlockSpec((tm, d), lambda i: (i, 1)),
                  pl.BlockSpec((tm, d), lambda i: (i, 2)),
                  full(b_merge), full(wn), full(wf), full(wm), full(wo),
                  row(d), full(g_ffn), full(w_router), full(b_router)],
        out_specs=(row(d), row(d), row(LANES), row(LANES)),
        compiler_params=_cparams(("parallel",)),
        name="merge_route",
    )(o_nsa, o_fox, o_mem, proj, proj, proj, b_merge, wn, wf, wm, wo, x2, g_ffn, w_router, b_router)


def _rank_kernel(idx_ref, rank_ref, cnt_ref, carry_ref, *, tm):
    @pl.when(pl.program_id(0) == 0)
    def _():
        carry_ref[...] = jnp.zeros(carry_ref.shape, F32)

    idx = idx_ref[...]
    lane_e = lax.broadcasted_iota(jnp.int32, (tm, N_EXPERTS), 1)
    hots = [jnp.where(idx[:, k:k + 1] == lane_e, 1.0, 0.0) for k in range(TOP_K)]
    cnt = hots[0]
    for hk in hots[1:]:
        cnt = cnt + hk
    r = lax.broadcasted_iota(jnp.int32, (tm, tm), 0)
    c = lax.broadcasted_iota(jnp.int32, (tm, tm), 1)
    strict = jnp.where(c < r, 1.0, 0.0).astype(BF16)
    before = _dot(strict, cnt.astype(BF16)) + carry_ref[...]
    lane = lax.broadcasted_iota(jnp.int32, (tm, LANES), 1)
    slab = jnp.zeros((tm, LANES), F32)
    for k in range(TOP_K):
        slab = jnp.where(lane == k, jnp.sum(hots[k] * before, axis=-1, keepdims=True), slab)
    rank_ref[...] = slab.astype(jnp.int32)
    total = carry_ref[...] + jnp.sum(cnt, axis=0, keepdims=True)
    carry_ref[...] = total
    cnt_ref[...] = jnp.broadcast_to(total, cnt_ref.shape).astype(jnp.int32)


def _moe_rank(idx_slab, tm=256):
    t = idx_slab.shape[0]
    return pl.pallas_call(
        functools.partial(_rank_kernel, tm=tm),
        out_shape=(jax.ShapeDtypeStruct((t, LANES), jnp.int32),
                   jax.ShapeDtypeStruct((8, N_EXPERTS), jnp.int32)),
        grid=(t // tm,),
        in_specs=[pl.BlockSpec((tm, LANES), lambda i: (i, 0))],
        out_specs=(pl.BlockSpec((tm, LANES), lambda i: (i, 0)),
                   pl.BlockSpec((8, N_EXPERTS), lambda i: (0, 0))),
        scratch_shapes=[pltpu.VMEM((1, N_EXPERTS), F32)],
        compiler_params=_cparams(("arbitrary",)),
        name="moe_rank",
    )(idx_slab)


def _expert_kernel(be_ref, nu_ref, tok_ref, hn_ref, wg_ref, wu_ref, bg_ref, bu_ref, wd_ref, bd_ref, o_ref,
                   buf_ref, sem, *, unroll):
    i = pl.program_id(0)
    n_used = nu_ref[0]

    def row_copy(tok, r, s):
        return pltpu.make_async_copy(hn_ref.at[pl.ds(tok, 1)], buf_ref.at[s, pl.ds(r, 1)], sem.at[s])

    def issue(step, s):
        base = step * MOE_ROWS

        def body(r, c):
            row_copy(tok_ref[base + r], r, s).start()
            return c

        lax.fori_loop(0, MOE_ROWS, body, 0, unroll=unroll)

    @pl.when((i == 0) & (n_used > 0))
    def _():
        issue(0, 0)

    @pl.when(i + 1 < n_used)
    def _():
        issue(i + 1, (i + 1) % 2)

    @pl.when(i < n_used)
    def _():
        cur = i % 2

        def wait(r, c):
            row_copy(0, 0, cur).wait()
            return c

        lax.fori_loop(0, MOE_ROWS, wait, 0, unroll=unroll)
        x = buf_ref[cur].astype(BF16)
        g = jnp.minimum(_dot(x, wg_ref[0]) + bg_ref[0], SWIGLU_LIMIT)
        u = jnp.clip(_dot(x, wu_ref[0]) + bu_ref[0], -SWIGLU_LIMIT, SWIGLU_LIMIT)
        act = (u + 1.0) * g * jax.nn.sigmoid(SWIGLU_ALPHA * g)
        o_ref[...] = _dot(act.astype(BF16), wd_ref[0]) + bd_ref[0]

    @pl.when(i >= n_used)
    def _():
        o_ref[...] = jnp.zeros(o_ref.shape, o_ref.dtype)


def _moe_experts(block_exp, n_used, slot_tok, hn, wg, wu, bg, bu, wd, bd):
    n_slots = slot_tok.shape[0]
    d = hn.shape[1]
    f = wg.shape[2]
    nblk = n_slots // MOE_ROWS
    wspec = lambda shp: pl.BlockSpec((1,) + shp, lambda i, be, nu, tk: (be[i], 0, 0))
    return pl.pallas_call(
        functools.partial(_expert_kernel, unroll=8),
        out_shape=jax.ShapeDtypeStruct((n_slots, d), F32),
        grid_spec=pltpu.PrefetchScalarGridSpec(
            num_scalar_prefetch=3,
            grid=(nblk,),
            in_specs=[pl.BlockSpec(memory_space=pl.ANY),
                      wspec((d, f)), wspec((d, f)), wspec((1, f)), wspec((1, f)),
                      wspec((f, d)), wspec((1, d))],
            out_specs=pl.BlockSpec((MOE_ROWS, d), lambda i, be, nu, tk: (i, 0)),
            scratch_shapes=[pltpu.VMEM((2, MOE_ROWS, d), hn.dtype), pltpu.SemaphoreType.DMA((2,))]),
        compiler_params=_cparams(("arbitrary",)),
        name="moe_experts",
    )(block_exp, n_used, slot_tok, hn, wg, wu, bg, bu, wd, bd)


def _combine_kernel(dest_ref, gate_ref, x1_ref, g_ref, ys_ref, o_ref, buf_ref, sem, *, tm, unroll):
    i = pl.program_id(0)
    n = pl.num_programs(0)

    def row_copy(slot_row, k, r, s):
        return pltpu.make_async_copy(ys_ref.at[pl.ds(slot_row, 1)], buf_ref.at[s, k, pl.ds(r, 1)], sem.at[s])

    def issue(step, s):
        base = step * (tm * TOP_K)

        def body(r, c):
            for k in range(TOP_K):
                row_copy(dest_ref[base + r * TOP_K + k], k, r, s).start()
            return c

        lax.fori_loop(0, tm, body, 0, unroll=unroll)

    @pl.when(i == 0)
    def _():
        issue(0, 0)

    @pl.when(i + 1 < n)
    def _():
        issue(i + 1, (i + 1) % 2)

    cur = i % 2

    def wait(r, c):
        row_copy(0, 0, 0, cur).wait()
        return c

    lax.fori_loop(0, tm * TOP_K, wait, 0, unroll=unroll)

    gate = gate_ref[...]
    y = x1_ref[...]
    for k in range(TOP_K):
        y = y + gate[:, k:k + 1] * buf_ref[cur, k]
    ms = jnp.mean(y * y, axis=-1, keepdims=True)
    o_ref[...] = y * lax.rsqrt(ms + RMS_EPS) * g_ref[...]


def _moe_combine(dest_flat, gate_slab, x1, g_final, ys, tm=128):
    t, d = x1.shape
    return pl.pallas_call(
        functools.partial(_combine_kernel, tm=tm, unroll=8),
        out_shape=jax.ShapeDtypeStruct((t, d), F32),
        grid_spec=pltpu.PrefetchScalarGridSpec(
            num_scalar_prefetch=1,
            grid=(t // tm,),
            in_specs=[pl.BlockSpec((tm, LANES), lambda i, dst: (i, 0)),
                      pl.BlockSpec((tm, d), lambda i, dst: (i, 0)),
                      pl.BlockSpec((1, d), lambda i, dst: (0, 0)),
                      pl.BlockSpec(memory_space=pl.ANY)],
            out_specs=pl.BlockSpec((tm, d), lambda i, dst: (i, 0)),
            scratch_shapes=[pltpu.VMEM((2, TOP_K, tm, d), F32), pltpu.SemaphoreType.DMA((2,))]),
        compiler_params=_cparams(("arbitrary",)),
        name="moe_combine",
    )(dest_flat, gate_slab, x1, g_final, ys)


def _rope_tables(s):
    inv = ROPE_THETA ** (-jnp.arange(0, HEAD_DIM, 2, dtype=F32) / HEAD_DIM)
    ang = jnp.arange(s, dtype=F32)[:, None] * inv[None, :]
    cos, sin = jnp.cos(ang), jnp.sin(ang)
    return jnp.concatenate([cos, cos], axis=-1), jnp.concatenate([-sin, sin], axis=-1)


def _permute_w_in(w):
    d = w.shape[0]
    pad = N_PROJ - (C_SMALL + 32)
    cols = [w[:, 3360:6432], w[:, 0:512], w[:, 1304:1816], w[:, 1816:2328], w[:, 2328:2840],
            w[:, 2848:3360], w[:, 512:1280], w[:, 1280:1304], w[:, 2840:2848], jnp.zeros((d, pad), w.dtype)]
    return jnp.concatenate(cols, axis=1).astype(BF16)


def _layer(x, mem, g_mix, w_in, b_forget, b_merge, pe_k, w1_k, w2_k, pe_v, w1_v, w2_v, g_mem, w_mem_kv,
           w_br_nsa, w_br_fox, w_br_mem, w_out, g_ffn, w_router, b_router, w_gate_up, b_gate_up,
           w_down, b_down, g_final):
    b, s, d = x.shape
    t = b * s
    x2 = x.reshape(t, d)
    cos64, sin64 = _rope_tables(s)
    cos2 = jnp.concatenate([cos64, cos64], axis=-1)
    sin2 = jnp.concatenate([sin64, sin64], axis=-1)

    proj = _inproj(x2, g_mix.reshape(1, d), _permute_w_in(w_in), tm=min(1024, t))
    proj3 = proj.reshape(b, s, N_PROJ)
    b_row = jnp.zeros((1, LANES), F32).at[0, FL_OFF:FL_OFF + FOX_HEADS].set(b_forget)
    cum, cum_t = _fox_cum(proj3, b_row)

    nb = s // CMP_STRIDE
    wide = CMP_STRIDE * HEAD_DIM

    def to_blocks(c0):
        a = proj3[:, :, c0:c0 + NSA_GROUPS * HEAD_DIM].reshape(b, nb, CMP_STRIDE, NSA_GROUPS, HEAD_DIM)
        return a.transpose(0, 3, 1, 2, 4).reshape(b, NSA_GROUPS, nb, wide)

    kcmp, vcmp_t = _nsa_compress(
        to_blocks(C_NSAKV), to_blocks(C_NSAKV + LANES), cos64.reshape(nb, wide), sin64.reshape(nb, wide),
        pe_k.reshape(2, wide), pe_v.reshape(2, wide),
        w1_k.reshape(CMP_LEN * HEAD_DIM, HEAD_DIM).astype(BF16), w1_v.reshape(CMP_LEN * HEAD_DIM, HEAD_DIM).astype(BF16),
        w2_k.astype(BF16), w2_v.astype(BF16))
    fox_tile = min(FOX_TILE, s)
    kf, vft, ks, vst, kw, vwt = _attn_prep(proj3, cos2, sin2, fox_tile, NSA_TILE)
    o_nsa = _nsa_attend(proj3, cos2, sin2, kcmp, vcmp_t, ks, vst, kw, vwt, NSA_TILE)
    o_fox = _fox_attend(proj3, kf, vft, cum, cum_t, fox_tile)
    mem_kv = _mem_kv(mem, g_mem.reshape(1, d), w_mem_kv.astype(BF16))
    o_mem = _mem_attend(proj3, mem_kv)

    x1, hn, idx_slab, gate_slab = _merge_route(
        o_nsa.reshape(t, -1), o_fox.reshape(t, -1), o_mem.reshape(t, -1), proj, b_merge.reshape(1, -1),
        w_br_nsa.astype(BF16), w_br_fox.astype(BF16), w_br_mem.astype(BF16), w_out.astype(BF16),
        x2, g_ffn.reshape(1, d), w_router, b_router.reshape(1, -1))

    rank_slab, cnt8 = _moe_rank(idx_slab)
    counts = cnt8[0]
    padded = (counts + MOE_ROWS - 1) // MOE_ROWS * MOE_ROWS
    pad_end = jnp.cumsum(padded)
    pad_start = pad_end - padded
    n_assign = t * TOP_K
    nblk = -(-(n_assign + N_EXPERTS * (MOE_ROWS - 1)) // MOE_ROWS)
    n_slots = nblk * MOE_ROWS
    top_idx = idx_slab[:, :TOP_K]
    dest = (pad_start[top_idx] + rank_slab[:, :TOP_K]).reshape(-1).astype(jnp.int32)
    tok_flat = jnp.repeat(jnp.arange(t, dtype=jnp.int32), TOP_K)
    slot_tok = jnp.zeros((n_slots,), jnp.int32).at[dest].set(tok_flat)
    blk_start = jnp.arange(nblk, dtype=jnp.int32) * MOE_ROWS
    block_exp = jnp.minimum(jnp.sum(blk_start[:, None] >= pad_end[None, :], axis=1),
                            N_EXPERTS - 1).astype(jnp.int32)
    n_used = (pad_end[-1] // MOE_ROWS).astype(jnp.int32).reshape(1)

    f = w_down.shape[1]
    wgu = w_gate_up.reshape(N_EXPERTS, d, f, 2)
    bgu = b_gate_up.reshape(N_EXPERTS, 1, f, 2)
    ys = _moe_experts(block_exp, n_used, slot_tok, hn, wgu[..., 0].astype(BF16), wgu[..., 1].astype(BF16),
                      bgu[..., 0], bgu[..., 1], w_down.astype(BF16), b_down.reshape(N_EXPERTS, 1, d))
    out = _moe_combine(dest, gate_slab, x1, g_final.reshape(1, d), ys)
    return out.reshape(b, s, d)


def kernel(x, mem, g_mix, w_in, b_forget, b_merge, nsa_pe_k, nsa_w1_k, nsa_w2_k, nsa_pe_v, nsa_w1_v, nsa_w2_v, g_mem, w_mem_kv, w_branch_nsa, w_branch_fox, w_branch_mem, w_out, g_ffn, w_router, b_router, w_gate_up, b_gate_up, w_down, b_down, g_final):
    assert g_mix.shape[0] == 1, "single-layer trunk"
    return _layer(x, mem, g_mix[0], w_in[0], b_forget[0], b_merge[0], nsa_pe_k[0], nsa_w1_k[0], nsa_w2_k[0],
                  nsa_pe_v[0], nsa_w1_v[0], nsa_w2_v[0], g_mem[0], w_mem_kv[0], w_branch_nsa[0],
                  w_branch_fox[0], w_branch_mem[0], w_out[0], g_ffn[0], w_router[0], b_router[0],
                  w_gate_up[0], b_gate_up[0], w_down[0], b_down[0], g_final)
```

```python
import functools

import jax
import jax.numpy as jnp
import numpy as np
from jax import lax
from jax.experimental import pallas as pl
from jax.experimental.pallas import tpu as pltpu

F32 = jnp.float32
BF16 = jnp.bfloat16

D_MODEL = 1024
HEAD_DIM = 64
ROPE_THETA = 10000.0
RMS_EPS = 1e-5
NEG_INF = -1e30
FORCE_SCORE = 1e4
NSA_HEADS = 8
NSA_GROUPS = 2
NSA_REP = NSA_HEADS // NSA_GROUPS
CMP_LEN = 32
CMP_STRIDE = 16
SEL_BLOCK = 64
SEL_TOPK = 16
WINDOW = 512
FOX_HEADS = 8
MEM_HEADS = 4
MEM_HEAD_DIM = 128
N_EXPERTS = 32
TOP_K = 4
SWIGLU_LIMIT = 7.0
SWIGLU_ALPHA = 1.702

LANES = 128
VMEM_LIMIT = 48 * 1024 * 1024

C_MERGE = 0
C_QA = 3072
C_QF = 3584
C_KF = 4096
C_VF = 4608
C_QM = 5120
C_NSAKV = 5632
C_SMALL = 6400
N_PROJ = 6656
GL_OFF = 0
FL_OFF = 24

MOE_ROWS = 256
NSA_TILE = 256
FOX_TILE = 512


def _cparams(sem, vmem=VMEM_LIMIT):
    return pltpu.CompilerParams(dimension_semantics=sem, vmem_limit_bytes=vmem)


def _dot(a, b):
    return jnp.dot(a, b, preferred_element_type=F32)


def _dot_nt(a, b):
    return lax.dot_general(a, b, (((1,), (1,)), ((), ())), preferred_element_type=F32)


def _rope(x, cos, sin_signed):
    w = x.shape[-1]
    lane = lax.broadcasted_iota(jnp.int32, x.shape, x.ndim - 1)
    first = (lane & (HEAD_DIM - 1)) < (HEAD_DIM // 2)
    rot = jnp.where(first, pltpu.roll(x, w - HEAD_DIM // 2, x.ndim - 1),
                    pltpu.roll(x, HEAD_DIM // 2, x.ndim - 1))
    return x * cos + rot * sin_signed


def _split3(x):
    hi = x.astype(BF16)
    r1 = x - hi.astype(F32)
    mid = r1.astype(BF16)
    lo = (r1 - mid.astype(F32)).astype(BF16)
    return hi, mid, lo


def _inproj_kernel(x_ref, g_ref, w_ref, o_ref, hn_ref):
    @pl.when(pl.program_id(1) == 0)
    def _():
        x = x_ref[...]
        ms = jnp.mean(x * x, axis=-1, keepdims=True)
        hn_ref[...] = (x * lax.rsqrt(ms + RMS_EPS) * g_ref[...]).astype(BF16)

    o_ref[...] = _dot(hn_ref[...], w_ref[...])


def _inproj(x2, g, w_bf16, tm=1024, tn=512):
    t, d = x2.shape
    n = w_bf16.shape[1]
    return pl.pallas_call(
        _inproj_kernel,
        out_shape=jax.ShapeDtypeStruct((t, n), F32),
        grid=(t // tm, n // tn),
        in_specs=[pl.BlockSpec((tm, d), lambda i, j: (i, 0)),
                  pl.BlockSpec((1, d), lambda i, j: (0, 0)),
                  pl.BlockSpec((d, tn), lambda i, j: (0, j))],
        out_specs=pl.BlockSpec((tm, tn), lambda i, j: (i, j)),
        scratch_shapes=[pltpu.VMEM((tm, d), BF16)],
        compiler_params=_cparams(("parallel", "arbitrary")),
        name="inproj",
    )(x2, g, w_bf16)


def _cum_kernel(s_ref, b_ref, c_ref, ct_ref, *, blk):
    s = s_ref.shape[1]
    z = s_ref[0] + b_ref[...]
    logf = jnp.minimum(z, 0.0) - jnp.log1p(jnp.exp(-jnp.abs(z)))
    r = lax.broadcasted_iota(jnp.int32, (blk, blk), 0)
    c = lax.broadcasted_iota(jnp.int32, (blk, blk), 1)
    tri = jnp.where(c <= r, 1.0, 0.0).astype(BF16)
    carry = jnp.zeros((1, LANES), F32)
    for i in range(s // blk):
        hi, mid, lo = _split3(logf[i * blk:(i + 1) * blk])
        loc = (_dot(tri, hi) + _dot(tri, mid)) + _dot(tri, lo)
        out = loc + carry
        c_ref[0, i * blk:(i + 1) * blk, :] = out
        carry = out[blk - 1:blk, :]
    ct_ref[0] = c_ref[0].T


def _fox_cum(proj3, b_row, blk=256):
    b, s, _ = proj3.shape
    return pl.pallas_call(
        functools.partial(_cum_kernel, blk=blk),
        out_shape=(jax.ShapeDtypeStruct((b, s, LANES), F32),
                   jax.ShapeDtypeStruct((b, LANES, s), F32)),
        grid=(b,),
        in_specs=[pl.BlockSpec((1, s, LANES), lambda i: (i, 0, C_SMALL // LANES)),
                  pl.BlockSpec((1, LANES), lambda i: (0, 0))],
        out_specs=(pl.BlockSpec((1, s, LANES), lambda i: (i, 0, 0)),
                   pl.BlockSpec((1, LANES, s), lambda i: (i, 0, 0))),
        compiler_params=_cparams(("parallel",)),
        name="fox_cum",
    )(proj3, b_row)


def _cmp_kernel(ak_ref, av_ref, cos_ref, sin_ref, pek_ref, pev_ref, w1k_ref, w1v_ref,
                w2k_ref, w2v_ref, kc_ref, vct_ref, vbuf_ref):
    nb = ak_ref.shape[2]
    half = ak_ref.shape[3]

    def mlp(a, pe_ref, w1_ref, w2_ref):
        pa = _dot((a + pe_ref[0:1, :]).astype(BF16), w1_ref[0:half, :])
        pb = _dot((a + pe_ref[1:2, :]).astype(BF16), w1_ref[half:2 * half, :])
        z = pa + pltpu.roll(pb, nb - 1, 0)
        h = z * jax.nn.sigmoid(z)
        return _dot(h.astype(BF16), w2_ref[...])

    for g in range(NSA_GROUPS):
        lo, hi = g * HEAD_DIM, (g + 1) * HEAD_DIM
        kc_ref[0, :, lo:hi] = mlp(_rope(ak_ref[0, g], cos_ref[...], sin_ref[...]),
                                  pek_ref, w1k_ref, w2k_ref).astype(BF16)
        vbuf_ref[:, lo:hi] = mlp(av_ref[0, g], pev_ref, w1v_ref, w2v_ref)
    vct_ref[0] = vbuf_ref[...].T.astype(BF16)


def _nsa_compress(ak, av, cos_a, sin_a, pek, pev, w1k, w1v, w2k, w2v):
    b, g, nb, wide = ak.shape
    blk4 = pl.BlockSpec((1, g, nb, wide), lambda i: (i, 0, 0, 0))
    full = lambda shp: pl.BlockSpec(shp, lambda i: (0,) * len(shp))
    return pl.pallas_call(
        _cmp_kernel,
        out_shape=(jax.ShapeDtypeStruct((b, nb, g * HEAD_DIM), BF16),
                   jax.ShapeDtypeStruct((b, g * HEAD_DIM, nb), BF16)),
        grid=(b,),
        in_specs=[blk4, blk4, full(cos_a.shape), full(sin_a.shape), full(pek.shape), full(pev.shape),
                  full(w1k.shape), full(w1v.shape), full(w2k.shape), full(w2v.shape)],
        out_specs=(pl.BlockSpec((1, nb, g * HEAD_DIM), lambda i: (i, 0, 0)),
                   pl.BlockSpec((1, g * HEAD_DIM, nb), lambda i: (i, 0, 0))),
        scratch_shapes=[pltpu.VMEM((nb, g * HEAD_DIM), F32)],
        compiler_params=_cparams(("parallel",)),
        name="nsa_compress",
    )(ak, av, cos_a, sin_a, pek, pev, w1k, w1v, w2k, w2v)


def _prep_kernel(kf_ref, vf_ref, ks_ref, vs_ref, kw_ref, vw_ref, cos_ref, sin_ref,
                 okf_ref, ovf_ref, oks_ref, ovs_ref, okw_ref, ovw_ref):
    okf_ref[0] = kf_ref[0].astype(BF16)
    ovf_ref[0, 0] = vf_ref[0].T.astype(BF16)
    oks_ref[0] = _rope(ks_ref[0], cos_ref[...], sin_ref[...]).astype(BF16)
    okw_ref[0] = _rope(kw_ref[0], cos_ref[...], sin_ref[...]).astype(BF16)
    tn = ovs_ref.shape[3]
    for i in range(ovs_ref.shape[1]):
        ovs_ref[0, i] = vs_ref[0, i * tn:(i + 1) * tn, :].T.astype(BF16)
        ovw_ref[0, i] = vw_ref[0, i * tn:(i + 1) * tn, :].T.astype(BF16)


def _attn_prep(proj3, cos2, sin2, tk, tn):
    b, s, _ = proj3.shape
    wf = FOX_HEADS * HEAD_DIM
    kvb = C_NSAKV // LANES
    nk = s // tk
    sub = tk // tn
    col = lambda w, c: pl.BlockSpec((1, tk, w), lambda i, j: (i, j, c))
    rows = lambda w: pl.BlockSpec((1, tk, w), lambda i, j: (i, j, 0))
    ntile = pl.BlockSpec((1, sub, LANES, tn), lambda i, j: (i, j, 0, 0))
    return pl.pallas_call(
        _prep_kernel,
        out_shape=(jax.ShapeDtypeStruct((b, s, wf), BF16), jax.ShapeDtypeStruct((b, nk, wf, tk), BF16),
                   jax.ShapeDtypeStruct((b, s, LANES), BF16), jax.ShapeDtypeStruct((b, nk * sub, LANES, tn), BF16),
                   jax.ShapeDtypeStruct((b, s, LANES), BF16), jax.ShapeDtypeStruct((b, nk * sub, LANES, tn), BF16)),
        grid=(b, nk),
        in_specs=[col(wf, C_KF // wf), col(wf, C_VF // wf), col(LANES, kvb + 2), col(LANES, kvb + 3),
                  col(LANES, kvb + 4), col(LANES, kvb + 5),
                  pl.BlockSpec((tk, LANES), lambda i, j: (j, 0)), pl.BlockSpec((tk, LANES), lambda i, j: (j, 0))],
        out_specs=(rows(wf), pl.BlockSpec((1, 1, wf, tk), lambda i, j: (i, j, 0, 0)),
                   rows(LANES), ntile, rows(LANES), ntile),
        compiler_params=_cparams(("parallel", "parallel")),
        name="attn_prep",
    )(proj3, proj3, proj3, proj3, proj3, proj3, cos2, sin2)


def _online_update(s, vt, m_ref, l_ref, acc_ref, idx, mask=None):
    m_old = m_ref[idx]
    m_new = jnp.maximum(m_old, jnp.max(s, axis=0, keepdims=True))
    alpha = jnp.exp(m_old - m_new)
    p = jnp.exp(s - m_new)
    if mask is not None:
        p = jnp.where(mask, p, 0.0)
    l_ref[idx] = alpha * l_ref[idx] + jnp.sum(p, axis=0, keepdims=True)
    acc_ref[idx] = alpha * acc_ref[idx] + _dot(vt, p.astype(BF16))
    m_ref[idx] = m_new


def _padded_qt(q_ref, cos, sin, qt_ref, tq, heads_per_slot):
    qt_ref[...] = jnp.zeros(qt_ref.shape, qt_ref.dtype)
    n_heads = q_ref.shape[2] // HEAD_DIM
    for hp in range(n_heads // 2):
        q2 = q_ref[0, :, hp * LANES:(hp + 1) * LANES]
        if cos is not None:
            q2 = _rope(q2, cos, sin)
        qt = (q2 * (HEAD_DIM ** -0.5)).T.astype(BF16)
        for sub in range(2):
            h = 2 * hp + sub
            slot, r = h // heads_per_slot, h % heads_per_slot
            half = (slot % 2) if heads_per_slot > 1 else sub
            qt_ref[slot, half * HEAD_DIM:(half + 1) * HEAD_DIM, r * tq:(r + 1) * tq] = (
                qt[sub * HEAD_DIM:(sub + 1) * HEAD_DIM, :])


def _nsa_kernel(q_ref, cos_ref, sin_ref, kc_ref, vct_ref, ks_ref, vst_ref, kw_ref, vwt_ref, sm_ref, o_ref,
                qt_ref, selt_ref, oct_ref, m_ref, l_ref, acc_ref, *, tq, n_sel):
    qi = pl.program_id(1)
    tk = tq
    G, R = NSA_GROUPS, NSA_REP
    nb = kc_ref.shape[1]
    n_prev = WINDOW // tk
    _padded_qt(q_ref, cos_ref[...], sin_ref[...], qt_ref, tq, R)
    m_ref[...] = jnp.full(m_ref.shape, NEG_INF, F32)
    l_ref[...] = jnp.zeros(l_ref.shape, F32)
    acc_ref[...] = jnp.zeros(acc_ref.shape, F32)

    tpos = qi * tq + (lax.broadcasted_iota(jnp.int32, (nb, R * tq), 1) & (tq - 1))
    nrow = lax.broadcasted_iota(jnp.int32, (nb, R * tq), 0)
    cmask = (nrow * CMP_STRIDE + (CMP_LEN - 1) <= tpos) & (nrow < nb - 1)
    oj = lax.broadcasted_iota(jnp.int32, (n_sel, nb), 0)
    on = lax.broadcasted_iota(jnp.int32, (n_sel, nb), 1) * CMP_STRIDE
    overlap_t = jnp.where((on < (oj + 1) * SEL_BLOCK) & (on + CMP_LEN > oj * SEL_BLOCK), 1.0, 0.0).astype(BF16)
    blk = lax.broadcasted_iota(jnp.int32, (n_sel, tq), 0)
    trow = qi * tq + lax.broadcasted_iota(jnp.int32, (n_sel, tq), 1)
    cur = trow // SEL_BLOCK
    forced = (blk == 0) | (blk == cur) | (blk == cur - 1)
    future = blk * SEL_BLOCK > trow
    kc = kc_ref[0]
    for g in range(G):
        s = jnp.where(cmask, _dot(kc, qt_ref[g]), NEG_INF)
        mx = jnp.max(s, axis=0, keepdims=True)
        e = jnp.where(cmask, jnp.exp(s - mx), 0.0)
        p = e / jnp.maximum(jnp.sum(e, axis=0, keepdims=True), 1e-30)
        oct_ref[g] = _dot(vct_ref[0, g * HEAD_DIM:(g + 1) * HEAD_DIM, :], p.astype(BF16))
        psum = p[:, 0:tq]
        for r in range(1, R):
            psum = psum + p[:, r * tq:(r + 1) * tq]
        hi, mid, lo = _split3(psum)
        imp = (_dot(overlap_t, hi) + _dot(overlap_t, mid)) + _dot(overlap_t, lo)
        imp = jnp.where(forced, FORCE_SCORE, jnp.where(future, -FORCE_SCORE, imp))
        rank = jnp.zeros((n_sel, tq), F32)
        for i in range(n_sel):
            ri = imp[i:i + 1, :]
            ahead = (ri > imp) | ((ri == imp) & (blk > i))
            rank = rank + jnp.where(ahead, 1.0, 0.0)
        picked = jnp.where(rank < float(min(SEL_TOPK, n_sel)), 1.0, 0.0).astype(BF16)
        for r in range(R):
            selt_ref[g, :, r * tq:(r + 1) * tq] = picked

    qpos = qi * tq + (lax.broadcasted_iota(jnp.int32, (tk, R * tq), 1) & (tq - 1))
    krow = lax.broadcasted_iota(jnp.int32, (tk, R * tq), 0)
    er = lax.broadcasted_iota(jnp.int32, (tk, n_sel), 0)
    ec = lax.broadcasted_iota(jnp.int32, (tk, n_sel), 1)

    def sel_tile(j, diagonal):
        k = ks_ref[0, pl.ds(pl.multiple_of(j * tk, tk), tk), :]
        expand_t = jnp.where((j * tk + er) // SEL_BLOCK == ec, 1.0, 0.0).astype(BF16)
        for g in range(G):
            chosen = _dot(expand_t, selt_ref[g]) > 0.5
            if diagonal:
                chosen = chosen & (j * tk + krow <= qpos)
            s = jnp.where(chosen, _dot(k, qt_ref[g]), NEG_INF)
            _online_update(s, vst_ref[0, j, g * HEAD_DIM:(g + 1) * HEAD_DIM, :], m_ref, l_ref, acc_ref, g)

    def sel_body(j, c):
        sel_tile(j, False)
        return c

    lax.fori_loop(0, qi, sel_body, 0)
    sel_tile(qi, True)

    for back in range(n_prev, -1, -1):
        @pl.when(qi - back >= 0)
        def _(back=back):
            j = qi - back
            k = kw_ref[0, pl.ds(pl.multiple_of(j * tk, tk), tk), :]
            diff = qpos - (j * tk + krow)
            if back == 0:
                mask = diff >= 0
            elif back == n_prev:
                mask = diff < WINDOW
            else:
                mask = None
            for g in range(G):
                s = _dot(k, qt_ref[g])
                if mask is not None:
                    s = jnp.where(mask, s, NEG_INF)
                _online_update(s, vwt_ref[0, j, g * HEAD_DIM:(g + 1) * HEAD_DIM, :], m_ref, l_ref, acc_ref,
                               G + g, mask if back == n_prev else None)

    gates_t = jax.nn.sigmoid(sm_ref[0]).T
    rows = []
    for h in range(NSA_HEADS):
        g, r = h // R, h % R
        cols = slice(r * tq, (r + 1) * tq)
        gate = lambda br: gates_t[GL_OFF + 3 * h + br:GL_OFF + 3 * h + br + 1, :]
        o_sel = acc_ref[g, :, cols] / jnp.maximum(l_ref[g, :, cols], 1e-30)
        o_win = acc_ref[G + g, :, cols] / jnp.maximum(l_ref[G + g, :, cols], 1e-30)
        rows.append(gate(0) * oct_ref[g, :, cols] + gate(1) * o_sel + gate(2) * o_win)
    o_ref[0] = jnp.concatenate(rows, axis=0).T


def _nsa_attend(proj3, cos2, sin2, kcmp, vcmp_t, ks, vst, kw, vwt, tq):
    b, s, _ = proj3.shape
    n_sel = s // SEL_BLOCK
    wq = NSA_HEADS * HEAD_DIM
    whole = lambda a: pl.BlockSpec((1,) + a.shape[1:], lambda i, j: (i,) + (0,) * (a.ndim - 1))
    return pl.pallas_call(
        functools.partial(_nsa_kernel, tq=tq, n_sel=n_sel),
        out_shape=jax.ShapeDtypeStruct((b, s, wq), F32),
        grid=(b, s // tq),
        in_specs=[pl.BlockSpec((1, tq, wq), lambda i, j: (i, j, C_QA // wq)),
                  pl.BlockSpec((tq, LANES), lambda i, j: (j, 0)),
                  pl.BlockSpec((tq, LANES), lambda i, j: (j, 0)),
                  whole(kcmp), whole(vcmp_t), whole(ks), whole(vst), whole(kw), whole(vwt),
                  pl.BlockSpec((1, tq, LANES), lambda i, j: (i, j, C_SMALL // LANES))],
        out_specs=pl.BlockSpec((1, tq, wq), lambda i, j: (i, j, 0)),
        scratch_shapes=[pltpu.VMEM((NSA_GROUPS, LANES, NSA_REP * tq), BF16),
                        pltpu.VMEM((NSA_GROUPS, n_sel, NSA_REP * tq), BF16),
                        pltpu.VMEM((NSA_GROUPS, HEAD_DIM, NSA_REP * tq), F32),
                        pltpu.VMEM((2 * NSA_GROUPS, 1, NSA_REP * tq), F32),
                        pltpu.VMEM((2 * NSA_GROUPS, 1, NSA_REP * tq), F32),
                        pltpu.VMEM((2 * NSA_GROUPS, HEAD_DIM, NSA_REP * tq), F32)],
        compiler_params=_cparams(("parallel", "parallel")),
        name="nsa_attend",
    )(proj3, cos2, sin2, kcmp, vcmp_t, ks, vst, kw, vwt, proj3)


def _fox_kernel(q_ref, k_ref, vt_ref, cq_ref, ck_ref, o_ref, qt_ref, m_ref, l_ref, acc_ref, *, tq):
    qi = pl.program_id(1)
    tk = tq
    _padded_qt(q_ref, None, None, qt_ref, tq, 1)
    m_ref[...] = jnp.full(m_ref.shape, NEG_INF, F32)
    l_ref[...] = jnp.zeros(l_ref.shape, F32)
    acc_ref[...] = jnp.zeros(acc_ref.shape, F32)
    cq = cq_ref[0]
    causal = lax.broadcasted_iota(jnp.int32, (tk, tq), 0) <= lax.broadcasted_iota(jnp.int32, (tk, tq), 1)

    def tile(j, diagonal):
        rows = pl.ds(pl.multiple_of(j * tk, tk), tk)
        ck = ck_ref[0, rows, :]
        for h in range(FOX_HEADS):
            hp = h // 2
            s = _dot(k_ref[0, rows, hp * LANES:(hp + 1) * LANES], qt_ref[h])
            s = (s - ck[:, FL_OFF + h:FL_OFF + h + 1]) + cq[FL_OFF + h:FL_OFF + h + 1, :]
            if diagonal:
                s = jnp.where(causal, s, NEG_INF)
            _online_update(s, vt_ref[0, j, h * HEAD_DIM:(h + 1) * HEAD_DIM, :], m_ref, l_ref, acc_ref, h)

    def body(j, c):
        tile(j, False)
        return c

    lax.fori_loop(0, qi, body, 0)
    tile(qi, True)
    rows = [acc_ref[h] / jnp.maximum(l_ref[h], 1e-30) for h in range(FOX_HEADS)]
    o_ref[0] = jnp.concatenate(rows, axis=0).T


def _fox_attend(proj3, kf, vft, cum, cum_t, tq):
    b, s, _ = proj3.shape
    w = FOX_HEADS * HEAD_DIM
    whole = lambda a: pl.BlockSpec((1,) + a.shape[1:], lambda i, j: (i,) + (0,) * (a.ndim - 1))
    return pl.pallas_call(
        functools.partial(_fox_kernel, tq=tq),
        out_shape=jax.ShapeDtypeStruct((b, s, w), F32),
        grid=(b, s // tq),
        in_specs=[pl.BlockSpec((1, tq, w), lambda i, j: (i, j, C_QF // w)),
                  whole(kf), whole(vft),
                  pl.BlockSpec((1, LANES, tq), lambda i, j: (i, 0, j)),
                  whole(cum)],
        out_specs=pl.BlockSpec((1, tq, w), lambda i, j: (i, j, 0)),
        scratch_shapes=[pltpu.VMEM((FOX_HEADS, LANES, tq), BF16),
                        pltpu.VMEM((FOX_HEADS, 1, tq), F32),
                        pltpu.VMEM((FOX_HEADS, 1, tq), F32),
                        pltpu.VMEM((FOX_HEADS, HEAD_DIM, tq), F32)],
        compiler_params=_cparams(("parallel", "parallel")),
        name="fox_attend",
    )(proj3, kf, vft, cum_t, cum)


def _memkv_kernel(m_ref, g_ref, w_ref, o_ref):
    x = m_ref[0]
    ms = jnp.mean(x * x, axis=-1, keepdims=True)
    hn = (x * lax.rsqrt(ms + RMS_EPS) * g_ref[...]).astype(BF16)
    o_ref[0] = _dot(hn, w_ref[...]).astype(BF16)


def _mem_kv(mem, g, w_bf16):
    b, m, d = mem.shape
    n = w_bf16.shape[1]
    return pl.pallas_call(
        _memkv_kernel,
        out_shape=jax.ShapeDtypeStruct((b, m, n), BF16),
        grid=(b,),
        in_specs=[pl.BlockSpec((1, m, d), lambda i: (i, 0, 0)),
                  pl.BlockSpec((1, d), lambda i: (0, 0)),
                  pl.BlockSpec((d, n), lambda i: (0, 0))],
        out_specs=pl.BlockSpec((1, m, n), lambda i: (i, 0, 0)),
        compiler_params=_cparams(("parallel",)),
        name="mem_kv",
    )(mem, g, w_bf16)


def _memattn_kernel(q_ref, kv_ref, o_ref):
    w = MEM_HEADS * MEM_HEAD_DIM
    for h in range(MEM_HEADS):
        lo, hi = h * MEM_HEAD_DIM, (h + 1) * MEM_HEAD_DIM
        s = _dot_nt(q_ref[0, :, lo:hi].astype(BF16), kv_ref[0, :, lo:hi]) * (MEM_HEAD_DIM ** -0.5)
        e = jnp.exp(s - jnp.max(s, axis=-1, keepdims=True))
        p = e / jnp.sum(e, axis=-1, keepdims=True)
        o_ref[0, :, lo:hi] = _dot(p.astype(BF16), kv_ref[0, :, w + lo:w + hi])


def _mem_attend(proj3, kv, tq=512):
    b, s, _ = proj3.shape
    m = kv.shape[1]
    w = MEM_HEADS * MEM_HEAD_DIM
    return pl.pallas_call(
        _memattn_kernel,
        out_shape=jax.ShapeDtypeStruct((b, s, w), F32),
        grid=(b, s // tq),
        in_specs=[pl.BlockSpec((1, tq, w), lambda i, j: (i, j, C_QM // w)),
                  pl.BlockSpec((1, m, 2 * w), lambda i, j: (i, 0, 0))],
        out_specs=pl.BlockSpec((1, tq, w), lambda i, j: (i, j, 0)),
        compiler_params=_cparams(("parallel", "parallel")),
        name="mem_attend",
    )(proj3, kv)


def _merge_kernel(on_ref, of_ref, om_ref, ln_ref, lf_ref, lm_ref, bm_ref, wn_ref, wf_ref, wm_ref,
                  wo_ref, x_ref, g_ref, wr_ref, br_ref, x1_ref, hn_ref, idx_ref, gate_ref):
    d = x_ref.shape[1]

    def branch(o_ref, l_ref, w_ref, k):
        gate = jax.nn.sigmoid(l_ref[...] + bm_ref[:, k * d:(k + 1) * d])
        return gate * _dot(o_ref[...].astype(BF16), w_ref[...])

    merged = branch(on_ref, ln_ref, wn_ref, 0) + branch(of_ref, lf_ref, wf_ref, 1) + branch(om_ref, lm_ref, wm_ref, 2)
    x1 = x_ref[...] + _dot(merged.astype(BF16), wo_ref[...])
    x1_ref[...] = x1
    ms = jnp.mean(x1 * x1, axis=-1, keepdims=True)
    hn = x1 * lax.rsqrt(ms + RMS_EPS) * g_ref[...]
    hn_ref[...] = hn
    h_hi = hn.astype(BF16)
    h_lo = (hn - h_hi.astype(F32)).astype(BF16)
    wr = wr_ref[...]
    w_hi = wr.astype(BF16)
    w_lo = (wr - w_hi.astype(F32)).astype(BF16)
    logits = (_dot(h_hi, w_hi) + (_dot(h_lo, w_hi) + _dot(h_hi, w_lo))) + br_ref[...]
    tm, ne = logits.shape
    lane_e = lax.broadcasted_iota(jnp.int32, (tm, ne), 1)
    lane = lax.broadcasted_iota(jnp.int32, (tm, LANES), 1)
    idx_slab = jnp.zeros((tm, LANES), jnp.int32)
    val_slab = jnp.zeros((tm, LANES), F32)
    work = logits
    vals = []
    for k in range(TOP_K):
        m = jnp.max(work, axis=-1, keepdims=True)
        idx = jnp.min(jnp.where(work == m, lane_e, ne), axis=-1, keepdims=True)
        work = jnp.where(lane_e == idx, -jnp.inf, work)
        idx_slab = jnp.where(lane == k, idx, idx_slab)
        vals.append(m)
    es = [jnp.exp(v - vals[0]) for v in vals]
    tot = es[0]
    for e in es[1:]:
        tot = tot + e
    for k in range(TOP_K):
        val_slab = jnp.where(lane == k, es[k] / tot, val_slab)
    idx_ref[...] = idx_slab
    gate_ref[...] = val_slab


def _merge_route(o_nsa, o_fox, o_mem, proj, b_merge, wn, wf, wm, wo, x2, g_ffn, w_router, b_router, tm=256):
    t, d = x2.shape
    wb = o_nsa.shape[1]
    row = lambda w: pl.BlockSpec((tm, w), lambda i: (i, 0))
    full = lambda a: pl.BlockSpec(a.shape, lambda i: (0,) * a.ndim)
    return pl.pallas_call(
        _merge_kernel,
        out_shape=(jax.ShapeDtypeStruct((t, d), F32), jax.ShapeDtypeStruct((t, d), F32),
                   jax.ShapeDtypeStruct((t, LANES), jnp.int32), jax.ShapeDtypeStruct((t, LANES), F32)),
        grid=(t // tm,),
        in_specs=[row(wb), row(wb), row(wb),
                  pl.BlockSpec((tm, d), lambda i: (i, 0)),
                  pl.BlockSpec((tm, d), lambda i: (i, 1)),
                  pl.BlockSpec((tm, d), lambda i: (i, 2)),
                  full(b_merge), full(wn), full(wf), full(wm), full(wo),
                  row(d), full(g_ffn), full(w_router), full(b_router)],
        out_specs=(row(d), row(d), row(LANES), row(LANES)),
        compiler_params=_cparams(("parallel",)),
        name="merge_route",
    )(o_nsa, o_fox, o_mem, proj, proj, proj, b_merge, wn, wf, wm, wo, x2, g_ffn, w_router, b_router)


def _rank_kernel(idx_ref, rank_ref, cnt_ref, carry_ref, *, tm):
    @pl.when(pl.program_id(0) == 0)
    def _():
        carry_ref[...] = jnp.zeros(carry_ref.shape, F32)

    idx = idx_ref[...]
    lane_e = lax.broadcasted_iota(jnp.int32, (tm, N_EXPERTS), 1)
    hots = [jnp.where(idx[:, k:k + 1] == lane_e, 1.0, 0.0) for k in range(TOP_K)]
    cnt = hots[0]
    for hk in hots[1:]:
        cnt = cnt + hk
    r = lax.broadcasted_iota(jnp.int32, (tm, tm), 0)
    c = lax.broadcasted_iota(jnp.int32, (tm, tm), 1)
    strict = jnp.where(c < r, 1.0, 0.0).astype(BF16)
    before = _dot(strict, cnt.astype(BF16)) + carry_ref[...]
    lane = lax.broadcasted_iota(jnp.int32, (tm, LANES), 1)
    slab = jnp.zeros((tm, LANES), F32)
    for k in range(TOP_K):
        slab = jnp.where(lane == k, jnp.sum(hots[k] * before, axis=-1, keepdims=True), slab)
    rank_ref[...] = slab.astype(jnp.int32)
    total = carry_ref[...] + jnp.sum(cnt, axis=0, keepdims=True)
    carry_ref[...] = total
    cnt_ref[...] = jnp.broadcast_to(total, cnt_ref.shape).astype(jnp.int32)


def _moe_rank(idx_slab, tm=256):
    t = idx_slab.shape[0]
    return pl.pallas_call(
        functools.partial(_rank_kernel, tm=tm),
        out_shape=(jax.ShapeDtypeStruct((t, LANES), jnp.int32),
                   jax.ShapeDtypeStruct((8, N_EXPERTS), jnp.int32)),
        grid=(t // tm,),
        in_specs=[pl.BlockSpec((tm, LANES), lambda i: (i, 0))],
        out_specs=(pl.BlockSpec((tm, LANES), lambda i: (i, 0)),
                   pl.BlockSpec((8, N_EXPERTS), lambda i: (0, 0))),
        scratch_shapes=[pltpu.VMEM((1, N_EXPERTS), F32)],
        compiler_params=_cparams(("arbitrary",)),
        name="moe_rank",
    )(idx_slab)


def _expert_kernel(be_ref, nu_ref, tok_ref, hn_ref, wg_ref, wu_ref, bg_ref, bu_ref, wd_ref, bd_ref, o_ref,
                   buf_ref, act_ref, sem, *, unroll, chunk):
    i = pl.program_id(0)
    n_used = nu_ref[0]

    def row_copy(tok, r, s):
        return pltpu.make_async_copy(hn_ref.at[pl.ds(tok, 1)], buf_ref.at[s, pl.ds(r, 1)], sem.at[s])

    def wait_rows(s):
        def wait(r, c):
            row_copy(0, 0, s).wait()
            return c

        lax.fori_loop(0, MOE_ROWS, wait, 0, unroll=unroll)

    @pl.when((i == 0) & (n_used > 0))
    def _():
        def body(r, c):
            row_copy(tok_ref[r], r, 0).start()
            return c

        lax.fori_loop(0, MOE_ROWS, body, 0, unroll=unroll)

    @pl.when(i < n_used)
    def _():
        cur = i % 2
        nxt = 1 - cur
        base = jnp.minimum(i + 1, n_used - 1) * MOE_ROWS
        f = wg_ref.shape[2]
        d = wd_ref.shape[2]
        n_groups = f // chunk + d // chunk
        per = MOE_ROWS // n_groups

        def issue_group(k):
            for r in range(k * per, (k + 1) * per):
                row_copy(tok_ref[base + r], r, nxt).start()

        wait_rows(cur)
        x = buf_ref[cur].astype(BF16)
        for c in range(f // chunk):
            cols = slice(c * chunk, (c + 1) * chunk)
            g = jnp.minimum(_dot(x, wg_ref[0, :, cols]) + bg_ref[0, :, cols], SWIGLU_LIMIT)
            u = jnp.clip(_dot(x, wu_ref[0, :, cols]) + bu_ref[0, :, cols], -SWIGLU_LIMIT, SWIGLU_LIMIT)
            act_ref[:, cols] = ((u + 1.0) * g * jax.nn.sigmoid(SWIGLU_ALPHA * g)).astype(BF16)
            issue_group(c)
        act = act_ref[...]
        for c in range(d // chunk):
            cols = slice(c * chunk, (c + 1) * chunk)
            o_ref[:, cols] = _dot(act, wd_ref[0, :, cols]) + bd_ref[0, :, cols]
            issue_group(f // chunk + c)

        @pl.when(i == n_used - 1)
        def _():
            wait_rows(nxt)

    @pl.when(i >= n_used)
    def _():
        o_ref[...] = jnp.zeros(o_ref.shape, o_ref.dtype)


def _moe_experts(block_exp, n_used, slot_tok, hn, wg, wu, bg, bu, wd, bd):
    n_slots = slot_tok.shape[0]
    d = hn.shape[1]
    f = wg.shape[2]
    nblk = n_slots // MOE_ROWS
    wspec = lambda shp: pl.BlockSpec((1,) + shp, lambda i, be, nu, tk: (be[i], 0, 0))
    return pl.pallas_call(
        functools.partial(_expert_kernel, unroll=8, chunk=256),
        out_shape=jax.ShapeDtypeStruct((n_slots, d), F32),
        grid_spec=pltpu.PrefetchScalarGridSpec(
            num_scalar_prefetch=3,
            grid=(nblk,),
            in_specs=[pl.BlockSpec(memory_space=pl.ANY),
                      wspec((d, f)), wspec((d, f)), wspec((1, f)), wspec((1, f)),
                      wspec((f, d)), wspec((1, d))],
            out_specs=pl.BlockSpec((MOE_ROWS, d), lambda i, be, nu, tk: (i, 0)),
            scratch_shapes=[pltpu.VMEM((2, MOE_ROWS, d), hn.dtype), pltpu.VMEM((MOE_ROWS, f), BF16),
                            pltpu.SemaphoreType.DMA((2,))]),
        compiler_params=_cparams(("arbitrary",)),
        name="moe_experts",
    )(block_exp, n_used, slot_tok, hn, wg, wu, bg, bu, wd, bd)


def _combine_kernel(dest_ref, gate_ref, x1_ref, g_ref, ys_ref, o_ref, buf_ref, sem, *, tm, unroll):
    i = pl.program_id(0)
    n = pl.num_programs(0)

    def row_copy(slot_row, k, r, s):
        return pltpu.make_async_copy(ys_ref.at[pl.ds(slot_row, 1)], buf_ref.at[s, k, pl.ds(r, 1)], sem.at[s])

    def issue(step, s):
        base = step * (tm * TOP_K)

        def body(r, c):
            for k in range(TOP_K):
                row_copy(dest_ref[base + r * TOP_K + k], k, r, s).start()
            return c

        lax.fori_loop(0, tm, body, 0, unroll=unroll)

    @pl.when(i == 0)
    def _():
        issue(0, 0)

    @pl.when(i + 1 < n)
    def _():
        issue(i + 1, (i + 1) % 2)

    cur = i % 2

    def wait(r, c):
        row_copy(0, 0, 0, cur).wait()
        return c

    lax.fori_loop(0, tm * TOP_K, wait, 0, unroll=unroll)

    gate = gate_ref[...]
    y = x1_ref[...]
    for k in range(TOP_K):
        y = y + gate[:, k:k + 1] * buf_ref[cur, k]
    ms = jnp.mean(y * y, axis=-1, keepdims=True)
    o_ref[...] = y * lax.rsqrt(ms + RMS_EPS) * g_ref[...]


def _moe_combine(dest_flat, gate_slab, x1, g_final, ys, tm=128):
    t, d = x1.shape
    return pl.pallas_call(
        functools.partial(_combine_kernel, tm=tm, unroll=8),
        out_shape=jax.ShapeDtypeStruct((t, d), F32),
        grid_spec=pltpu.PrefetchScalarGridSpec(
            num_scalar_prefetch=1,
            grid=(t // tm,),
            in_specs=[pl.BlockSpec((tm, LANES), lambda i, dst: (i, 0)),
                      pl.BlockSpec((tm, d), lambda i, dst: (i, 0)),
                      pl.BlockSpec((1, d), lambda i, dst: (0, 0)),
                      pl.BlockSpec(memory_space=pl.ANY)],
            out_specs=pl.BlockSpec((tm, d), lambda i, dst: (i, 0)),
            scratch_shapes=[pltpu.VMEM((2, TOP_K, tm, d), F32), pltpu.SemaphoreType.DMA((2,))]),
        compiler_params=_cparams(("arbitrary",)),
        name="moe_combine",
    )(dest_flat, gate_slab, x1, g_final, ys)


def _rope_tables(s):
    inv = ROPE_THETA ** (-jnp.arange(0, HEAD_DIM, 2, dtype=F32) / HEAD_DIM)
    ang = jnp.arange(s, dtype=F32)[:, None] * inv[None, :]
    cos, sin = jnp.cos(ang), jnp.sin(ang)
    return jnp.concatenate([cos, cos], axis=-1), jnp.concatenate([-sin, sin], axis=-1)


def _permute_w_in(w):
    d = w.shape[0]
    pad = N_PROJ - (C_SMALL + 32)
    cols = [w[:, 3360:6432], w[:, 0:512], w[:, 1304:1816], w[:, 1816:2328], w[:, 2328:2840],
            w[:, 2848:3360], w[:, 512:1280], w[:, 1280:1304], w[:, 2840:2848], jnp.zeros((d, pad), w.dtype)]
    return jnp.concatenate(cols, axis=1).astype(BF16)


def _layer(x, mem, g_mix, w_in, b_forget, b_merge, pe_k, w1_k, w2_k, pe_v, w1_v, w2_v, g_mem, w_mem_kv,
           w_br_nsa, w_br_fox, w_br_mem, w_out, g_ffn, w_router, b_router, w_gate_up, b_gate_up,
           w_down, b_down, g_final):
    b, s, d = x.shape
    t = b * s
    x2 = x.reshape(t, d)
    cos64, sin64 = _rope_tables(s)
    cos2 = jnp.concatenate([cos64, cos64], axis=-1)
    sin2 = jnp.concatenate([sin64, sin64], axis=-1)

    proj = _inproj(x2, g_mix.reshape(1, d), _permute_w_in(w_in), tm=min(1024, t))
    proj3 = proj.reshape(b, s, N_PROJ)
    b_row = jnp.zeros((1, LANES), F32).at[0, FL_OFF:FL_OFF + FOX_HEADS].set(b_forget)
    cum, cum_t = _fox_cum(proj3, b_row)

    nb = s // CMP_STRIDE
    wide = CMP_STRIDE * HEAD_DIM

    def to_blocks(c0):
        a = proj3[:, :, c0:c0 + NSA_GROUPS * HEAD_DIM].reshape(b, nb, CMP_STRIDE, NSA_GROUPS, HEAD_DIM)
        return a.transpose(0, 3, 1, 2, 4).reshape(b, NSA_GROUPS, nb, wide)

    kcmp, vcmp_t = _nsa_compress(
        to_blocks(C_NSAKV), to_blocks(C_NSAKV + LANES), cos64.reshape(nb, wide), sin64.reshape(nb, wide),
        pe_k.reshape(2, wide), pe_v.reshape(2, wide),
        w1_k.reshape(CMP_LEN * HEAD_DIM, HEAD_DIM).astype(BF16), w1_v.reshape(CMP_LEN * HEAD_DIM, HEAD_DIM).astype(BF16),
        w2_k.astype(BF16), w2_v.astype(BF16))
    fox_tile = min(FOX_TILE, s)
    kf, vft, ks, vst, kw, vwt = _attn_prep(proj3, cos2, sin2, fox_tile, NSA_TILE)
    o_nsa = _nsa_attend(proj3, cos2, sin2, kcmp, vcmp_t, ks, vst, kw, vwt, NSA_TILE)
    o_fox = _fox_attend(proj3, kf, vft, cum, cum_t, fox_tile)
    mem_kv = _mem_kv(mem, g_mem.reshape(1, d), w_mem_kv.astype(BF16))
    o_mem = _mem_attend(proj3, mem_kv)

    x1, hn, idx_slab, gate_slab = _merge_route(
        o_nsa.reshape(t, -1), o_fox.reshape(t, -1), o_mem.reshape(t, -1), proj, b_merge.reshape(1, -1),
        w_br_nsa.astype(BF16), w_br_fox.astype(BF16), w_br_mem.astype(BF16), w_out.astype(BF16),
        x2, g_ffn.reshape(1, d), w_router, b_router.reshape(1, -1))

    rank_slab, cnt8 = _moe_rank(idx_slab)
    counts = cnt8[0]
    padded = (counts + MOE_ROWS - 1) // MOE_ROWS * MOE_ROWS
    pad_end = jnp.cumsum(padded)
    pad_start = pad_end - padded
    n_assign = t * TOP_K
    nblk = -(-(n_assign + N_EXPERTS * (MOE_ROWS - 1)) // MOE_ROWS)
    n_slots = nblk * MOE_ROWS
    top_idx = idx_slab[:, :TOP_K]
    dest = (pad_start[top_idx] + rank_slab[:, :TOP_K]).reshape(-1).astype(jnp.int32)
    tok_flat = jnp.repeat(jnp.arange(t, dtype=jnp.int32), TOP_K)
    slot_tok = jnp.zeros((n_slots,), jnp.int32).at[dest].set(tok_flat, unique_indices=True)
    blk_start = jnp.arange(nblk, dtype=jnp.int32) * MOE_ROWS
    block_exp = jnp.minimum(jnp.sum(blk_start[:, None] >= pad_end[None, :], axis=1),
                            N_EXPERTS - 1).astype(jnp.int32)
    n_used = (pad_end[-1] // MOE_ROWS).astype(jnp.int32).reshape(1)

    f = w_down.shape[1]
    wgu = w_gate_up.reshape(N_EXPERTS, d, f, 2)
    bgu = b_gate_up.reshape(N_EXPERTS, 1, f, 2)
    ys = _moe_experts(block_exp, n_used, slot_tok, hn, wgu[..., 0].astype(BF16), wgu[..., 1].astype(BF16),
                      bgu[..., 0], bgu[..., 1], w_down.astype(BF16), b_down.reshape(N_EXPERTS, 1, d))
    out = _moe_combine(dest, gate_slab, x1, g_final.reshape(1, d), ys)
    return out.reshape(b, s, d)


def kernel(x, mem, g_mix, w_in, b_forget, b_merge, nsa_pe_k, nsa_w1_k, nsa_w2_k, nsa_pe_v, nsa_w1_v, nsa_w2_v, g_mem, w_mem_kv, w_branch_nsa, w_branch_fox, w_branch_mem, w_out, g_ffn, w_router, b_router, w_gate_up, b_gate_up, w_down, b_down, g_final):
    assert g_mix.shape[0] == 1, "single-layer trunk"
    return _layer(x, mem, g_mix[0], w_in[0], b_forget[0], b_merge[0], nsa_pe_k[0], nsa_w1_k[0], nsa_w2_k[0],
                  nsa_pe_v[0], nsa_w1_v[0], nsa_w2_v[0], g_mem[0], w_mem_kv[0], w_branch_nsa[0],
                  w_branch_fox[0], w_branch_mem[0], w_out[0], g_ffn[0], w_router[0], b_router[0],
                  w_gate_up[0], b_gate_up[0], w_down[0], b_down[0], g_final)
```

```python
import functools

import jax
import jax.numpy as jnp
import numpy as np
from jax import lax
from jax.experimental import pallas as pl
from jax.experimental.pallas import tpu as pltpu

F32 = jnp.float32
BF16 = jnp.bfloat16

D_MODEL = 1024
HEAD_DIM = 64
ROPE_THETA = 10000.0
RMS_EPS = 1e-5
NEG_INF = -1e30
FORCE_SCORE = 1e4
NSA_HEADS = 8
NSA_GROUPS = 2
NSA_REP = NSA_HEADS // NSA_GROUPS
CMP_LEN = 32
CMP_STRIDE = 16
SEL_BLOCK = 64
SEL_TOPK = 16
WINDOW = 512
FOX_HEADS = 8
MEM_HEADS = 4
MEM_HEAD_DIM = 128
N_EXPERTS = 32
TOP_K = 4
SWIGLU_LIMIT = 7.0
SWIGLU_ALPHA = 1.702

LANES = 128
VMEM_LIMIT = 48 * 1024 * 1024

C_MERGE = 0
C_QA = 3072
C_QF = 3584
C_KF = 4096
C_VF = 4608
C_QM = 5120
C_NSAKV = 5632
C_SMALL = 6400
N_PROJ = 6656
GL_OFF = 0
FL_OFF = 24

MOE_ROWS = 256
NSA_TILE = 256
FOX_TILE = 512


def _cparams(sem, vmem=VMEM_LIMIT):
    return pltpu.CompilerParams(dimension_semantics=sem, vmem_limit_bytes=vmem)


def _dot(a, b):
    return jnp.dot(a, b, preferred_element_type=F32)


def _dot_nt(a, b):
    return lax.dot_general(a, b, (((1,), (1,)), ((), ())), preferred_element_type=F32)


def _rope(x, cos, sin_signed):
    w = x.shape[-1]
    lane = lax.broadcasted_iota(jnp.int32, x.shape, x.ndim - 1)
    first = (lane & (HEAD_DIM - 1)) < (HEAD_DIM // 2)
    rot = jnp.where(first, pltpu.roll(x, w - HEAD_DIM // 2, x.ndim - 1),
                    pltpu.roll(x, HEAD_DIM // 2, x.ndim - 1))
    return x * cos + rot * sin_signed


def _split3(x):
    hi = x.astype(BF16)
    r1 = x - hi.astype(F32)
    mid = r1.astype(BF16)
    lo = (r1 - mid.astype(F32)).astype(BF16)
    return hi, mid, lo


def _inproj_kernel(x_ref, g_ref, w_ref, o_ref, hn_ref):
    @pl.when(pl.program_id(1) == 0)
    def _():
        x = x_ref[...]
        ms = jnp.mean(x * x, axis=-1, keepdims=True)
        hn_ref[...] = (x * lax.rsqrt(ms + RMS_EPS) * g_ref[...]).astype(BF16)

    o_ref[...] = _dot(hn_ref[...], w_ref[...])


def _inproj(x2, g, w_bf16, tm=1024, tn=512):
    t, d = x2.shape
    n = w_bf16.shape[1]
    return pl.pallas_call(
        _inproj_kernel,
        out_shape=jax.ShapeDtypeStruct((t, n), F32),
        grid=(t // tm, n // tn),
        in_specs=[pl.BlockSpec((tm, d), lambda i, j: (i, 0)),
                  pl.BlockSpec((1, d), lambda i, j: (0, 0)),
                  pl.BlockSpec((d, tn), lambda i, j: (0, j))],
        out_specs=pl.BlockSpec((tm, tn), lambda i, j: (i, j)),
        scratch_shapes=[pltpu.VMEM((tm, d), BF16)],
        compiler_params=_cparams(("parallel", "arbitrary")),
        name="inproj",
    )(x2, g, w_bf16)


def _cum_kernel(s_ref, b_ref, c_ref, ct_ref, *, blk):
    s = s_ref.shape[1]
    z = s_ref[0] + b_ref[...]
    logf = jnp.minimum(z, 0.0) - jnp.log1p(jnp.exp(-jnp.abs(z)))
    r = lax.broadcasted_iota(jnp.int32, (blk, blk), 0)
    c = lax.broadcasted_iota(jnp.int32, (blk, blk), 1)
    tri = jnp.where(c <= r, 1.0, 0.0).astype(BF16)
    carry = jnp.zeros((1, LANES), F32)
    for i in range(s // blk):
        hi, mid, lo = _split3(logf[i * blk:(i + 1) * blk])
        loc = (_dot(tri, hi) + _dot(tri, mid)) + _dot(tri, lo)
        out = loc + carry
        c_ref[0, i * blk:(i + 1) * blk, :] = out
        carry = out[blk - 1:blk, :]
    ct_ref[0] = c_ref[0].T


def _fox_cum(proj3, b_row, blk=256):
    b, s, _ = proj3.shape
    return pl.pallas_call(
        functools.partial(_cum_kernel, blk=blk),
        out_shape=(jax.ShapeDtypeStruct((b, s, LANES), F32),
                   jax.ShapeDtypeStruct((b, LANES, s), F32)),
        grid=(b,),
        in_specs=[pl.BlockSpec((1, s, LANES), lambda i: (i, 0, C_SMALL // LANES)),
                  pl.BlockSpec((1, LANES), lambda i: (0, 0))],
        out_specs=(pl.BlockSpec((1, s, LANES), lambda i: (i, 0, 0)),
                   pl.BlockSpec((1, LANES, s), lambda i: (i, 0, 0))),
        compiler_params=_cparams(("parallel",)),
        name="fox_cum",
    )(proj3, b_row)


def _cmp_kernel(ak_ref, av_ref, cos_ref, sin_ref, pek_ref, pev_ref, w1k_ref, w1v_ref,
                w2k_ref, w2v_ref, kc_ref, vct_ref, vbuf_ref):
    nb = ak_ref.shape[2]
    half = ak_ref.shape[3]

    def mlp(a, pe_ref, w1_ref, w2_ref):
        pa = _dot((a + pe_ref[0:1, :]).astype(BF16), w1_ref[0:half, :])
        pb = _dot((a + pe_ref[1:2, :]).astype(BF16), w1_ref[half:2 * half, :])
        z = pa + pltpu.roll(pb, nb - 1, 0)
        h = z * jax.nn.sigmoid(z)
        return _dot(h.astype(BF16), w2_ref[...])

    for g in range(NSA_GROUPS):
        lo, hi = g * HEAD_DIM, (g + 1) * HEAD_DIM
        kc_ref[0, :, lo:hi] = mlp(_rope(ak_ref[0, g], cos_ref[...], sin_ref[...]),
                                  pek_ref, w1k_ref, w2k_ref).astype(BF16)
        vbuf_ref[:, lo:hi] = mlp(av_ref[0, g], pev_ref, w1v_ref, w2v_ref)
    vct_ref[0] = vbuf_ref[...].T.astype(BF16)


def _nsa_compress(ak, av, cos_a, sin_a, pek, pev, w1k, w1v, w2k, w2v):
    b, g, nb, wide = ak.shape
    blk4 = pl.BlockSpec((1, g, nb, wide), lambda i: (i, 0, 0, 0))
    full = lambda shp: pl.BlockSpec(shp, lambda i: (0,) * len(shp))
    return pl.pallas_call(
        _cmp_kernel,
        out_shape=(jax.ShapeDtypeStruct((b, nb, g * HEAD_DIM), BF16),
                   jax.ShapeDtypeStruct((b, g * HEAD_DIM, nb), BF16)),
        grid=(b,),
        in_specs=[blk4, blk4, full(cos_a.shape), full(sin_a.shape), full(pek.shape), full(pev.shape),
                  full(w1k.shape), full(w1v.shape), full(w2k.shape), full(w2v.shape)],
        out_specs=(pl.BlockSpec((1, nb, g * HEAD_DIM), lambda i: (i, 0, 0)),
                   pl.BlockSpec((1, g * HEAD_DIM, nb), lambda i: (i, 0, 0))),
        scratch_shapes=[pltpu.VMEM((nb, g * HEAD_DIM), F32)],
        compiler_params=_cparams(("parallel",)),
        name="nsa_compress",
    )(ak, av, cos_a, sin_a, pek, pev, w1k, w1v, w2k, w2v)


def _prep_kernel(kf_ref, vf_ref, ks_ref, vs_ref, kw_ref, vw_ref, cos_ref, sin_ref,
                 okf_ref, ovf_ref, oks_ref, ovs_ref, okw_ref, ovw_ref):
    okf_ref[0] = kf_ref[0].astype(BF16)
    ovf_ref[0, 0] = vf_ref[0].T.astype(BF16)
    oks_ref[0] = _rope(ks_ref[0], cos_ref[...], sin_ref[...]).astype(BF16)
    okw_ref[0] = _rope(kw_ref[0], cos_ref[...], sin_ref[...]).astype(BF16)
    tn = ovs_ref.shape[3]
    for i in range(ovs_ref.shape[1]):
        ovs_ref[0, i] = vs_ref[0, i * tn:(i + 1) * tn, :].T.astype(BF16)
        ovw_ref[0, i] = vw_ref[0, i * tn:(i + 1) * tn, :].T.astype(BF16)


def _attn_prep(proj3, cos2, sin2, tk, tn):
    b, s, _ = proj3.shape
    wf = FOX_HEADS * HEAD_DIM
    kvb = C_NSAKV // LANES
    nk = s // tk
    sub = tk // tn
    col = lambda w, c: pl.BlockSpec((1, tk, w), lambda i, j: (i, j, c))
    rows = lambda w: pl.BlockSpec((1, tk, w), lambda i, j: (i, j, 0))
    ntile = pl.BlockSpec((1, sub, LANES, tn), lambda i, j: (i, j, 0, 0))
    return pl.pallas_call(
        _prep_kernel,
        out_shape=(jax.ShapeDtypeStruct((b, s, wf), BF16), jax.ShapeDtypeStruct((b, nk, wf, tk), BF16),
                   jax.ShapeDtypeStruct((b, s, LANES), BF16), jax.ShapeDtypeStruct((b, nk * sub, LANES, tn), BF16),
                   jax.ShapeDtypeStruct((b, s, LANES), BF16), jax.ShapeDtypeStruct((b, nk * sub, LANES, tn), BF16)),
        grid=(b, nk),
        in_specs=[col(wf, C_KF // wf), col(wf, C_VF // wf), col(LANES, kvb + 2), col(LANES, kvb + 3),
                  col(LANES, kvb + 4), col(LANES, kvb + 5),
                  pl.BlockSpec((tk, LANES), lambda i, j: (j, 0)), pl.BlockSpec((tk, LANES), lambda i, j: (j, 0))],
        out_specs=(rows(wf), pl.BlockSpec((1, 1, wf, tk), lambda i, j: (i, j, 0, 0)),
                   rows(LANES), ntile, rows(LANES), ntile),
        compiler_params=_cparams(("parallel", "parallel")),
        name="attn_prep",
    )(proj3, proj3, proj3, proj3, proj3, proj3, cos2, sin2)


def _online_update(s, vt, m_ref, l_ref, acc_ref, idx, mask=None):
    m_old = m_ref[idx]
    m_new = jnp.maximum(m_old, jnp.max(s, axis=0, keepdims=True))
    alpha = jnp.exp(m_old - m_new)
    p = jnp.exp(s - m_new)
    if mask is not None:
        p = jnp.where(mask, p, 0.0)
    l_ref[idx] = alpha * l_ref[idx] + jnp.sum(p, axis=0, keepdims=True)
    acc_ref[idx] = alpha * acc_ref[idx] + _dot(vt, p.astype(BF16))
    m_ref[idx] = m_new


def _padded_qt(q_ref, cos, sin, qt_ref, tq, heads_per_slot):
    qt_ref[...] = jnp.zeros(qt_ref.shape, qt_ref.dtype)
    n_heads = q_ref.shape[2] // HEAD_DIM
    for hp in range(n_heads // 2):
        q2 = q_ref[0, :, hp * LANES:(hp + 1) * LANES]
        if cos is not None:
            q2 = _rope(q2, cos, sin)
        qt = (q2 * (HEAD_DIM ** -0.5)).T.astype(BF16)
        for sub in range(2):
            h = 2 * hp + sub
            slot, r = h // heads_per_slot, h % heads_per_slot
            half = (slot % 2) if heads_per_slot > 1 else sub
            qt_ref[slot, half * HEAD_DIM:(half + 1) * HEAD_DIM, r * tq:(r + 1) * tq] = (
                qt[sub * HEAD_DIM:(sub + 1) * HEAD_DIM, :])


def _nsa_kernel(q_ref, cos_ref, sin_ref, kc_ref, vct_ref, ks_ref, vst_ref, kw_ref, vwt_ref, sm_ref, o_ref,
                qt_ref, selt_ref, oct_ref, m_ref, l_ref, acc_ref, *, tq, n_sel):
    qi = pl.program_id(1)
    tk = tq
    G, R = NSA_GROUPS, NSA_REP
    nb = kc_ref.shape[1]
    n_prev = WINDOW // tk
    _padded_qt(q_ref, cos_ref[...], sin_ref[...], qt_ref, tq, R)
    m_ref[...] = jnp.full(m_ref.shape, NEG_INF, F32)
    l_ref[...] = jnp.zeros(l_ref.shape, F32)
    acc_ref[...] = jnp.zeros(acc_ref.shape, F32)

    tpos = qi * tq + (lax.broadcasted_iota(jnp.int32, (nb, R * tq), 1) & (tq - 1))
    nrow = lax.broadcasted_iota(jnp.int32, (nb, R * tq), 0)
    cmask = (nrow * CMP_STRIDE + (CMP_LEN - 1) <= tpos) & (nrow < nb - 1)
    oj = lax.broadcasted_iota(jnp.int32, (n_sel, nb), 0)
    on = lax.broadcasted_iota(jnp.int32, (n_sel, nb), 1) * CMP_STRIDE
    overlap_t = jnp.where((on < (oj + 1) * SEL_BLOCK) & (on + CMP_LEN > oj * SEL_BLOCK), 1.0, 0.0).astype(BF16)
    blk = lax.broadcasted_iota(jnp.int32, (n_sel, tq), 0)
    trow = qi * tq + lax.broadcasted_iota(jnp.int32, (n_sel, tq), 1)
    cur = trow // SEL_BLOCK
    forced = (blk == 0) | (blk == cur) | (blk == cur - 1)
    future = blk * SEL_BLOCK > trow
    kc = kc_ref[0]
    for g in range(G):
        s = jnp.where(cmask, _dot(kc, qt_ref[g]), NEG_INF)
        mx = jnp.max(s, axis=0, keepdims=True)
        e = jnp.where(cmask, jnp.exp(s - mx), 0.0)
        p = e / jnp.maximum(jnp.sum(e, axis=0, keepdims=True), 1e-30)
        oct_ref[g] = _dot(vct_ref[0, g * HEAD_DIM:(g + 1) * HEAD_DIM, :], p.astype(BF16))
        psum = p[:, 0:tq]
        for r in range(1, R):
            psum = psum + p[:, r * tq:(r + 1) * tq]
        hi, mid, lo = _split3(psum)
        imp = (_dot(overlap_t, hi) + _dot(overlap_t, mid)) + _dot(overlap_t, lo)
        imp = jnp.where(forced, FORCE_SCORE, jnp.where(future, -FORCE_SCORE, imp))
        rank = jnp.zeros((n_sel, tq), F32)
        for i in range(n_sel):
            ri = imp[i:i + 1, :]
            ahead = (ri > imp) | ((ri == imp) & (blk > i))
            rank = rank + jnp.where(ahead, 1.0, 0.0)
        picked = jnp.where(rank < float(min(SEL_TOPK, n_sel)), 1.0, 0.0).astype(BF16)
        for r in range(R):
            selt_ref[g, :, r * tq:(r + 1) * tq] = picked

    qpos = qi * tq + (lax.broadcasted_iota(jnp.int32, (tk, R * tq), 1) & (tq - 1))
    krow = lax.broadcasted_iota(jnp.int32, (tk, R * tq), 0)
    er = lax.broadcasted_iota(jnp.int32, (tk, n_sel), 0)
    ec = lax.broadcasted_iota(jnp.int32, (tk, n_sel), 1)

    def sel_tile(j, diagonal):
        k = ks_ref[0, pl.ds(pl.multiple_of(j * tk, tk), tk), :]
        expand_t = jnp.where((j * tk + er) // SEL_BLOCK == ec, 1.0, 0.0).astype(BF16)
        for g in range(G):
            chosen = _dot(expand_t, selt_ref[g]) > 0.5
            if diagonal:
                chosen = chosen & (j * tk + krow <= qpos)
            s = jnp.where(chosen, _dot(k, qt_ref[g]), NEG_INF)
            _online_update(s, vst_ref[0, j, g * HEAD_DIM:(g + 1) * HEAD_DIM, :], m_ref, l_ref, acc_ref, g)

    def sel_body(j, c):
        sel_tile(j, False)
        return c

    lax.fori_loop(0, qi, sel_body, 0)
    sel_tile(qi, True)

    for back in range(n_prev, -1, -1):
        @pl.when(qi - back >= 0)
        def _(back=back):
            j = qi - back
            k = kw_ref[0, pl.ds(pl.multiple_of(j * tk, tk), tk), :]
            diff = qpos - (j * tk + krow)
            if back == 0:
                mask = diff >= 0
            elif back == n_prev:
                mask = diff < WINDOW
            else:
                mask = None
            for g in range(G):
                s = _dot(k, qt_ref[g])
                if mask is not None:
                    s = jnp.where(mask, s, NEG_INF)
                _online_update(s, vwt_ref[0, j, g * HEAD_DIM:(g + 1) * HEAD_DIM, :], m_ref, l_ref, acc_ref,
                               G + g, mask if back == n_prev else None)

    gates_t = jax.nn.sigmoid(sm_ref[0]).T
    rows = []
    for h in range(NSA_HEADS):
        g, r = h // R, h % R
        cols = slice(r * tq, (r + 1) * tq)
        gate = lambda br: gates_t[GL_OFF + 3 * h + br:GL_OFF + 3 * h + br + 1, :]
        o_sel = acc_ref[g, :, cols] / jnp.maximum(l_ref[g, :, cols], 1e-30)
        o_win = acc_ref[G + g, :, cols] / jnp.maximum(l_ref[G + g, :, cols], 1e-30)
        rows.append(gate(0) * oct_ref[g, :, cols] + gate(1) * o_sel + gate(2) * o_win)
    o_ref[0] = jnp.concatenate(rows, axis=0).T


def _nsa_attend(proj3, cos2, sin2, kcmp, vcmp_t, ks, vst, kw, vwt, tq):
    b, s, _ = proj3.shape
    n_sel = s // SEL_BLOCK
    wq = NSA_HEADS * HEAD_DIM
    whole = lambda a: pl.BlockSpec((1,) + a.shape[1:], lambda i, j: (i,) + (0,) * (a.ndim - 1))
    return pl.pallas_call(
        functools.partial(_nsa_kernel, tq=tq, n_sel=n_sel),
        out_shape=jax.ShapeDtypeStruct((b, s, wq), F32),
        grid=(b, s // tq),
        in_specs=[pl.BlockSpec((1, tq, wq), lambda i, j: (i, j, C_QA // wq)),
                  pl.BlockSpec((tq, LANES), lambda i, j: (j, 0)),
                  pl.BlockSpec((tq, LANES), lambda i, j: (j, 0)),
                  whole(kcmp), whole(vcmp_t), whole(ks), whole(vst), whole(kw), whole(vwt),
                  pl.BlockSpec((1, tq, LANES), lambda i, j: (i, j, C_SMALL // LANES))],
        out_specs=pl.BlockSpec((1, tq, wq), lambda i, j: (i, j, 0)),
        scratch_shapes=[pltpu.VMEM((NSA_GROUPS, LANES, NSA_REP * tq), BF16),
                        pltpu.VMEM((NSA_GROUPS, n_sel, NSA_REP * tq), BF16),
                        pltpu.VMEM((NSA_GROUPS, HEAD_DIM, NSA_REP * tq), F32),
                        pltpu.VMEM((2 * NSA_GROUPS, 1, NSA_REP * tq), F32),
                        pltpu.VMEM((2 * NSA_GROUPS, 1, NSA_REP * tq), F32),
                        pltpu.VMEM((2 * NSA_GROUPS, HEAD_DIM, NSA_REP * tq), F32)],
        compiler_params=_cparams(("parallel", "parallel")),
        name="nsa_attend",
    )(proj3, cos2, sin2, kcmp, vcmp_t, ks, vst, kw, vwt, proj3)


def _fox_kernel(q_ref, k_ref, vt_ref, cq_ref, ck_ref, o_ref, qt_ref, m_ref, l_ref, acc_ref, *, tq):
    qi = pl.program_id(1)
    tk = tq
    _padded_qt(q_ref, None, None, qt_ref, tq, 1)
    m_ref[...] = jnp.full(m_ref.shape, NEG_INF, F32)
    l_ref[...] = jnp.zeros(l_ref.shape, F32)
    acc_ref[...] = jnp.zeros(acc_ref.shape, F32)
    cq = cq_ref[0]
    causal = lax.broadcasted_iota(jnp.int32, (tk, tq), 0) <= lax.broadcasted_iota(jnp.int32, (tk, tq), 1)

    def tile(j, diagonal):
        rows = pl.ds(pl.multiple_of(j * tk, tk), tk)
        ck = ck_ref[0, rows, :]
        for h in range(FOX_HEADS):
            hp = h // 2
            s = _dot(k_ref[0, rows, hp * LANES:(hp + 1) * LANES], qt_ref[h])
            s = (s - ck[:, FL_OFF + h:FL_OFF + h + 1]) + cq[FL_OFF + h:FL_OFF + h + 1, :]
            if diagonal:
                s = jnp.where(causal, s, NEG_INF)
            _online_update(s, vt_ref[0, j, h * HEAD_DIM:(h + 1) * HEAD_DIM, :], m_ref, l_ref, acc_ref, h)

    def body(j, c):
        tile(j, False)
        return c

    lax.fori_loop(0, qi, body, 0)
    tile(qi, True)
    rows = [acc_ref[h] / jnp.maximum(l_ref[h], 1e-30) for h in range(FOX_HEADS)]
    o_ref[0] = jnp.concatenate(rows, axis=0).T


def _fox_attend(proj3, kf, vft, cum, cum_t, tq):
    b, s, _ = proj3.shape
    w = FOX_HEADS * HEAD_DIM
    whole = lambda a: pl.BlockSpec((1,) + a.shape[1:], lambda i, j: (i,) + (0,) * (a.ndim - 1))
    return pl.pallas_call(
        functools.partial(_fox_kernel, tq=tq),
        out_shape=jax.ShapeDtypeStruct((b, s, w), F32),
        grid=(b, s // tq),
        in_specs=[pl.BlockSpec((1, tq, w), lambda i, j: (i, j, C_QF // w)),
                  whole(kf), whole(vft),
                  pl.BlockSpec((1, LANES, tq), lambda i, j: (i, 0, j)),
                  whole(cum)],
        out_specs=pl.BlockSpec((1, tq, w), lambda i, j: (i, j, 0)),
        scratch_shapes=[pltpu.VMEM((FOX_HEADS, LANES, tq), BF16),
                        pltpu.VMEM((FOX_HEADS, 1, tq), F32),
                        pltpu.VMEM((FOX_HEADS, 1, tq), F32),
                        pltpu.VMEM((FOX_HEADS, HEAD_DIM, tq), F32)],
        compiler_params=_cparams(("parallel", "parallel")),
        name="fox_attend",
    )(proj3, kf, vft, cum_t, cum)


def _memkv_kernel(m_ref, g_ref, w_ref, o_ref):
    x = m_ref[0]
    ms = jnp.mean(x * x, axis=-1, keepdims=True)
    hn = (x * lax.rsqrt(ms + RMS_EPS) * g_ref[...]).astype(BF16)
    o_ref[0] = _dot(hn, w_ref[...]).astype(BF16)


def _mem_kv(mem, g, w_bf16):
    b, m, d = mem.shape
    n = w_bf16.shape[1]
    return pl.pallas_call(
        _memkv_kernel,
        out_shape=jax.ShapeDtypeStruct((b, m, n), BF16),
        grid=(b,),
        in_specs=[pl.BlockSpec((1, m, d), lambda i: (i, 0, 0)),
                  pl.BlockSpec((1, d), lambda i: (0, 0)),
                  pl.BlockSpec((d, n), lambda i: (0, 0))],
        out_specs=pl.BlockSpec((1, m, n), lambda i: (i, 0, 0)),
        compiler_params=_cparams(("parallel",)),
        name="mem_kv",
    )(mem, g, w_bf16)


def _memattn_kernel(q_ref, kv_ref, o_ref):
    w = MEM_HEADS * MEM_HEAD_DIM
    for h in range(MEM_HEADS):
        lo, hi = h * MEM_HEAD_DIM, (h + 1) * MEM_HEAD_DIM
        s = _dot_nt(q_ref[0, :, lo:hi].astype(BF16), kv_ref[0, :, lo:hi]) * (MEM_HEAD_DIM ** -0.5)
        e = jnp.exp(s - jnp.max(s, axis=-1, keepdims=True))
        p = e / jnp.sum(e, axis=-1, keepdims=True)
        o_ref[0, :, lo:hi] = _dot(p.astype(BF16), kv_ref[0, :, w + lo:w + hi])


def _mem_attend(proj3, kv, tq=512):
    b, s, _ = proj3.shape
    m = kv.shape[1]
    w = MEM_HEADS * MEM_HEAD_DIM
    return pl.pallas_call(
        _memattn_kernel,
        out_shape=jax.ShapeDtypeStruct((b, s, w), F32),
        grid=(b, s // tq),
        in_specs=[pl.BlockSpec((1, tq, w), lambda i, j: (i, j, C_QM // w)),
                  pl.BlockSpec((1, m, 2 * w), lambda i, j: (i, 0, 0))],
        out_specs=pl.BlockSpec((1, tq, w), lambda i, j: (i, j, 0)),
        compiler_params=_cparams(("parallel", "parallel")),
        name="mem_attend",
    )(proj3, kv)


def _merge_kernel(on_ref, of_ref, om_ref, ln_ref, lf_ref, lm_ref, bm_ref, wn_ref, wf_ref, wm_ref,
                  wo_ref, x_ref, g_ref, wr_ref, br_ref, x1_ref, hn_ref, idx_ref, gate_ref):
    d = x_ref.shape[1]

    def branch(o_ref, l_ref, w_ref, k):
        gate = jax.nn.sigmoid(l_ref[...] + bm_ref[:, k * d:(k + 1) * d])
        return gate * _dot(o_ref[...].astype(BF16), w_ref[...])

    merged = branch(on_ref, ln_ref, wn_ref, 0) + branch(of_ref, lf_ref, wf_ref, 1) + branch(om_ref, lm_ref, wm_ref, 2)
    x1 = x_ref[...] + _dot(merged.astype(BF16), wo_ref[...])
    x1_ref[...] = x1
    ms = jnp.mean(x1 * x1, axis=-1, keepdims=True)
    hn = x1 * lax.rsqrt(ms + RMS_EPS) * g_ref[...]
    hn_ref[...] = hn
    h_hi = hn.astype(BF16)
    h_lo = (hn - h_hi.astype(F32)).astype(BF16)
    wr = wr_ref[...]
    w_hi = wr.astype(BF16)
    w_lo = (wr - w_hi.astype(F32)).astype(BF16)
    logits = (_dot(h_hi, w_hi) + (_dot(h_lo, w_hi) + _dot(h_hi, w_lo))) + br_ref[...]
    tm, ne = logits.shape
    lane_e = lax.broadcasted_iota(jnp.int32, (tm, ne), 1)
    lane = lax.broadcasted_iota(jnp.int32, (tm, LANES), 1)
    idx_slab = jnp.zeros((tm, LANES), jnp.int32)
    val_slab = jnp.zeros((tm, LANES), F32)
    work = logits
    vals = []
    for k in range(TOP_K):
        m = jnp.max(work, axis=-1, keepdims=True)
        idx = jnp.min(jnp.where(work == m, lane_e, ne), axis=-1, keepdims=True)
        work = jnp.where(lane_e == idx, -jnp.inf, work)
        idx_slab = jnp.where(lane == k, idx, idx_slab)
        vals.append(m)
    es = [jnp.exp(v - vals[0]) for v in vals]
    tot = es[0]
    for e in es[1:]:
        tot = tot + e
    for k in range(TOP_K):
        val_slab = jnp.where(lane == k, es[k] / tot, val_slab)
    idx_ref[...] = idx_slab
    gate_ref[...] = val_slab


def _merge_route(o_nsa, o_fox, o_mem, proj, b_merge, wn, wf, wm, wo, x2, g_ffn, w_router, b_router, tm=256):
    t, d = x2.shape
    wb = o_nsa.shape[1]
    row = lambda w: pl.BlockSpec((tm, w), lambda i: (i, 0))
    full = lambda a: pl.BlockSpec(a.shape, lambda i: (0,) * a.ndim)
    return pl.pallas_call(
        _merge_kernel,
        out_shape=(jax.ShapeDtypeStruct((t, d), F32), jax.ShapeDtypeStruct((t, d), F32),
                   jax.ShapeDtypeStruct((t, LANES), jnp.int32), jax.ShapeDtypeStruct((t, LANES), F32)),
        grid=(t // tm,),
        in_specs=[row(wb), row(wb), row(wb),
                  pl.BlockSpec((tm, d), lambda i: (i, 0)),
                  pl.BlockSpec((tm, d), lambda i: (i, 1)),
                  pl.BlockSpec((tm, d), lambda i: (i, 2)),
                  full(b_merge), full(wn), full(wf), full(wm), full(wo),
                  row(d), full(g_ffn), full(w_router), full(b_router)],
        out_specs=(row(d), row(d), row(LANES), row(LANES)),
        compiler_params=_cparams(("parallel",)),
        name="merge_route",
    )(o_nsa, o_fox, o_mem, proj, proj, proj, b_merge, wn, wf, wm, wo, x2, g_ffn, w_router, b_router)


def _rank_kernel(idx_ref, rank_ref, cnt_ref, carry_ref, *, tm):
    @pl.when(pl.program_id(0) == 0)
    def _():
        carry_ref[...] = jnp.zeros(carry_ref.shape, F32)

    idx = idx_ref[...]
    lane_e = lax.broadcasted_iota(jnp.int32, (tm, N_EXPERTS), 1)
    hots = [jnp.where(idx[:, k:k + 1] == lane_e, 1.0, 0.0) for k in range(TOP_K)]
    cnt = hots[0]
    for hk in hots[1:]:
        cnt = cnt + hk
    r = lax.broadcasted_iota(jnp.int32, (tm, tm), 0)
    c = lax.broadcasted_iota(jnp.int32, (tm, tm), 1)
    strict = jnp.where(c < r, 1.0, 0.0).astype(BF16)
    before = _dot(strict, cnt.astype(BF16)) + carry_ref[...]
    lane = lax.broadcasted_iota(jnp.int32, (tm, LANES), 1)
    slab = jnp.zeros((tm, LANES), F32)
    for k in range(TOP_K):
        slab = jnp.where(lane == k, jnp.sum(hots[k] * before, axis=-1, keepdims=True), slab)
    rank_ref[...] = slab.astype(jnp.int32)
    total = carry_ref[...] + jnp.sum(cnt, axis=0, keepdims=True)
    carry_ref[...] = total
    cnt_ref[...] = jnp.broadcast_to(total, cnt_ref.shape).astype(jnp.int32)


def _moe_rank(idx_slab, tm=256):
    t = idx_slab.shape[0]
    return pl.pallas_call(
        functools.partial(_rank_kernel, tm=tm),
        out_shape=(jax.ShapeDtypeStruct((t, LANES), jnp.int32),
                   jax.ShapeDtypeStruct((8, N_EXPERTS), jnp.int32)),
        grid=(t // tm,),
        in_specs=[pl.BlockSpec((tm, LANES), lambda i: (i, 0))],
        out_specs=(pl.BlockSpec((tm, LANES), lambda i: (i, 0)),
                   pl.BlockSpec((8, N_EXPERTS), lambda i: (0, 0))),
        scratch_shapes=[pltpu.VMEM((1, N_EXPERTS), F32)],
        compiler_params=_cparams(("arbitrary",)),
        name="moe_rank",
    )(idx_slab)


def _expert_kernel(be_ref, nu_ref, nx_ref, tok_ref, hn_ref, wgu_ref, wdn_ref, bg_ref, bu_ref, bd_ref, o_ref,
                   buf_ref, act_ref, sgu_ref, sdn_ref, wg_ref, wu_ref, wd_ref, sem, wsem, *, unroll, chunk):
    i = pl.program_id(0)
    n_used = nu_ref[0]
    e = be_ref[i]
    f = wg_ref.shape[1]
    d = wd_ref.shape[1]

    def row_copy(tok, r, s):
        return pltpu.make_async_copy(hn_ref.at[pl.ds(tok, 1)], buf_ref.at[s, pl.ds(r, 1)], sem.at[s])

    def weight_copies(ex):
        return (pltpu.make_async_copy(wgu_ref.at[ex], sgu_ref, wsem.at[0]),
                pltpu.make_async_copy(wdn_ref.at[ex], sdn_ref, wsem.at[1]))

    def wait_rows(s):
        def wait(r, c):
            row_copy(0, 0, s).wait()
            return c

        lax.fori_loop(0, MOE_ROWS, wait, 0, unroll=unroll)

    @pl.when((i == 0) & (n_used > 0))
    def _():
        for cp in weight_copies(e):
            cp.start()

        def body(r2, c):
            for j in range(2):
                row_copy(tok_ref[2 * r2 + j], 2 * r2 + j, 0).start(priority=j)
            return c

        lax.fori_loop(0, MOE_ROWS // 2, body, 0, unroll=unroll)

    first = (i == 0) | (e != be_ref[jnp.maximum(i - 1, 0)])

    @pl.when((i < n_used) & first)
    def _():
        for cp in weight_copies(e):
            cp.wait()
        half = chunk // 2
        pr = lax.broadcasted_iota(jnp.int32, (chunk, chunk), 0)
        pc = lax.broadcasted_iota(jnp.int32, (chunk, chunk), 1)
        perm = jnp.where(pr == jnp.where(pc < half, 2 * pc, 2 * (pc - half) + 1), 1.0, 0.0).astype(BF16)
        for c in range(2 * f // chunk):
            sep = _dot(sgu_ref[:, c * chunk:(c + 1) * chunk].astype(BF16), perm).astype(BF16)
            wg_ref[:, c * half:(c + 1) * half] = sep[:, :half]
            wu_ref[:, c * half:(c + 1) * half] = sep[:, half:]
        wd_ref[...] = sdn_ref[...].astype(BF16)

        @pl.when(nx_ref[i] >= 0)
        def _():
            for cp in weight_copies(nx_ref[i]):
                cp.start()

    @pl.when(i < n_used)
    def _():
        cur = i % 2
        nxt = 1 - cur
        base = jnp.minimum(i + 1, n_used - 1) * MOE_ROWS
        n_groups = f // chunk + d // chunk
        per = MOE_ROWS // n_groups

        def issue_group(k):
            for r in range(k * per, (k + 1) * per):
                row_copy(tok_ref[base + r], r, nxt).start(priority=r % 2)

        wait_rows(cur)
        x = buf_ref[cur].astype(BF16)
        for c in range(f // chunk):
            cols = slice(c * chunk, (c + 1) * chunk)
            g = jnp.minimum(_dot(x, wg_ref[:, cols]) + bg_ref[0, :, cols], SWIGLU_LIMIT)
            u = jnp.clip(_dot(x, wu_ref[:, cols]) + bu_ref[0, :, cols], -SWIGLU_LIMIT, SWIGLU_LIMIT)
            act_ref[:, cols] = ((u + 1.0) * g * jax.nn.sigmoid(SWIGLU_ALPHA * g)).astype(BF16)
            issue_group(c)
        act = act_ref[...]
        for c in range(d // chunk):
            cols = slice(c * chunk, (c + 1) * chunk)
            o_ref[:, cols] = _dot(act, wd_ref[:, cols]) + bd_ref[0, :, cols]
            issue_group(f // chunk + c)

        @pl.when(i == n_used - 1)
        def _():
            wait_rows(nxt)

    @pl.when(i >= n_used)
    def _():
        o_ref[...] = jnp.zeros(o_ref.shape, o_ref.dtype)


def _moe_experts(block_exp, n_used, next_exp, slot_tok, hn, w_gate_up, w_down, bg, bu, bd):
    n_slots = slot_tok.shape[0]
    d = hn.shape[1]
    f = w_down.shape[1]
    nblk = n_slots // MOE_ROWS
    bspec = lambda w: pl.BlockSpec((1, 1, w), lambda i, be, nu, nx, tk: (be[i], 0, 0))
    return pl.pallas_call(
        functools.partial(_expert_kernel, unroll=8, chunk=256),
        out_shape=jax.ShapeDtypeStruct((n_slots, d), F32),
        grid_spec=pltpu.PrefetchScalarGridSpec(
            num_scalar_prefetch=4,
            grid=(nblk,),
            in_specs=[pl.BlockSpec(memory_space=pl.ANY), pl.BlockSpec(memory_space=pl.ANY),
                      pl.BlockSpec(memory_space=pl.ANY), bspec(f), bspec(f), bspec(d)],
            out_specs=pl.BlockSpec((MOE_ROWS, d), lambda i, be, nu, nx, tk: (i, 0)),
            scratch_shapes=[pltpu.VMEM((2, MOE_ROWS, d), hn.dtype), pltpu.VMEM((MOE_ROWS, f), BF16),
                            pltpu.VMEM((d, 2 * f), F32), pltpu.VMEM((f, d), F32),
                            pltpu.VMEM((d, f), BF16), pltpu.VMEM((d, f), BF16), pltpu.VMEM((f, d), BF16),
                            pltpu.SemaphoreType.DMA((2,)), pltpu.SemaphoreType.DMA((2,))]),
        compiler_params=_cparams(("arbitrary",)),
        name="moe_experts",
    )(block_exp, n_used, next_exp, slot_tok, hn, w_gate_up, w_down, bg, bu, bd)


def _combine_kernel(dest_ref, gate_ref, x1_ref, g_ref, ys_ref, o_ref, buf_ref, sem, *, tm, unroll):
    i = pl.program_id(0)
    n = pl.num_programs(0)

    def row_copy(slot_row, k, r, s):
        return pltpu.make_async_copy(ys_ref.at[pl.ds(slot_row, 1)], buf_ref.at[s, k, pl.ds(r, 1)], sem.at[s])

    def issue(step, s):
        base = step * (tm * TOP_K)

        def body(r, c):
            for k in range(TOP_K):
                row_copy(dest_ref[base + r * TOP_K + k], k, r, s).start(priority=k % 2)
            return c

        lax.fori_loop(0, tm, body, 0, unroll=unroll)

    @pl.when(i == 0)
    def _():
        issue(0, 0)

    @pl.when(i + 1 < n)
    def _():
        issue(i + 1, (i + 1) % 2)

    cur = i % 2

    def wait(r, c):
        row_copy(0, 0, 0, cur).wait()
        return c

    lax.fori_loop(0, tm * TOP_K, wait, 0, unroll=unroll)

    gate = gate_ref[...]
    y = x1_ref[...]
    for k in range(TOP_K):
        y = y + gate[:, k:k + 1] * buf_ref[cur, k]
    ms = jnp.mean(y * y, axis=-1, keepdims=True)
    o_ref[...] = y * lax.rsqrt(ms + RMS_EPS) * g_ref[...]


def _moe_combine(dest_flat, gate_slab, x1, g_final, ys, tm=128):
    t, d = x1.shape
    return pl.pallas_call(
        functools.partial(_combine_kernel, tm=tm, unroll=8),
        out_shape=jax.ShapeDtypeStruct((t, d), F32),
        grid_spec=pltpu.PrefetchScalarGridSpec(
            num_scalar_prefetch=1,
            grid=(t // tm,),
            in_specs=[pl.BlockSpec((tm, LANES), lambda i, dst: (i, 0)),
                      pl.BlockSpec((tm, d), lambda i, dst: (i, 0)),
                      pl.BlockSpec((1, d), lambda i, dst: (0, 0)),
                      pl.BlockSpec(memory_space=pl.ANY)],
            out_specs=pl.BlockSpec((tm, d), lambda i, dst: (i, 0)),
            scratch_shapes=[pltpu.VMEM((2, TOP_K, tm, d), F32), pltpu.SemaphoreType.DMA((2,))]),
        compiler_params=_cparams(("arbitrary",)),
        name="moe_combine",
    )(dest_flat, gate_slab, x1, g_final, ys)


def _rope_tables(s):
    inv = ROPE_THETA ** (-jnp.arange(0, HEAD_DIM, 2, dtype=F32) / HEAD_DIM)
    ang = jnp.arange(s, dtype=F32)[:, None] * inv[None, :]
    cos, sin = jnp.cos(ang), jnp.sin(ang)
    return jnp.concatenate([cos, cos], axis=-1), jnp.concatenate([-sin, sin], axis=-1)


def _permute_w_in(w):
    d = w.shape[0]
    pad = N_PROJ - (C_SMALL + 32)
    cols = [w[:, 3360:6432], w[:, 0:512], w[:, 1304:1816], w[:, 1816:2328], w[:, 2328:2840],
            w[:, 2848:3360], w[:, 512:1280], w[:, 1280:1304], w[:, 2840:2848], jnp.zeros((d, pad), w.dtype)]
    return jnp.concatenate(cols, axis=1).astype(BF16)


def _layer(x, mem, g_mix, w_in, b_forget, b_merge, pe_k, w1_k, w2_k, pe_v, w1_v, w2_v, g_mem, w_mem_kv,
           w_br_nsa, w_br_fox, w_br_mem, w_out, g_ffn, w_router, b_router, w_gate_up, b_gate_up,
           w_down, b_down, g_final):
    b, s, d = x.shape
    t = b * s
    x2 = x.reshape(t, d)
    cos64, sin64 = _rope_tables(s)
    cos2 = jnp.concatenate([cos64, cos64], axis=-1)
    sin2 = jnp.concatenate([sin64, sin64], axis=-1)

    proj = _inproj(x2, g_mix.reshape(1, d), _permute_w_in(w_in), tm=min(1024, t))
    proj3 = proj.reshape(b, s, N_PROJ)
    b_row = jnp.zeros((1, LANES), F32).at[0, FL_OFF:FL_OFF + FOX_HEADS].set(b_forget)
    cum, cum_t = _fox_cum(proj3, b_row)

    nb = s // CMP_STRIDE
    wide = CMP_STRIDE * HEAD_DIM

    def to_blocks(c0):
        a = proj3[:, :, c0:c0 + NSA_GROUPS * HEAD_DIM].reshape(b, nb, CMP_STRIDE, NSA_GROUPS, HEAD_DIM)
        return a.transpose(0, 3, 1, 2, 4).reshape(b, NSA_GROUPS, nb, wide)

    kcmp, vcmp_t = _nsa_compress(
        to_blocks(C_NSAKV), to_blocks(C_NSAKV + LANES), cos64.reshape(nb, wide), sin64.reshape(nb, wide),
        pe_k.reshape(2, wide), pe_v.reshape(2, wide),
        w1_k.reshape(CMP_LEN * HEAD_DIM, HEAD_DIM).astype(BF16), w1_v.reshape(CMP_LEN * HEAD_DIM, HEAD_DIM).astype(BF16),
        w2_k.astype(BF16), w2_v.astype(BF16))
    fox_tile = min(FOX_TILE, s)
    kf, vft, ks, vst, kw, vwt = _attn_prep(proj3, cos2, sin2, fox_tile, NSA_TILE)
    o_nsa = _nsa_attend(proj3, cos2, sin2, kcmp, vcmp_t, ks, vst, kw, vwt, NSA_TILE)
    o_fox = _fox_attend(proj3, kf, vft, cum, cum_t, fox_tile)
    mem_kv = _mem_kv(mem, g_mem.reshape(1, d), w_mem_kv.astype(BF16))
    o_mem = _mem_attend(proj3, mem_kv)

    x1, hn, idx_slab, gate_slab = _merge_route(
        o_nsa.reshape(t, -1), o_fox.reshape(t, -1), o_mem.reshape(t, -1), proj, b_merge.reshape(1, -1),
        w_br_nsa.astype(BF16), w_br_fox.astype(BF16), w_br_mem.astype(BF16), w_out.astype(BF16),
        x2, g_ffn.reshape(1, d), w_router, b_router.reshape(1, -1))

    rank_slab, cnt8 = _moe_rank(idx_slab)
    counts = cnt8[0]
    padded = (counts + MOE_ROWS - 1) // MOE_ROWS * MOE_ROWS
    pad_end = jnp.cumsum(padded)
    pad_start = pad_end - padded
    n_assign = t * TOP_K
    nblk = -(-(n_assign + N_EXPERTS * (MOE_ROWS - 1)) // MOE_ROWS)
    n_slots = nblk * MOE_ROWS
    top_idx = idx_slab[:, :TOP_K]
    dest = (pad_start[top_idx] + rank_slab[:, :TOP_K]).reshape(-1).astype(jnp.int32)
    tok_flat = jnp.repeat(jnp.arange(t, dtype=jnp.int32), TOP_K)
    slot_tok = jnp.zeros((n_slots,), jnp.int32).at[dest].set(tok_flat, unique_indices=True)
    blk_start = jnp.arange(nblk, dtype=jnp.int32) * MOE_ROWS
    block_exp = jnp.minimum(jnp.sum(blk_start[:, None] >= pad_end[None, :], axis=1),
                            N_EXPERTS - 1).astype(jnp.int32)
    n_used = (pad_end[-1] // MOE_ROWS).astype(jnp.int32).reshape(1)

    e_ids = jnp.arange(N_EXPERTS, dtype=jnp.int32)
    later = (e_ids[None, :] > block_exp[:, None]) & (counts[None, :] > 0)
    next_exp = jnp.min(jnp.where(later, e_ids[None, :], N_EXPERTS), axis=1)
    next_exp = jnp.where(next_exp < N_EXPERTS, next_exp, -1).astype(jnp.int32)

    f = w_down.shape[1]
    bgu = b_gate_up.reshape(N_EXPERTS, 1, f, 2)
    ys = _moe_experts(block_exp, n_used, next_exp, slot_tok, hn, w_gate_up, w_down,
                      bgu[..., 0], bgu[..., 1], b_down.reshape(N_EXPERTS, 1, d))
    out = _moe_combine(dest, gate_slab, x1, g_final.reshape(1, d), ys)
    return out.reshape(b, s, d)


def kernel(x, mem, g_mix, w_in, b_forget, b_merge, nsa_pe_k, nsa_w1_k, nsa_w2_k, nsa_pe_v, nsa_w1_v, nsa_w2_v, g_mem, w_mem_kv, w_branch_nsa, w_branch_fox, w_branch_mem, w_out, g_ffn, w_router, b_router, w_gate_up, b_gate_up, w_down, b_down, g_final):
    assert g_mix.shape[0] == 1, "single-layer trunk"
    return _layer(x, mem, g_mix[0], w_in[0], b_forget[0], b_merge[0], nsa_pe_k[0], nsa_w1_k[0], nsa_w2_k[0],
                  nsa_pe_v[0], nsa_w1_v[0], nsa_w2_v[0], g_mem[0], w_mem_kv[0], w_branch_nsa[0],
                  w_branch_fox[0], w_branch_mem[0], w_out[0], g_ffn[0], w_router[0], b_router[0],
                  w_gate_up[0], b_gate_up[0], w_down[0], b_down[0], g_final)
```

```python
import functools

import jax
import jax.numpy as jnp
import numpy as np
from jax import lax
from jax.experimental import pallas as pl
from jax.experimental.pallas import tpu as pltpu

F32 = jnp.float32
BF16 = jnp.bfloat16

D_MODEL = 1024
HEAD_DIM = 64
ROPE_THETA = 10000.0
RMS_EPS = 1e-5
NEG_INF = -1e30
FORCE_SCORE = 1e4
NSA_HEADS = 8
NSA_GROUPS = 2
NSA_REP = NSA_HEADS // NSA_GROUPS
CMP_LEN = 32
CMP_STRIDE = 16
SEL_BLOCK = 64
SEL_TOPK = 16
WINDOW = 512
FOX_HEADS = 8
MEM_HEADS = 4
MEM_HEAD_DIM = 128
N_EXPERTS = 32
TOP_K = 4
SWIGLU_LIMIT = 7.0
SWIGLU_ALPHA = 1.702

LANES = 128
VMEM_LIMIT = 48 * 1024 * 1024

C_MERGE = 0
C_QA = 3072
C_QF = 3584
C_KF = 4096
C_VF = 4608
C_QM = 5120
C_NSAKV = 5632
C_SMALL = 6400
N_PROJ = 6656
GL_OFF = 0
FL_OFF = 24

MOE_ROWS = 256
NSA_TILE = 256
FOX_TILE = 512


def _cparams(sem, vmem=VMEM_LIMIT):
    return pltpu.CompilerParams(dimension_semantics=sem, vmem_limit_bytes=vmem)


def _dot(a, b):
    return jnp.dot(a, b, preferred_element_type=F32)


def _dot_nt(a, b):
    return lax.dot_general(a, b, (((1,), (1,)), ((), ())), preferred_element_type=F32)


def _rope(x, cos, sin_signed):
    w = x.shape[-1]
    lane = lax.broadcasted_iota(jnp.int32, x.shape, x.ndim - 1)
    first = (lane & (HEAD_DIM - 1)) < (HEAD_DIM // 2)
    rot = jnp.where(first, pltpu.roll(x, w - HEAD_DIM // 2, x.ndim - 1),
                    pltpu.roll(x, HEAD_DIM // 2, x.ndim - 1))
    return x * cos + rot * sin_signed


def _split3(x):
    hi = x.astype(BF16)
    r1 = x - hi.astype(F32)
    mid = r1.astype(BF16)
    lo = (r1 - mid.astype(F32)).astype(BF16)
    return hi, mid, lo


def _inproj_kernel(x_ref, g_ref, w_ref, o_ref, hn_ref):
    @pl.when(pl.program_id(1) == 0)
    def _():
        x = x_ref[...]
        ms = jnp.mean(x * x, axis=-1, keepdims=True)
        hn_ref[...] = (x * lax.rsqrt(ms + RMS_EPS) * g_ref[...]).astype(BF16)

    o_ref[...] = _dot(hn_ref[...], w_ref[...])


def _inproj(x2, g, w_bf16, tm=1024, tn=512):
    t, d = x2.shape
    n = w_bf16.shape[1]
    return pl.pallas_call(
        _inproj_kernel,
        out_shape=jax.ShapeDtypeStruct((t, n), F32),
        grid=(t // tm, n // tn),
        in_specs=[pl.BlockSpec((tm, d), lambda i, j: (i, 0)),
                  pl.BlockSpec((1, d), lambda i, j: (0, 0)),
                  pl.BlockSpec((d, tn), lambda i, j: (0, j))],
        out_specs=pl.BlockSpec((tm, tn), lambda i, j: (i, j)),
        scratch_shapes=[pltpu.VMEM((tm, d), BF16)],
        compiler_params=_cparams(("parallel", "arbitrary")),
        name="inproj",
    )(x2, g, w_bf16)


def _cum_kernel(s_ref, b_ref, c_ref, ct_ref, cx_ref, *, blk):
    s = s_ref.shape[1]
    z = s_ref[0] + b_ref[...]
    logf = jnp.minimum(z, 0.0) - jnp.log1p(jnp.exp(-jnp.abs(z)))
    r = lax.broadcasted_iota(jnp.int32, (blk, blk), 0)
    c = lax.broadcasted_iota(jnp.int32, (blk, blk), 1)
    tri = jnp.where(c <= r, 1.0, 0.0).astype(BF16)
    pr = lax.broadcasted_iota(jnp.int32, (LANES, LANES), 0) - FL_OFF
    pc = lax.broadcasted_iota(jnp.int32, (LANES, LANES), 1)
    head_row = (pr >= 0) & (pr < FOX_HEADS)
    place = [jnp.where(head_row & (pc == 3 * pr + part), -1.0, 0.0).astype(BF16) for part in range(3)]
    lane = lax.broadcasted_iota(jnp.int32, (blk, LANES), 1)
    ones = jnp.where((lane >= 3 * FOX_HEADS) & (lane < 3 * FOX_HEADS + 3), 1.0, 0.0)
    carry = jnp.zeros((1, LANES), F32)
    for i in range(s // blk):
        hi, mid, lo = _split3(logf[i * blk:(i + 1) * blk])
        loc = (_dot(tri, hi) + _dot(tri, mid)) + _dot(tri, lo)
        out = loc + carry
        c_ref[0, i * blk:(i + 1) * blk, :] = out
        carry = out[blk - 1:blk, :]
        hi, mid, lo = _split3(out)
        cx = (_dot(hi, place[0]) + _dot(mid, place[1])) + _dot(lo, place[2])
        cx_ref[0, i * blk:(i + 1) * blk, :] = (cx + ones).astype(BF16)
    ct_ref[0] = c_ref[0].T


def _fox_cum(proj3, b_row, blk=256):
    b, s, _ = proj3.shape
    return pl.pallas_call(
        functools.partial(_cum_kernel, blk=blk),
        out_shape=(jax.ShapeDtypeStruct((b, s, LANES), F32),
                   jax.ShapeDtypeStruct((b, LANES, s), F32),
                   jax.ShapeDtypeStruct((b, s, LANES), BF16)),
        grid=(b,),
        in_specs=[pl.BlockSpec((1, s, LANES), lambda i: (i, 0, C_SMALL // LANES)),
                  pl.BlockSpec((1, LANES), lambda i: (0, 0))],
        out_specs=(pl.BlockSpec((1, s, LANES), lambda i: (i, 0, 0)),
                   pl.BlockSpec((1, LANES, s), lambda i: (i, 0, 0)),
                   pl.BlockSpec((1, s, LANES), lambda i: (i, 0, 0))),
        compiler_params=_cparams(("parallel",)),
        name="fox_cum",
    )(proj3, b_row)


def _cmp_kernel(ak_ref, av_ref, cos_ref, sin_ref, pek_ref, pev_ref, w1k_ref, w1v_ref,
                w2k_ref, w2v_ref, kc_ref, vct_ref, vbuf_ref):
    nb = ak_ref.shape[2]
    half = ak_ref.shape[3]

    def mlp(a, pe_ref, w1_ref, w2_ref):
        pa = _dot((a + pe_ref[0:1, :]).astype(BF16), w1_ref[0:half, :])
        pb = _dot((a + pe_ref[1:2, :]).astype(BF16), w1_ref[half:2 * half, :])
        z = pa + pltpu.roll(pb, nb - 1, 0)
        h = z * jax.nn.sigmoid(z)
        return _dot(h.astype(BF16), w2_ref[...])

    for g in range(NSA_GROUPS):
        lo, hi = g * HEAD_DIM, (g + 1) * HEAD_DIM
        kc_ref[0, :, lo:hi] = mlp(_rope(ak_ref[0, g], cos_ref[...], sin_ref[...]),
                                  pek_ref, w1k_ref, w2k_ref).astype(BF16)
        vbuf_ref[:, lo:hi] = mlp(av_ref[0, g], pev_ref, w1v_ref, w2v_ref)
    vct_ref[0] = vbuf_ref[...].T.astype(BF16)


def _nsa_compress(ak, av, cos_a, sin_a, pek, pev, w1k, w1v, w2k, w2v):
    b, g, nb, wide = ak.shape
    blk4 = pl.BlockSpec((1, g, nb, wide), lambda i: (i, 0, 0, 0))
    full = lambda shp: pl.BlockSpec(shp, lambda i: (0,) * len(shp))
    return pl.pallas_call(
        _cmp_kernel,
        out_shape=(jax.ShapeDtypeStruct((b, nb, g * HEAD_DIM), BF16),
                   jax.ShapeDtypeStruct((b, g * HEAD_DIM, nb), BF16)),
        grid=(b,),
        in_specs=[blk4, blk4, full(cos_a.shape), full(sin_a.shape), full(pek.shape), full(pev.shape),
                  full(w1k.shape), full(w1v.shape), full(w2k.shape), full(w2v.shape)],
        out_specs=(pl.BlockSpec((1, nb, g * HEAD_DIM), lambda i: (i, 0, 0)),
                   pl.BlockSpec((1, g * HEAD_DIM, nb), lambda i: (i, 0, 0))),
        scratch_shapes=[pltpu.VMEM((nb, g * HEAD_DIM), F32)],
        compiler_params=_cparams(("parallel",)),
        name="nsa_compress",
    )(ak, av, cos_a, sin_a, pek, pev, w1k, w1v, w2k, w2v)


def _prep_kernel(kf_ref, vf_ref, ks_ref, vs_ref, kw_ref, vw_ref, cos_ref, sin_ref,
                 okf_ref, ovf_ref, oks_ref, ovs_ref, okw_ref, ovw_ref):
    okf_ref[0] = kf_ref[0].astype(BF16)
    ovf_ref[0, 0] = vf_ref[0].T.astype(BF16)
    oks_ref[0] = _rope(ks_ref[0], cos_ref[...], sin_ref[...]).astype(BF16)
    okw_ref[0] = _rope(kw_ref[0], cos_ref[...], sin_ref[...]).astype(BF16)
    tn = ovs_ref.shape[3]
    for i in range(ovs_ref.shape[1]):
        ovs_ref[0, i] = vs_ref[0, i * tn:(i + 1) * tn, :].T.astype(BF16)
        ovw_ref[0, i] = vw_ref[0, i * tn:(i + 1) * tn, :].T.astype(BF16)


def _attn_prep(proj3, cos2, sin2, tk, tn):
    b, s, _ = proj3.shape
    wf = FOX_HEADS * HEAD_DIM
    kvb = C_NSAKV // LANES
    nk = s // tk
    sub = tk // tn
    col = lambda w, c: pl.BlockSpec((1, tk, w), lambda i, j: (i, j, c))
    rows = lambda w: pl.BlockSpec((1, tk, w), lambda i, j: (i, j, 0))
    ntile = pl.BlockSpec((1, sub, LANES, tn), lambda i, j: (i, j, 0, 0))
    return pl.pallas_call(
        _prep_kernel,
        out_shape=(jax.ShapeDtypeStruct((b, s, wf), BF16), jax.ShapeDtypeStruct((b, nk, wf, tk), BF16),
                   jax.ShapeDtypeStruct((b, s, LANES), BF16), jax.ShapeDtypeStruct((b, nk * sub, LANES, tn), BF16),
                   jax.ShapeDtypeStruct((b, s, LANES), BF16), jax.ShapeDtypeStruct((b, nk * sub, LANES, tn), BF16)),
        grid=(b, nk),
        in_specs=[col(wf, C_KF // wf), col(wf, C_VF // wf), col(LANES, kvb + 2), col(LANES, kvb + 3),
                  col(LANES, kvb + 4), col(LANES, kvb + 5),
                  pl.BlockSpec((tk, LANES), lambda i, j: (j, 0)), pl.BlockSpec((tk, LANES), lambda i, j: (j, 0))],
        out_specs=(rows(wf), pl.BlockSpec((1, 1, wf, tk), lambda i, j: (i, j, 0, 0)),
                   rows(LANES), ntile, rows(LANES), ntile),
        compiler_params=_cparams(("parallel", "parallel")),
        name="attn_prep",
    )(proj3, proj3, proj3, proj3, proj3, proj3, cos2, sin2)


def _online_update(s, vt, m_ref, l_ref, acc_ref, idx, mask=None):
    m_old = m_ref[idx]
    m_new = jnp.maximum(m_old, jnp.max(s, axis=0, keepdims=True))
    alpha = jnp.exp(m_old - m_new)
    p = jnp.exp(s - m_new)
    if mask is not None:
        p = jnp.where(mask, p, 0.0)
    l_ref[idx] = alpha * l_ref[idx] + jnp.sum(p, axis=0, keepdims=True)
    acc_ref[idx] = alpha * acc_ref[idx] + _dot(vt, p.astype(BF16))
    m_ref[idx] = m_new


def _padded_qt(q_ref, cos, sin, qt_ref, tq, heads_per_slot):
    qt_ref[...] = jnp.zeros(qt_ref.shape, qt_ref.dtype)
    n_heads = q_ref.shape[2] // HEAD_DIM
    for hp in range(n_heads // 2):
        q2 = q_ref[0, :, hp * LANES:(hp + 1) * LANES]
        if cos is not None:
            q2 = _rope(q2, cos, sin)
        qt = (q2 * (HEAD_DIM ** -0.5)).T.astype(BF16)
        for sub in range(2):
            h = 2 * hp + sub
            slot, r = h // heads_per_slot, h % heads_per_slot
            half = (slot % 2) if heads_per_slot > 1 else sub
            qt_ref[slot, half * HEAD_DIM:(half + 1) * HEAD_DIM, r * tq:(r + 1) * tq] = (
                qt[sub * HEAD_DIM:(sub + 1) * HEAD_DIM, :])


def _nsa_kernel(q_ref, cos_ref, sin_ref, kc_ref, vct_ref, ks_ref, vst_ref, kw_ref, vwt_ref, sm_ref, o_ref,
                qt_ref, oct_ref, m_ref, l_ref, acc_ref, *, tq, n_sel):
    qi = pl.program_id(1)
    tk = tq
    G, R = NSA_GROUPS, NSA_REP
    nb = kc_ref.shape[1]
    n_prev = WINDOW // tk
    _padded_qt(q_ref, cos_ref[...], sin_ref[...], qt_ref, tq, R)
    m_ref[...] = jnp.full(m_ref.shape, NEG_INF, F32)
    l_ref[...] = jnp.zeros(l_ref.shape, F32)
    acc_ref[...] = jnp.zeros(acc_ref.shape, F32)

    tpos = qi * tq + (lax.broadcasted_iota(jnp.int32, (nb, R * tq), 1) & (tq - 1))
    nrow = lax.broadcasted_iota(jnp.int32, (nb, R * tq), 0)
    cmask = (nrow * CMP_STRIDE + (CMP_LEN - 1) <= tpos) & (nrow < nb - 1)
    oj = lax.broadcasted_iota(jnp.int32, (n_sel, nb), 0)
    on = lax.broadcasted_iota(jnp.int32, (n_sel, nb), 1) * CMP_STRIDE
    overlap_t = jnp.where((on < (oj + 1) * SEL_BLOCK) & (on + CMP_LEN > oj * SEL_BLOCK), 1.0, 0.0).astype(BF16)
    blk = lax.broadcasted_iota(jnp.int32, (n_sel, tq), 0)
    trow = qi * tq + lax.broadcasted_iota(jnp.int32, (n_sel, tq), 1)
    cur = trow // SEL_BLOCK
    forced = (blk == 0) | (blk == cur) | (blk == cur - 1)
    future = blk * SEL_BLOCK > trow
    kc = kc_ref[0]
    for g in range(G):
        s = jnp.where(cmask, _dot(kc, qt_ref[g, 0:LANES, :]), NEG_INF)
        mx = jnp.max(s, axis=0, keepdims=True)
        e = jnp.where(cmask, jnp.exp(s - mx), 0.0)
        p = e / jnp.maximum(jnp.sum(e, axis=0, keepdims=True), 1e-30)
        oct_ref[g] = _dot(vct_ref[0, g * HEAD_DIM:(g + 1) * HEAD_DIM, :], p.astype(BF16))
        psum = p[:, 0:tq]
        for r in range(1, R):
            psum = psum + p[:, r * tq:(r + 1) * tq]
        hi, mid, lo = _split3(psum)
        imp = (_dot(overlap_t, hi) + _dot(overlap_t, mid)) + _dot(overlap_t, lo)
        imp = jnp.where(forced, FORCE_SCORE, jnp.where(future, -FORCE_SCORE, imp))
        rank = jnp.zeros((n_sel, tq), F32)
        for i in range(n_sel):
            ri = imp[i:i + 1, :]
            ahead = (ri > imp) | ((ri == imp) & (blk > i))
            rank = rank + jnp.where(ahead, 1.0, 0.0)
        bias = jnp.where(rank < float(min(SEL_TOPK, n_sel)), 0.0, NEG_INF).astype(BF16)
        for r in range(R):
            qt_ref[g, LANES:LANES + n_sel, r * tq:(r + 1) * tq] = bias

    qpos = qi * tq + (lax.broadcasted_iota(jnp.int32, (tk, R * tq), 1) & (tq - 1))
    krow = lax.broadcasted_iota(jnp.int32, (tk, R * tq), 0)
    er = lax.broadcasted_iota(jnp.int32, (tk, LANES), 0)
    ec = lax.broadcasted_iota(jnp.int32, (tk, LANES), 1)

    def sel_tile(j, diagonal):
        k = ks_ref[0, pl.ds(pl.multiple_of(j * tk, tk), tk), :]
        onehot = jnp.where((j * tk + er) // SEL_BLOCK == ec, 1.0, 0.0).astype(BF16)
        kx = jnp.concatenate([k, onehot], axis=1)
        for g in range(G):
            s = _dot(kx, qt_ref[g])
            if diagonal:
                s = jnp.where(j * tk + krow <= qpos, s, NEG_INF)
            _online_update(s, vst_ref[0, j, g * HEAD_DIM:(g + 1) * HEAD_DIM, :], m_ref, l_ref, acc_ref, g)

    def sel_body(j, c):
        sel_tile(j, False)
        return c

    lax.fori_loop(0, qi, sel_body, 0)
    sel_tile(qi, True)

    for back in range(n_prev, -1, -1):
        @pl.when(qi - back >= 0)
        def _(back=back):
            j = qi - back
            k = kw_ref[0, pl.ds(pl.multiple_of(j * tk, tk), tk), :]
            diff = qpos - (j * tk + krow)
            if back == 0:
                mask = diff >= 0
            elif back == n_prev:
                mask = diff < WINDOW
            else:
                mask = None
            for g in range(G):
                s = _dot(k, qt_ref[g, 0:LANES, :])
                if mask is not None:
                    s = jnp.where(mask, s, NEG_INF)
                _online_update(s, vwt_ref[0, j, g * HEAD_DIM:(g + 1) * HEAD_DIM, :], m_ref, l_ref, acc_ref,
                               G + g, mask if back == n_prev else None)

    gates_t = jax.nn.sigmoid(sm_ref[0]).T
    rows = []
    for h in range(NSA_HEADS):
        g, r = h // R, h % R
        cols = slice(r * tq, (r + 1) * tq)
        gate = lambda br: gates_t[GL_OFF + 3 * h + br:GL_OFF + 3 * h + br + 1, :]
        o_sel = acc_ref[g, :, cols] / jnp.maximum(l_ref[g, :, cols], 1e-30)
        o_win = acc_ref[G + g, :, cols] / jnp.maximum(l_ref[G + g, :, cols], 1e-30)
        rows.append(gate(0) * oct_ref[g, :, cols] + gate(1) * o_sel + gate(2) * o_win)
    o_ref[0] = jnp.concatenate(rows, axis=0).T


def _nsa_attend(proj3, cos2, sin2, kcmp, vcmp_t, ks, vst, kw, vwt, tq):
    b, s, _ = proj3.shape
    n_sel = s // SEL_BLOCK
    wq = NSA_HEADS * HEAD_DIM
    whole = lambda a: pl.BlockSpec((1,) + a.shape[1:], lambda i, j: (i,) + (0,) * (a.ndim - 1))
    return pl.pallas_call(
        functools.partial(_nsa_kernel, tq=tq, n_sel=n_sel),
        out_shape=jax.ShapeDtypeStruct((b, s, wq), F32),
        grid=(b, s // tq),
        in_specs=[pl.BlockSpec((1, tq, wq), lambda i, j: (i, j, C_QA // wq)),
                  pl.BlockSpec((tq, LANES), lambda i, j: (j, 0)),
                  pl.BlockSpec((tq, LANES), lambda i, j: (j, 0)),
                  whole(kcmp), whole(vcmp_t), whole(ks), whole(vst), whole(kw), whole(vwt),
                  pl.BlockSpec((1, tq, LANES), lambda i, j: (i, j, C_SMALL // LANES))],
        out_specs=pl.BlockSpec((1, tq, wq), lambda i, j: (i, j, 0)),
        scratch_shapes=[pltpu.VMEM((NSA_GROUPS, 2 * LANES, NSA_REP * tq), BF16),
                        pltpu.VMEM((NSA_GROUPS, HEAD_DIM, NSA_REP * tq), F32),
                        pltpu.VMEM((2 * NSA_GROUPS, 1, NSA_REP * tq), F32),
                        pltpu.VMEM((2 * NSA_GROUPS, 1, NSA_REP * tq), F32),
                        pltpu.VMEM((2 * NSA_GROUPS, HEAD_DIM, NSA_REP * tq), F32)],
        compiler_params=_cparams(("parallel", "parallel")),
        name="nsa_attend",
    )(proj3, cos2, sin2, kcmp, vcmp_t, ks, vst, kw, vwt, proj3)


def _fox_kernel(q_ref, k_ref, vt_ref, cq_ref, ck_ref, o_ref, qt_ref, m_ref, l_ref, acc_ref, *, tq):
    qi = pl.program_id(1)
    tk = tq
    _padded_qt(q_ref, None, None, qt_ref, tq, 1)
    m_ref[...] = jnp.full(m_ref.shape, NEG_INF, F32)
    l_ref[...] = jnp.zeros(l_ref.shape, F32)
    acc_ref[...] = jnp.zeros(acc_ref.shape, F32)
    cq = cq_ref[0]
    row = lax.broadcasted_iota(jnp.int32, (LANES, tq), 0)
    for h in range(FOX_HEADS):
        hi, mid, lo = _split3(cq[FL_OFF + h:FL_OFF + h + 1, :])
        ext = jnp.where((row >= 3 * h) & (row < 3 * h + 3), 1.0, 0.0)
        for part, val in enumerate((hi, mid, lo)):
            ext = jnp.where(row == 3 * FOX_HEADS + part, val.astype(F32), ext)
        qt_ref[h, LANES:2 * LANES, :] = ext.astype(BF16)
    causal = lax.broadcasted_iota(jnp.int32, (tk, tq), 0) <= lax.broadcasted_iota(jnp.int32, (tk, tq), 1)

    def tile(j, diagonal):
        rows = pl.ds(pl.multiple_of(j * tk, tk), tk)
        ckx = ck_ref[0, rows, :]
        for h in range(FOX_HEADS):
            hp = h // 2
            kx = jnp.concatenate([k_ref[0, rows, hp * LANES:(hp + 1) * LANES], ckx], axis=1)
            s = _dot(kx, qt_ref[h])
            if diagonal:
                s = jnp.where(causal, s, NEG_INF)
            _online_update(s, vt_ref[0, j, h * HEAD_DIM:(h + 1) * HEAD_DIM, :], m_ref, l_ref, acc_ref, h)

    def body(j, c):
        tile(j, False)
        return c

    lax.fori_loop(0, qi, body, 0)
    tile(qi, True)
    rows = [acc_ref[h] / jnp.maximum(l_ref[h], 1e-30) for h in range(FOX_HEADS)]
    o_ref[0] = jnp.concatenate(rows, axis=0).T


def _fox_attend(proj3, kf, vft, cum_x, cum_t, tq):
    b, s, _ = proj3.shape
    w = FOX_HEADS * HEAD_DIM
    whole = lambda a: pl.BlockSpec((1,) + a.shape[1:], lambda i, j: (i,) + (0,) * (a.ndim - 1))
    return pl.pallas_call(
        functools.partial(_fox_kernel, tq=tq),
        out_shape=jax.ShapeDtypeStruct((b, s, w), F32),
        grid=(b, s // tq),
        in_specs=[pl.BlockSpec((1, tq, w), lambda i, j: (i, j, C_QF // w)),
                  whole(kf), whole(vft),
                  pl.BlockSpec((1, LANES, tq), lambda i, j: (i, 0, j)),
                  whole(cum_x)],
        out_specs=pl.BlockSpec((1, tq, w), lambda i, j: (i, j, 0)),
        scratch_shapes=[pltpu.VMEM((FOX_HEADS, 2 * LANES, tq), BF16),
                        pltpu.VMEM((FOX_HEADS, 1, tq), F32),
                        pltpu.VMEM((FOX_HEADS, 1, tq), F32),
                        pltpu.VMEM((FOX_HEADS, HEAD_DIM, tq), F32)],
        compiler_params=_cparams(("parallel", "parallel")),
        name="fox_attend",
    )(proj3, kf, vft, cum_t, cum_x)


def _memkv_kernel(m_ref, g_ref, w_ref, o_ref):
    x = m_ref[0]
    ms = jnp.mean(x * x, axis=-1, keepdims=True)
    hn = (x * lax.rsqrt(ms + RMS_EPS) * g_ref[...]).astype(BF16)
    o_ref[0] = _dot(hn, w_ref[...]).astype(BF16)


def _mem_kv(mem, g, w_bf16):
    b, m, d = mem.shape
    n = w_bf16.shape[1]
    return pl.pallas_call(
        _memkv_kernel,
        out_shape=jax.ShapeDtypeStruct((b, m, n), BF16),
        grid=(b,),
        in_specs=[pl.BlockSpec((1, m, d), lambda i: (i, 0, 0)),
                  pl.BlockSpec((1, d), lambda i: (0, 0)),
                  pl.BlockSpec((d, n), lambda i: (0, 0))],
        out_specs=pl.BlockSpec((1, m, n), lambda i: (i, 0, 0)),
        compiler_params=_cparams(("parallel",)),
        name="mem_kv",
    )(mem, g, w_bf16)


def _memattn_kernel(q_ref, kv_ref, o_ref):
    w = MEM_HEADS * MEM_HEAD_DIM
    for h in range(MEM_HEADS):
        lo, hi = h * MEM_HEAD_DIM, (h + 1) * MEM_HEAD_DIM
        s = _dot_nt(q_ref[0, :, lo:hi].astype(BF16), kv_ref[0, :, lo:hi]) * (MEM_HEAD_DIM ** -0.5)
        e = jnp.exp(s - jnp.max(s, axis=-1, keepdims=True))
        p = e / jnp.sum(e, axis=-1, keepdims=True)
        o_ref[0, :, lo:hi] = _dot(p.astype(BF16), kv_ref[0, :, w + lo:w + hi])


def _mem_attend(proj3, kv, tq=512):
    b, s, _ = proj3.shape
    m = kv.shape[1]
    w = MEM_HEADS * MEM_HEAD_DIM
    return pl.pallas_call(
        _memattn_kernel,
        out_shape=jax.ShapeDtypeStruct((b, s, w), F32),
        grid=(b, s // tq),
        in_specs=[pl.BlockSpec((1, tq, w), lambda i, j: (i, j, C_QM // w)),
                  pl.BlockSpec((1, m, 2 * w), lambda i, j: (i, 0, 0))],
        out_specs=pl.BlockSpec((1, tq, w), lambda i, j: (i, j, 0)),
        compiler_params=_cparams(("parallel", "parallel")),
        name="mem_attend",
    )(proj3, kv)


def _merge_kernel(on_ref, of_ref, om_ref, ln_ref, lf_ref, lm_ref, bm_ref, wn_ref, wf_ref, wm_ref,
                  wo_ref, x_ref, g_ref, wr_ref, br_ref, x1_ref, hn_ref, idx_ref, gate_ref):
    d = x_ref.shape[1]

    def branch(o_ref, l_ref, w_ref, k):
        gate = jax.nn.sigmoid(l_ref[...] + bm_ref[:, k * d:(k + 1) * d])
        return gate * _dot(o_ref[...].astype(BF16), w_ref[...])

    merged = branch(on_ref, ln_ref, wn_ref, 0) + branch(of_ref, lf_ref, wf_ref, 1) + branch(om_ref, lm_ref, wm_ref, 2)
    x1 = x_ref[...] + _dot(merged.astype(BF16), wo_ref[...])
    x1_ref[...] = x1
    ms = jnp.mean(x1 * x1, axis=-1, keepdims=True)
    hn = x1 * lax.rsqrt(ms + RMS_EPS) * g_ref[...]
    hn_ref[...] = hn
    h_hi = hn.astype(BF16)
    h_lo = (hn - h_hi.astype(F32)).astype(BF16)
    wr = wr_ref[...]
    w_hi = wr.astype(BF16)
    w_lo = (wr - w_hi.astype(F32)).astype(BF16)
    logits = (_dot(h_hi, w_hi) + (_dot(h_lo, w_hi) + _dot(h_hi, w_lo))) + br_ref[...]
    tm, ne = logits.shape
    lane_e = lax.broadcasted_iota(jnp.int32, (tm, ne), 1)
    lane = lax.broadcasted_iota(jnp.int32, (tm, LANES), 1)
    idx_slab = jnp.zeros((tm, LANES), jnp.int32)
    val_slab = jnp.zeros((tm, LANES), F32)
    work = logits
    vals = []
    for k in range(TOP_K):
        m = jnp.max(work, axis=-1, keepdims=True)
        idx = jnp.min(jnp.where(work == m, lane_e, ne), axis=-1, keepdims=True)
        work = jnp.where(lane_e == idx, -jnp.inf, work)
        idx_slab = jnp.where(lane == k, idx, idx_slab)
        vals.append(m)
    es = [jnp.exp(v - vals[0]) for v in vals]
    tot = es[0]
    for e in es[1:]:
        tot = tot + e
    for k in range(TOP_K):
        val_slab = jnp.where(lane == k, es[k] / tot, val_slab)
    idx_ref[...] = idx_slab
    gate_ref[...] = val_slab


def _merge_route(o_nsa, o_fox, o_mem, proj, b_merge, wn, wf, wm, wo, x2, g_ffn, w_router, b_router, tm=512):
    t, d = x2.shape
    wb = o_nsa.shape[1]
    row = lambda w: pl.BlockSpec((tm, w), lambda i: (i, 0))
    full = lambda a: pl.BlockSpec(a.shape, lambda i: (0,) * a.ndim)
    return pl.pallas_call(
        _merge_kernel,
        out_shape=(jax.ShapeDtypeStruct((t, d), F32), jax.ShapeDtypeStruct((t, d), F32),
                   jax.ShapeDtypeStruct((t, LANES), jnp.int32), jax.ShapeDtypeStruct((t, LANES), F32)),
        grid=(t // tm,),
        in_specs=[row(wb), row(wb), row(wb),
                  pl.BlockSpec((tm, d), lambda i: (i, 0)),
                  pl.BlockSpec((tm, d), lambda i: (i, 1)),
                  pl.BlockSpec((tm, d), lambda i: (i, 2)),
                  full(b_merge), full(wn), full(wf), full(wm), full(wo),
                  row(d), full(g_ffn), full(w_router), full(b_router)],
        out_specs=(row(d), row(d), row(LANES), row(LANES)),
        compiler_params=_cparams(("parallel",)),
        name="merge_route",
    )(o_nsa, o_fox, o_mem, proj, proj, proj, b_merge, wn, wf, wm, wo, x2, g_ffn, w_router, b_router)


def _rank_kernel(idx_ref, rank_ref, cnt_ref, carry_ref, *, tm):
    @pl.when(pl.program_id(0) == 0)
    def _():
        carry_ref[...] = jnp.zeros(carry_ref.shape, F32)

    idx = idx_ref[...]
    lane_e = lax.broadcasted_iota(jnp.int32, (tm, N_EXPERTS), 1)
    hots = [jnp.where(idx[:, k:k + 1] == lane_e, 1.0, 0.0) for k in range(TOP_K)]
    cnt = hots[0]
    for hk in hots[1:]:
        cnt = cnt + hk
    r = lax.broadcasted_iota(jnp.int32, (tm, tm), 0)
    c = lax.broadcasted_iota(jnp.int32, (tm, tm), 1)
    strict = jnp.where(c < r, 1.0, 0.0).astype(BF16)
    before = _dot(strict, cnt.astype(BF16)) + carry_ref[...]
    lane = lax.broadcasted_iota(jnp.int32, (tm, LANES), 1)
    slab = jnp.zeros((tm, LANES), F32)
    for k in range(TOP_K):
        slab = jnp.where(lane == k, jnp.sum(hots[k] * before, axis=-1, keepdims=True), slab)
    rank_ref[...] = slab.astype(jnp.int32)
    total = carry_ref[...] + jnp.sum(cnt, axis=0, keepdims=True)
    carry_ref[...] = total
    cnt_ref[...] = jnp.broadcast_to(total, cnt_ref.shape).astype(jnp.int32)


def _moe_rank(idx_slab, tm=256):
    t = idx_slab.shape[0]
    return pl.pallas_call(
        functools.partial(_rank_kernel, tm=tm),
        out_shape=(jax.ShapeDtypeStruct((t, LANES), jnp.int32),
                   jax.ShapeDtypeStruct((8, N_EXPERTS), jnp.int32)),
        grid=(t // tm,),
        in_specs=[pl.BlockSpec((tm, LANES), lambda i: (i, 0))],
        out_specs=(pl.BlockSpec((tm, LANES), lambda i: (i, 0)),
                   pl.BlockSpec((8, N_EXPERTS), lambda i: (0, 0))),
        scratch_shapes=[pltpu.VMEM((1, N_EXPERTS), F32)],
        compiler_params=_cparams(("arbitrary",)),
        name="moe_rank",
    )(idx_slab)


def _expert_kernel(be_ref, nu_ref, nx_ref, tok_ref, hn_ref, wgu_ref, wdn_ref, bg_ref, bu_ref, bd_ref, o_ref,
                   buf_ref, act_ref, sgu_ref, sdn_ref, wg_ref, wu_ref, wd_ref, sem, wsem, *, unroll, chunk):
    i = pl.program_id(0)
    n_used = nu_ref[0]
    e = be_ref[i]
    f = wg_ref.shape[1]
    d = wd_ref.shape[1]

    def row_copy(tok, r, s):
        return pltpu.make_async_copy(hn_ref.at[pl.ds(tok, 1)], buf_ref.at[s, pl.ds(r, 1)], sem.at[s])

    def weight_copies(ex):
        return (pltpu.make_async_copy(wgu_ref.at[ex], sgu_ref, wsem.at[0]),
                pltpu.make_async_copy(wdn_ref.at[ex], sdn_ref, wsem.at[1]))

    def wait_rows(s):
        def wait(r, c):
            row_copy(0, 0, s).wait()
            return c

        lax.fori_loop(0, MOE_ROWS, wait, 0, unroll=unroll)

    @pl.when((i == 0) & (n_used > 0))
    def _():
        for cp in weight_copies(e):
            cp.start(priority=1)

        def body(r2, c):
            for j in range(2):
                row_copy(tok_ref[2 * r2 + j], 2 * r2 + j, 0).start()
            return c

        lax.fori_loop(0, MOE_ROWS // 2, body, 0, unroll=unroll)

    first = (i == 0) | (e != be_ref[jnp.maximum(i - 1, 0)])

    @pl.when((i < n_used) & first)
    def _():
        for cp in weight_copies(e):
            cp.wait()
        half = chunk // 2
        pr = lax.broadcasted_iota(jnp.int32, (chunk, chunk), 0)
        pc = lax.broadcasted_iota(jnp.int32, (chunk, chunk), 1)
        perm = jnp.where(pr == jnp.where(pc < half, 2 * pc, 2 * (pc - half) + 1), 1.0, 0.0).astype(BF16)
        for c in range(2 * f // chunk):
            sep = _dot(sgu_ref[:, c * chunk:(c + 1) * chunk].astype(BF16), perm).astype(BF16)
            wg_ref[:, c * half:(c + 1) * half] = sep[:, :half]
            wu_ref[:, c * half:(c + 1) * half] = sep[:, half:]
        wd_ref[...] = sdn_ref[...].astype(BF16)

        @pl.when(nx_ref[i] >= 0)
        def _():
            for cp in weight_copies(nx_ref[i]):
                cp.start(priority=1)

    @pl.when(i < n_used)
    def _():
        cur = i % 2
        nxt = 1 - cur
        base = jnp.minimum(i + 1, n_used - 1) * MOE_ROWS
        n_groups = f // chunk + d // chunk
        per = MOE_ROWS // n_groups

        def issue_group(k):
            for r in range(k * per, (k + 1) * per):
                row_copy(tok_ref[base + r], r, nxt).start()

        wait_rows(cur)
        x = buf_ref[cur].astype(BF16)
        for c in range(f // chunk):
            cols = slice(c * chunk, (c + 1) * chunk)
            g = jnp.minimum(_dot(x, wg_ref[:, cols]) + bg_ref[0, :, cols], SWIGLU_LIMIT)
            u = jnp.clip(_dot(x, wu_ref[:, cols]) + bu_ref[0, :, cols], -SWIGLU_LIMIT, SWIGLU_LIMIT)
            act_ref[:, cols] = ((u + 1.0) * g * jax.nn.sigmoid(SWIGLU_ALPHA * g)).astype(BF16)
            issue_group(c)
        act = act_ref[...]
        for c in range(d // chunk):
            cols = slice(c * chunk, (c + 1) * chunk)
            o_ref[:, cols] = _dot(act, wd_ref[:, cols]) + bd_ref[0, :, cols]
            issue_group(f // chunk + c)

        @pl.when(i == n_used - 1)
        def _():
            wait_rows(nxt)

    @pl.when(i >= n_used)
    def _():
        o_ref[...] = jnp.zeros(o_ref.shape, o_ref.dtype)


def _moe_experts(block_exp, n_used, next_exp, slot_tok, hn, w_gate_up, w_down, bg, bu, bd):
    n_slots = slot_tok.shape[0]
    d = hn.shape[1]
    f = w_down.shape[1]
    nblk = n_slots // MOE_ROWS
    bspec = lambda w: pl.BlockSpec((1, 1, w), lambda i, be, nu, nx, tk: (be[i], 0, 0))
    return pl.pallas_call(
        functools.partial(_expert_kernel, unroll=8, chunk=256),
        out_shape=jax.ShapeDtypeStruct((n_slots, d), F32),
        grid_spec=pltpu.PrefetchScalarGridSpec(
            num_scalar_prefetch=4,
            grid=(nblk,),
            in_specs=[pl.BlockSpec(memory_space=pl.ANY), pl.BlockSpec(memory_space=pl.ANY),
                      pl.BlockSpec(memory_space=pl.ANY), bspec(f), bspec(f), bspec(d)],
            out_specs=pl.BlockSpec((MOE_ROWS, d), lambda i, be, nu, nx, tk: (i, 0)),
            scratch_shapes=[pltpu.VMEM((2, MOE_ROWS, d), hn.dtype), pltpu.VMEM((MOE_ROWS, f), BF16),
                            pltpu.VMEM((d, 2 * f), F32), pltpu.VMEM((f, d), F32),
                            pltpu.VMEM((d, f), BF16), pltpu.VMEM((d, f), BF16), pltpu.VMEM((f, d), BF16),
                            pltpu.SemaphoreType.DMA((2,)), pltpu.SemaphoreType.DMA((2,))]),
        compiler_params=_cparams(("arbitrary",)),
        name="moe_experts",
    )(block_exp, n_used, next_exp, slot_tok, hn, w_gate_up, w_down, bg, bu, bd)


def _combine_kernel(dest_ref, gate_ref, x1_ref, g_ref, ys_ref, o_ref, buf_ref, sem, *, tm, unroll):
    i = pl.program_id(0)
    n = pl.num_programs(0)

    def row_copy(slot_row, k, r, s):
        return pltpu.make_async_copy(ys_ref.at[pl.ds(slot_row, 1)], buf_ref.at[s, k, pl.ds(r, 1)], sem.at[s])

    def issue(step, s):
        base = step * (tm * TOP_K)
        for r in range(tm):
            for k in range(TOP_K):
                row_copy(dest_ref[base + r * TOP_K + k], k, r, s).start()

    @pl.when(i == 0)
    def _():
        issue(0, 0)

    @pl.when(i + 1 < n)
    def _():
        issue(i + 1, (i + 1) % 2)

    cur = i % 2

    def wait(r, c):
        row_copy(0, 0, 0, cur).wait()
        return c

    lax.fori_loop(0, tm * TOP_K, wait, 0, unroll=unroll)

    gate = gate_ref[...]
    y = x1_ref[...]
    for k in range(TOP_K):
        y = y + gate[:, k:k + 1] * buf_ref[cur, k]
    ms = jnp.mean(y * y, axis=-1, keepdims=True)
    o_ref[...] = y * lax.rsqrt(ms + RMS_EPS) * g_ref[...]


def _moe_combine(dest_flat, gate_slab, x1, g_final, ys, tm=128):
    t, d = x1.shape
    return pl.pallas_call(
        functools.partial(_combine_kernel, tm=tm, unroll=8),
        out_shape=jax.ShapeDtypeStruct((t, d), F32),
        grid_spec=pltpu.PrefetchScalarGridSpec(
            num_scalar_prefetch=1,
            grid=(t // tm,),
            in_specs=[pl.BlockSpec((tm, LANES), lambda i, dst: (i, 0)),
                      pl.BlockSpec((tm, d), lambda i, dst: (i, 0)),
                      pl.BlockSpec((1, d), lambda i, dst: (0, 0)),
                      pl.BlockSpec(memory_space=pl.ANY)],
            out_specs=pl.BlockSpec((tm, d), lambda i, dst: (i, 0)),
            scratch_shapes=[pltpu.VMEM((2, TOP_K, tm, d), F32), pltpu.SemaphoreType.DMA((2,))]),
        compiler_params=_cparams(("arbitrary",)),
        name="moe_combine",
    )(dest_flat, gate_slab, x1, g_final, ys)


def _rope_tables(s):
    inv = ROPE_THETA ** (-jnp.arange(0, HEAD_DIM, 2, dtype=F32) / HEAD_DIM)
    ang = jnp.arange(s, dtype=F32)[:, None] * inv[None, :]
    cos, sin = jnp.cos(ang), jnp.sin(ang)
    return jnp.concatenate([cos, cos], axis=-1), jnp.concatenate([-sin, sin], axis=-1)


def _permute_w_in(w):
    d = w.shape[0]
    pad = N_PROJ - (C_SMALL + 32)
    cols = [w[:, 3360:6432], w[:, 0:512], w[:, 1304:1816], w[:, 1816:2328], w[:, 2328:2840],
            w[:, 2848:3360], w[:, 512:1280], w[:, 1280:1304], w[:, 2840:2848], jnp.zeros((d, pad), w.dtype)]
    return jnp.concatenate(cols, axis=1).astype(BF16)


def _layer(x, mem, g_mix, w_in, b_forget, b_merge, pe_k, w1_k, w2_k, pe_v, w1_v, w2_v, g_mem, w_mem_kv,
           w_br_nsa, w_br_fox, w_br_mem, w_out, g_ffn, w_router, b_router, w_gate_up, b_gate_up,
           w_down, b_down, g_final):
    b, s, d = x.shape
    t = b * s
    x2 = x.reshape(t, d)
    cos64, sin64 = _rope_tables(s)
    cos2 = jnp.concatenate([cos64, cos64], axis=-1)
    sin2 = jnp.concatenate([sin64, sin64], axis=-1)

    proj = _inproj(x2, g_mix.reshape(1, d), _permute_w_in(w_in), tm=min(1024, t))
    proj3 = proj.reshape(b, s, N_PROJ)
    b_row = jnp.zeros((1, LANES), F32).at[0, FL_OFF:FL_OFF + FOX_HEADS].set(b_forget)
    _, cum_t, cum_x = _fox_cum(proj3, b_row)

    nb = s // CMP_STRIDE
    wide = CMP_STRIDE * HEAD_DIM

    def to_blocks(c0):
        a = proj3[:, :, c0:c0 + NSA_GROUPS * HEAD_DIM].reshape(b, nb, CMP_STRIDE, NSA_GROUPS, HEAD_DIM)
        return a.transpose(0, 3, 1, 2, 4).reshape(b, NSA_GROUPS, nb, wide)

    kcmp, vcmp_t = _nsa_compress(
        to_blocks(C_NSAKV), to_blocks(C_NSAKV + LANES), cos64.reshape(nb, wide), sin64.reshape(nb, wide),
        pe_k.reshape(2, wide), pe_v.reshape(2, wide),
        w1_k.reshape(CMP_LEN * HEAD_DIM, HEAD_DIM).astype(BF16), w1_v.reshape(CMP_LEN * HEAD_DIM, HEAD_DIM).astype(BF16),
        w2_k.astype(BF16), w2_v.astype(BF16))
    fox_tile = min(FOX_TILE, s)
    kf, vft, ks, vst, kw, vwt = _attn_prep(proj3, cos2, sin2, fox_tile, NSA_TILE)
    o_nsa = _nsa_attend(proj3, cos2, sin2, kcmp, vcmp_t, ks, vst, kw, vwt, NSA_TILE)
    o_fox = _fox_attend(proj3, kf, vft, cum_x, cum_t, fox_tile)
    mem_kv = _mem_kv(mem, g_mem.reshape(1, d), w_mem_kv.astype(BF16))
    o_mem = _mem_attend(proj3, mem_kv)

    x1, hn, idx_slab, gate_slab = _merge_route(
        o_nsa.reshape(t, -1), o_fox.reshape(t, -1), o_mem.reshape(t, -1), proj, b_merge.reshape(1, -1),
        w_br_nsa.astype(BF16), w_br_fox.astype(BF16), w_br_mem.astype(BF16), w_out.astype(BF16),
        x2, g_ffn.reshape(1, d), w_router, b_router.reshape(1, -1))

    rank_slab, cnt8 = _moe_rank(idx_slab)
    counts = cnt8[0]
    padded = (counts + MOE_ROWS - 1) // MOE_ROWS * MOE_ROWS
    pad_end = jnp.cumsum(padded)
    pad_start = pad_end - padded
    n_assign = t * TOP_K
    nblk = -(-(n_assign + N_EXPERTS * (MOE_ROWS - 1)) // MOE_ROWS)
    n_slots = nblk * MOE_ROWS
    top_idx = idx_slab[:, :TOP_K]
    dest = (pad_start[top_idx] + rank_slab[:, :TOP_K]).reshape(-1).astype(jnp.int32)
    tok_flat = jnp.repeat(jnp.arange(t, dtype=jnp.int32), TOP_K)
    slot_tok = jnp.zeros((n_slots,), jnp.int32).at[dest].set(tok_flat, unique_indices=True)
    blk_start = jnp.arange(nblk, dtype=jnp.int32) * MOE_ROWS
    block_exp = jnp.minimum(jnp.sum(blk_start[:, None] >= pad_end[None, :], axis=1),
                            N_EXPERTS - 1).astype(jnp.int32)
    n_used = (pad_end[-1] // MOE_ROWS).astype(jnp.int32).reshape(1)

    e_ids = jnp.arange(N_EXPERTS, dtype=jnp.int32)
    later = (e_ids[None, :] > block_exp[:, None]) & (counts[None, :] > 0)
    next_exp = jnp.min(jnp.where(later, e_ids[None, :], N_EXPERTS), axis=1)
    next_exp = jnp.where(next_exp < N_EXPERTS, next_exp, -1).astype(jnp.int32)

    f = w_down.shape[1]
    bgu = b_gate_up.reshape(N_EXPERTS, 1, f, 2)
    ys = _moe_experts(block_exp, n_used, next_exp, slot_tok, hn, w_gate_up, w_down,
                      bgu[..., 0], bgu[..., 1], b_down.reshape(N_EXPERTS, 1, d))
    out = _moe_combine(dest, gate_slab, x1, g_final.reshape(1, d), ys)
    return out.reshape(b, s, d)


def kernel(x, mem, g_mix, w_in, b_forget, b_merge, nsa_pe_k, nsa_w1_k, nsa_w2_k, nsa_pe_v, nsa_w1_v, nsa_w2_v, g_mem, w_mem_kv, w_branch_nsa, w_branch_fox, w_branch_mem, w_out, g_ffn, w_router, b_router, w_gate_up, b_gate_up, w_down, b_down, g_final):
    assert g_mix.shape[0] == 1, "single-layer trunk"
    return _layer(x, mem, g_mix[0], w_in[0], b_forget[0], b_merge[0], nsa_pe_k[0], nsa_w1_k[0], nsa_w2_k[0],
                  nsa_pe_v[0], nsa_w1_v[0], nsa_w2_v[0], g_mem[0], w_mem_kv[0], w_branch_nsa[0],
                  w_branch_fox[0], w_branch_mem[0], w_out[0], g_ffn[0], w_router[0], b_router[0],
                  w_gate_up[0], b_gate_up[0], w_down[0], b_down[0], g_final)
```

```python
import functools

import jax
import jax.numpy as jnp
import numpy as np
from jax import lax
from jax.experimental import pallas as pl
from jax.experimental.pallas import tpu as pltpu

F32 = jnp.float32
BF16 = jnp.bfloat16

D_MODEL = 1024
HEAD_DIM = 64
ROPE_THETA = 10000.0
RMS_EPS = 1e-5
NEG_INF = -1e30
FORCE_SCORE = 1e4
NSA_HEADS = 8
NSA_GROUPS = 2
NSA_REP = NSA_HEADS // NSA_GROUPS
CMP_LEN = 32
CMP_STRIDE = 16
SEL_BLOCK = 64
SEL_TOPK = 16
WINDOW = 512
FOX_HEADS = 8
MEM_HEADS = 4
MEM_HEAD_DIM = 128
N_EXPERTS = 32
TOP_K = 4
SWIGLU_LIMIT = 7.0
SWIGLU_ALPHA = 1.702

LANES = 128
VMEM_LIMIT = 48 * 1024 * 1024

C_MERGE = 0
C_QA = 3072
C_QF = 3584
C_KF = 4096
C_VF = 4608
C_QM = 5120
C_NSAKV = 5632
C_SMALL = 6400
N_PROJ = 6656
GL_OFF = 0
FL_OFF = 24

MOE_ROWS = 256
NSA_TILE = 256
FOX_TILE = 512


def _cparams(sem, vmem=VMEM_LIMIT):
    return pltpu.CompilerParams(dimension_semantics=sem, vmem_limit_bytes=vmem)


def _dot(a, b):
    return jnp.dot(a, b, preferred_element_type=F32)


def _dot_nt(a, b):
    return lax.dot_general(a, b, (((1,), (1,)), ((), ())), preferred_element_type=F32)


def _rope(x, cos, sin_signed):
    w = x.shape[-1]
    lane = lax.broadcasted_iota(jnp.int32, x.shape, x.ndim - 1)
    first = (lane & (HEAD_DIM - 1)) < (HEAD_DIM // 2)
    rot = jnp.where(first, pltpu.roll(x, w - HEAD_DIM // 2, x.ndim - 1),
                    pltpu.roll(x, HEAD_DIM // 2, x.ndim - 1))
    return x * cos + rot * sin_signed


def _split3(x):
    hi = x.astype(BF16)
    r1 = x - hi.astype(F32)
    mid = r1.astype(BF16)
    lo = (r1 - mid.astype(F32)).astype(BF16)
    return hi, mid, lo


def _inproj_kernel(x_ref, g_ref, w_ref, o_ref, hn_ref):
    @pl.when(pl.program_id(1) == 0)
    def _():
        x = x_ref[...]
        ms = jnp.mean(x * x, axis=-1, keepdims=True)
        hn_ref[...] = (x * lax.rsqrt(ms + RMS_EPS) * g_ref[...]).astype(BF16)

    o_ref[...] = _dot(hn_ref[...], w_ref[...])


def _inproj(x2, g, w_bf16, tm=1024, tn=512):
    t, d = x2.shape
    n = w_bf16.shape[1]
    return pl.pallas_call(
        _inproj_kernel,
        out_shape=jax.ShapeDtypeStruct((t, n), F32),
        grid=(t // tm, n // tn),
        in_specs=[pl.BlockSpec((tm, d), lambda i, j: (i, 0)),
                  pl.BlockSpec((1, d), lambda i, j: (0, 0)),
                  pl.BlockSpec((d, tn), lambda i, j: (0, j))],
        out_specs=pl.BlockSpec((tm, tn), lambda i, j: (i, j)),
        scratch_shapes=[pltpu.VMEM((tm, d), BF16)],
        compiler_params=_cparams(("parallel", "arbitrary")),
        name="inproj",
    )(x2, g, w_bf16)


def _cum_kernel(s_ref, b_ref, c_ref, ct_ref, *, blk):
    s = s_ref.shape[1]
    z = s_ref[0] + b_ref[...]
    logf = jnp.minimum(z, 0.0) - jnp.log1p(jnp.exp(-jnp.abs(z)))
    r = lax.broadcasted_iota(jnp.int32, (blk, blk), 0)
    c = lax.broadcasted_iota(jnp.int32, (blk, blk), 1)
    tri = jnp.where(c <= r, 1.0, 0.0).astype(BF16)
    carry = jnp.zeros((1, LANES), F32)
    for i in range(s // blk):
        hi, mid, lo = _split3(logf[i * blk:(i + 1) * blk])
        loc = (_dot(tri, hi) + _dot(tri, mid)) + _dot(tri, lo)
        out = loc + carry
        c_ref[0, i * blk:(i + 1) * blk, :] = out
        carry = out[blk - 1:blk, :]
    ct_ref[0] = c_ref[0].T


def _fox_cum(proj3, b_row, blk=256):
    b, s, _ = proj3.shape
    return pl.pallas_call(
        functools.partial(_cum_kernel, blk=blk),
        out_shape=(jax.ShapeDtypeStruct((b, s, LANES), F32),
                   jax.ShapeDtypeStruct((b, LANES, s), F32)),
        grid=(b,),
        in_specs=[pl.BlockSpec((1, s, LANES), lambda i: (i, 0, C_SMALL // LANES)),
                  pl.BlockSpec((1, LANES), lambda i: (0, 0))],
        out_specs=(pl.BlockSpec((1, s, LANES), lambda i: (i, 0, 0)),
                   pl.BlockSpec((1, LANES, s), lambda i: (i, 0, 0))),
        compiler_params=_cparams(("parallel",)),
        name="fox_cum",
    )(proj3, b_row)


def _cmp_kernel(ak_ref, av_ref, cos_ref, sin_ref, pek_ref, pev_ref, w1k_ref, w1v_ref,
                w2k_ref, w2v_ref, kc_ref, vct_ref, vbuf_ref):
    nb = ak_ref.shape[2]
    half = ak_ref.shape[3]

    def mlp(a, pe_ref, w1_ref, w2_ref):
        pa = _dot((a + pe_ref[0:1, :]).astype(BF16), w1_ref[0:half, :])
        pb = _dot((a + pe_ref[1:2, :]).astype(BF16), w1_ref[half:2 * half, :])
        z = pa + pltpu.roll(pb, nb - 1, 0)
        h = z * jax.nn.sigmoid(z)
        return _dot(h.astype(BF16), w2_ref[...])

    for g in range(NSA_GROUPS):
        lo, hi = g * HEAD_DIM, (g + 1) * HEAD_DIM
        kc_ref[0, :, lo:hi] = mlp(_rope(ak_ref[0, g], cos_ref[...], sin_ref[...]),
                                  pek_ref, w1k_ref, w2k_ref).astype(BF16)
        vbuf_ref[:, lo:hi] = mlp(av_ref[0, g], pev_ref, w1v_ref, w2v_ref)
    vct_ref[0] = vbuf_ref[...].T.astype(BF16)


def _nsa_compress(ak, av, cos_a, sin_a, pek, pev, w1k, w1v, w2k, w2v):
    b, g, nb, wide = ak.shape
    blk4 = pl.BlockSpec((1, g, nb, wide), lambda i: (i, 0, 0, 0))
    full = lambda shp: pl.BlockSpec(shp, lambda i: (0,) * len(shp))
    return pl.pallas_call(
        _cmp_kernel,
        out_shape=(jax.ShapeDtypeStruct((b, nb, g * HEAD_DIM), BF16),
                   jax.ShapeDtypeStruct((b, g * HEAD_DIM, nb), BF16)),
        grid=(b,),
        in_specs=[blk4, blk4, full(cos_a.shape), full(sin_a.shape), full(pek.shape), full(pev.shape),
                  full(w1k.shape), full(w1v.shape), full(w2k.shape), full(w2v.shape)],
        out_specs=(pl.BlockSpec((1, nb, g * HEAD_DIM), lambda i: (i, 0, 0)),
                   pl.BlockSpec((1, g * HEAD_DIM, nb), lambda i: (i, 0, 0))),
        scratch_shapes=[pltpu.VMEM((nb, g * HEAD_DIM), F32)],
        compiler_params=_cparams(("parallel",)),
        name="nsa_compress",
    )(ak, av, cos_a, sin_a, pek, pev, w1k, w1v, w2k, w2v)


def _prep_kernel(kf_ref, vf_ref, ks_ref, vs_ref, kw_ref, vw_ref, cos_ref, sin_ref,
                 okf_ref, ovf_ref, oks_ref, ovs_ref, okw_ref, ovw_ref):
    okf_ref[0] = kf_ref[0].astype(BF16)
    ovf_ref[0, 0] = vf_ref[0].T.astype(BF16)
    oks_ref[0] = _rope(ks_ref[0], cos_ref[...], sin_ref[...]).astype(BF16)
    okw_ref[0] = _rope(kw_ref[0], cos_ref[...], sin_ref[...]).astype(BF16)
    tn = ovs_ref.shape[3]
    for i in range(ovs_ref.shape[1]):
        ovs_ref[0, i] = vs_ref[0, i * tn:(i + 1) * tn, :].T.astype(BF16)
        ovw_ref[0, i] = vw_ref[0, i * tn:(i + 1) * tn, :].T.astype(BF16)


def _attn_prep(proj3, cos2, sin2, tk, tn):
    b, s, _ = proj3.shape
    wf = FOX_HEADS * HEAD_DIM
    kvb = C_NSAKV // LANES
    nk = s // tk
    sub = tk // tn
    col = lambda w, c: pl.BlockSpec((1, tk, w), lambda i, j: (i, j, c))
    rows = lambda w: pl.BlockSpec((1, tk, w), lambda i, j: (i, j, 0))
    ntile = pl.BlockSpec((1, sub, LANES, tn), lambda i, j: (i, j, 0, 0))
    return pl.pallas_call(
        _prep_kernel,
        out_shape=(jax.ShapeDtypeStruct((b, s, wf), BF16), jax.ShapeDtypeStruct((b, nk, wf, tk), BF16),
                   jax.ShapeDtypeStruct((b, s, LANES), BF16), jax.ShapeDtypeStruct((b, nk * sub, LANES, tn), BF16),
                   jax.ShapeDtypeStruct((b, s, LANES), BF16), jax.ShapeDtypeStruct((b, nk * sub, LANES, tn), BF16)),
        grid=(b, nk),
        in_specs=[col(wf, C_KF // wf), col(wf, C_VF // wf), col(LANES, kvb + 2), col(LANES, kvb + 3),
                  col(LANES, kvb + 4), col(LANES, kvb + 5),
                  pl.BlockSpec((tk, LANES), lambda i, j: (j, 0)), pl.BlockSpec((tk, LANES), lambda i, j: (j, 0))],
        out_specs=(rows(wf), pl.BlockSpec((1, 1, wf, tk), lambda i, j: (i, j, 0, 0)),
                   rows(LANES), ntile, rows(LANES), ntile),
        compiler_params=_cparams(("parallel", "parallel")),
        name="attn_prep",
    )(proj3, proj3, proj3, proj3, proj3, proj3, cos2, sin2)


def _online_update(s, vt, m_ref, l_ref, acc_ref, idx, mask=None):
    m_old = m_ref[idx]
    m_new = jnp.maximum(m_old, jnp.max(s, axis=0, keepdims=True))
    alpha = jnp.exp(m_old - m_new)
    p = jnp.exp(s - m_new)
    if mask is not None:
        p = jnp.where(mask, p, 0.0)
    l_ref[idx] = alpha * l_ref[idx] + jnp.sum(p, axis=0, keepdims=True)
    acc_ref[idx] = alpha * acc_ref[idx] + _dot(vt, p.astype(BF16))
    m_ref[idx] = m_new


def _padded_qt(q_ref, cos, sin, qt_ref, tq, heads_per_slot):
    qt_ref[...] = jnp.zeros(qt_ref.shape, qt_ref.dtype)
    n_heads = q_ref.shape[2] // HEAD_DIM
    for hp in range(n_heads // 2):
        q2 = q_ref[0, :, hp * LANES:(hp + 1) * LANES]
        if cos is not None:
            q2 = _rope(q2, cos, sin)
        qt = (q2 * (HEAD_DIM ** -0.5)).T.astype(BF16)
        for sub in range(2):
            h = 2 * hp + sub
            slot, r = h // heads_per_slot, h % heads_per_slot
            half = (slot % 2) if heads_per_slot > 1 else sub
            qt_ref[slot, half * HEAD_DIM:(half + 1) * HEAD_DIM, r * tq:(r + 1) * tq] = (
                qt[sub * HEAD_DIM:(sub + 1) * HEAD_DIM, :])


def _nsa_kernel(q_ref, cos_ref, sin_ref, kc_ref, vct_ref, ks_ref, vst_ref, kw_ref, vwt_ref, sm_ref, o_ref,
                qt_ref, oct_ref, m_ref, l_ref, acc_ref, *, tq, n_sel):
    qi = pl.program_id(1)
    tk = tq
    G, R = NSA_GROUPS, NSA_REP
    nb = kc_ref.shape[1]
    n_prev = WINDOW // tk
    _padded_qt(q_ref, cos_ref[...], sin_ref[...], qt_ref, tq, R)
    m_ref[...] = jnp.full(m_ref.shape, NEG_INF, F32)
    l_ref[...] = jnp.zeros(l_ref.shape, F32)
    acc_ref[...] = jnp.zeros(acc_ref.shape, F32)

    tpos = qi * tq + (lax.broadcasted_iota(jnp.int32, (nb, R * tq), 1) & (tq - 1))
    nrow = lax.broadcasted_iota(jnp.int32, (nb, R * tq), 0)
    cmask = (nrow * CMP_STRIDE + (CMP_LEN - 1) <= tpos) & (nrow < nb - 1)
    oj = lax.broadcasted_iota(jnp.int32, (n_sel, nb), 0)
    on = lax.broadcasted_iota(jnp.int32, (n_sel, nb), 1) * CMP_STRIDE
    overlap_t = jnp.where((on < (oj + 1) * SEL_BLOCK) & (on + CMP_LEN > oj * SEL_BLOCK), 1.0, 0.0).astype(BF16)
    blk = lax.broadcasted_iota(jnp.int32, (n_sel, tq), 0)
    trow = qi * tq + lax.broadcasted_iota(jnp.int32, (n_sel, tq), 1)
    cur = trow // SEL_BLOCK
    forced = (blk == 0) | (blk == cur) | (blk == cur - 1)
    future = blk * SEL_BLOCK > trow
    kc = kc_ref[0]
    for g in range(G):
        s = jnp.where(cmask, _dot(kc, qt_ref[g, 0:LANES, :]), NEG_INF)
        mx = jnp.max(s, axis=0, keepdims=True)
        e = jnp.where(cmask, jnp.exp(s - mx), 0.0)
        p = e / jnp.maximum(jnp.sum(e, axis=0, keepdims=True), 1e-30)
        oct_ref[g] = _dot(vct_ref[0, g * HEAD_DIM:(g + 1) * HEAD_DIM, :], p.astype(BF16))
        psum = p[:, 0:tq]
        for r in range(1, R):
            psum = psum + p[:, r * tq:(r + 1) * tq]
        hi, mid, lo = _split3(psum)
        imp = (_dot(overlap_t, hi) + _dot(overlap_t, mid)) + _dot(overlap_t, lo)
        imp = jnp.where(forced, FORCE_SCORE, jnp.where(future, -FORCE_SCORE, imp))
        rank = jnp.zeros((n_sel, tq), F32)
        for i in range(n_sel):
            ri = imp[i:i + 1, :]
            ahead = (ri > imp) | ((ri == imp) & (blk > i))
            rank = rank + jnp.where(ahead, 1.0, 0.0)
        bias = jnp.where(rank < float(min(SEL_TOPK, n_sel)), 0.0, NEG_INF).astype(BF16)
        for r in range(R):
            qt_ref[g, LANES:LANES + n_sel, r * tq:(r + 1) * tq] = bias

    qpos = qi * tq + (lax.broadcasted_iota(jnp.int32, (tk, R * tq), 1) & (tq - 1))
    krow = lax.broadcasted_iota(jnp.int32, (tk, R * tq), 0)
    er = lax.broadcasted_iota(jnp.int32, (tk, LANES), 0)
    ec = lax.broadcasted_iota(jnp.int32, (tk, LANES), 1)

    def sel_tile(j, diagonal):
        k = ks_ref[0, pl.ds(pl.multiple_of(j * tk, tk), tk), :]
        onehot = jnp.where((j * tk + er) // SEL_BLOCK == ec, 1.0, 0.0).astype(BF16)
        kx = jnp.concatenate([k, onehot], axis=1)
        for g in range(G):
            s = _dot(kx, qt_ref[g])
            if diagonal:
                s = jnp.where(j * tk + krow <= qpos, s, NEG_INF)
            _online_update(s, vst_ref[0, j, g * HEAD_DIM:(g + 1) * HEAD_DIM, :], m_ref, l_ref, acc_ref, g)

    def sel_body(j, c):
        sel_tile(j, False)
        return c

    lax.fori_loop(0, qi, sel_body, 0)
    sel_tile(qi, True)

    for back in range(n_prev, -1, -1):
        @pl.when(qi - back >= 0)
        def _(back=back):
            j = qi - back
            k = kw_ref[0, pl.ds(pl.multiple_of(j * tk, tk), tk), :]
            diff = qpos - (j * tk + krow)
            if back == 0:
                mask = diff >= 0
            elif back == n_prev:
                mask = diff < WINDOW
            else:
                mask = None
            for g in range(G):
                s = _dot(k, qt_ref[g, 0:LANES, :])
                if mask is not None:
                    s = jnp.where(mask, s, NEG_INF)
                _online_update(s, vwt_ref[0, j, g * HEAD_DIM:(g + 1) * HEAD_DIM, :], m_ref, l_ref, acc_ref,
                               G + g, mask if back == n_prev else None)

    gates_t = jax.nn.sigmoid(sm_ref[0]).T
    rows = []
    for h in range(NSA_HEADS):
        g, r = h // R, h % R
        cols = slice(r * tq, (r + 1) * tq)
        gate = lambda br: gates_t[GL_OFF + 3 * h + br:GL_OFF + 3 * h + br + 1, :]
        o_sel = acc_ref[g, :, cols] / jnp.maximum(l_ref[g, :, cols], 1e-30)
        o_win = acc_ref[G + g, :, cols] / jnp.maximum(l_ref[G + g, :, cols], 1e-30)
        rows.append(gate(0) * oct_ref[g, :, cols] + gate(1) * o_sel + gate(2) * o_win)
    o_ref[0] = jnp.concatenate(rows, axis=0).T


def _nsa_attend(proj3, cos2, sin2, kcmp, vcmp_t, ks, vst, kw, vwt, tq):
    b, s, _ = proj3.shape
    n_sel = s // SEL_BLOCK
    wq = NSA_HEADS * HEAD_DIM
    whole = lambda a: pl.BlockSpec((1,) + a.shape[1:], lambda i, j: (i,) + (0,) * (a.ndim - 1))
    return pl.pallas_call(
        functools.partial(_nsa_kernel, tq=tq, n_sel=n_sel),
        out_shape=jax.ShapeDtypeStruct((b, s, wq), F32),
        grid=(b, s // tq),
        in_specs=[pl.BlockSpec((1, tq, wq), lambda i, j: (i, j, C_QA // wq)),
                  pl.BlockSpec((tq, LANES), lambda i, j: (j, 0)),
                  pl.BlockSpec((tq, LANES), lambda i, j: (j, 0)),
                  whole(kcmp), whole(vcmp_t), whole(ks), whole(vst), whole(kw), whole(vwt),
                  pl.BlockSpec((1, tq, LANES), lambda i, j: (i, j, C_SMALL // LANES))],
        out_specs=pl.BlockSpec((1, tq, wq), lambda i, j: (i, j, 0)),
        scratch_shapes=[pltpu.VMEM((NSA_GROUPS, 2 * LANES, NSA_REP * tq), BF16),
                        pltpu.VMEM((NSA_GROUPS, HEAD_DIM, NSA_REP * tq), F32),
                        pltpu.VMEM((2 * NSA_GROUPS, 1, NSA_REP * tq), F32),
                        pltpu.VMEM((2 * NSA_GROUPS, 1, NSA_REP * tq), F32),
                        pltpu.VMEM((2 * NSA_GROUPS, HEAD_DIM, NSA_REP * tq), F32)],
        compiler_params=_cparams(("parallel", "parallel")),
        name="nsa_attend",
    )(proj3, cos2, sin2, kcmp, vcmp_t, ks, vst, kw, vwt, proj3)


def _fox_kernel(q_ref, k_ref, vt_ref, cq_ref, ck_ref, o_ref, qt_ref, m_ref, l_ref, acc_ref, *, tq):
    qi = pl.program_id(1)
    tk = tq
    _padded_qt(q_ref, None, None, qt_ref, tq, 1)
    m_ref[...] = jnp.full(m_ref.shape, NEG_INF, F32)
    l_ref[...] = jnp.zeros(l_ref.shape, F32)
    acc_ref[...] = jnp.zeros(acc_ref.shape, F32)
    cq = cq_ref[0]
    causal = lax.broadcasted_iota(jnp.int32, (tk, tq), 0) <= lax.broadcasted_iota(jnp.int32, (tk, tq), 1)

    def tile(j, diagonal):
        rows = pl.ds(pl.multiple_of(j * tk, tk), tk)
        ck = ck_ref[0, rows, :]
        for h in range(FOX_HEADS):
            hp = h // 2
            s = _dot(k_ref[0, rows, hp * LANES:(hp + 1) * LANES], qt_ref[h])
            s = (s - ck[:, FL_OFF + h:FL_OFF + h + 1]) + cq[FL_OFF + h:FL_OFF + h + 1, :]
            if diagonal:
                s = jnp.where(causal, s, NEG_INF)
            _online_update(s, vt_ref[0, j, h * HEAD_DIM:(h + 1) * HEAD_DIM, :], m_ref, l_ref, acc_ref, h)

    def body(j, c):
        tile(j, False)
        return c

    lax.fori_loop(0, qi, body, 0)
    tile(qi, True)
    rows = [acc_ref[h] / jnp.maximum(l_ref[h], 1e-30) for h in range(FOX_HEADS)]
    o_ref[0] = jnp.concatenate(rows, axis=0).T


def _fox_attend(proj3, kf, vft, cum, cum_t, tq):
    b, s, _ = proj3.shape
    w = FOX_HEADS * HEAD_DIM
    whole = lambda a: pl.BlockSpec((1,) + a.shape[1:], lambda i, j: (i,) + (0,) * (a.ndim - 1))
    return pl.pallas_call(
        functools.partial(_fox_kernel, tq=tq),
        out_shape=jax.ShapeDtypeStruct((b, s, w), F32),
        grid=(b, s // tq),
        in_specs=[pl.BlockSpec((1, tq, w), lambda i, j: (i, j, C_QF // w)),
                  whole(kf), whole(vft),
                  pl.BlockSpec((1, LANES, tq), lambda i, j: (i, 0, j)),
                  whole(cum)],
        out_specs=pl.BlockSpec((1, tq, w), lambda i, j: (i, j, 0)),
        scratch_shapes=[pltpu.VMEM((FOX_HEADS, LANES, tq), BF16),
                        pltpu.VMEM((FOX_HEADS, 1, tq), F32),
                        pltpu.VMEM((FOX_HEADS, 1, tq), F32),
                        pltpu.VMEM((FOX_HEADS, HEAD_DIM, tq), F32)],
        compiler_params=_cparams(("parallel", "parallel")),
        name="fox_attend",
    )(proj3, kf, vft, cum_t, cum)


def _memkv_kernel(m_ref, g_ref, w_ref, o_ref):
    x = m_ref[0]
    ms = jnp.mean(x * x, axis=-1, keepdims=True)
    hn = (x * lax.rsqrt(ms + RMS_EPS) * g_ref[...]).astype(BF16)
    o_ref[0] = _dot(hn, w_ref[...]).astype(BF16)


def _mem_kv(mem, g, w_bf16):
    b, m, d = mem.shape
    n = w_bf16.shape[1]
    return pl.pallas_call(
        _memkv_kernel,
        out_shape=jax.ShapeDtypeStruct((b, m, n), BF16),
        grid=(b,),
        in_specs=[pl.BlockSpec((1, m, d), lambda i: (i, 0, 0)),
                  pl.BlockSpec((1, d), lambda i: (0, 0)),
                  pl.BlockSpec((d, n), lambda i: (0, 0))],
        out_specs=pl.BlockSpec((1, m, n), lambda i: (i, 0, 0)),
        compiler_params=_cparams(("parallel",)),
        name="mem_kv",
    )(mem, g, w_bf16)


def _memattn_kernel(q_ref, kv_ref, o_ref):
    w = MEM_HEADS * MEM_HEAD_DIM
    for h in range(MEM_HEADS):
        lo, hi = h * MEM_HEAD_DIM, (h + 1) * MEM_HEAD_DIM
        s = _dot_nt(q_ref[0, :, lo:hi].astype(BF16), kv_ref[0, :, lo:hi]) * (MEM_HEAD_DIM ** -0.5)
        e = jnp.exp(s - jnp.max(s, axis=-1, keepdims=True))
        p = e / jnp.sum(e, axis=-1, keepdims=True)
        o_ref[0, :, lo:hi] = _dot(p.astype(BF16), kv_ref[0, :, w + lo:w + hi])


def _mem_attend(proj3, kv, tq=512):
    b, s, _ = proj3.shape
    m = kv.shape[1]
    w = MEM_HEADS * MEM_HEAD_DIM
    return pl.pallas_call(
        _memattn_kernel,
        out_shape=jax.ShapeDtypeStruct((b, s, w), F32),
        grid=(b, s // tq),
        in_specs=[pl.BlockSpec((1, tq, w), lambda i, j: (i, j, C_QM // w)),
                  pl.BlockSpec((1, m, 2 * w), lambda i, j: (i, 0, 0))],
        out_specs=pl.BlockSpec((1, tq, w), lambda i, j: (i, j, 0)),
        compiler_params=_cparams(("parallel", "parallel")),
        name="mem_attend",
    )(proj3, kv)


def _merge_kernel(on_ref, of_ref, om_ref, ln_ref, lf_ref, lm_ref, bm_ref, wn_ref, wf_ref, wm_ref,
                  wo_ref, x_ref, g_ref, wr_ref, br_ref, x1_ref, hn_ref, idx_ref, gate_ref):
    d = x_ref.shape[1]

    def branch(o_ref, l_ref, w_ref, k):
        gate = jax.nn.sigmoid(l_ref[...] + bm_ref[:, k * d:(k + 1) * d])
        return gate * _dot(o_ref[...].astype(BF16), w_ref[...])

    merged = branch(on_ref, ln_ref, wn_ref, 0) + branch(of_ref, lf_ref, wf_ref, 1) + branch(om_ref, lm_ref, wm_ref, 2)
    x1 = x_ref[...] + _dot(merged.astype(BF16), wo_ref[...])
    x1_ref[...] = x1
    ms = jnp.mean(x1 * x1, axis=-1, keepdims=True)
    hn = x1 * lax.rsqrt(ms + RMS_EPS) * g_ref[...]
    for a in range(d // LANES):
        hn_ref[:, a, :] = hn[:, a * LANES:(a + 1) * LANES]
    h_hi = hn.astype(BF16)
    h_lo = (hn - h_hi.astype(F32)).astype(BF16)
    wr = wr_ref[...]
    w_hi = wr.astype(BF16)
    w_lo = (wr - w_hi.astype(F32)).astype(BF16)
    logits = (_dot(h_hi, w_hi) + (_dot(h_lo, w_hi) + _dot(h_hi, w_lo))) + br_ref[...]
    tm, ne = logits.shape
    lane_e = lax.broadcasted_iota(jnp.int32, (tm, ne), 1)
    lane = lax.broadcasted_iota(jnp.int32, (tm, LANES), 1)
    idx_slab = jnp.zeros((tm, LANES), jnp.int32)
    val_slab = jnp.zeros((tm, LANES), F32)
    work = logits
    vals = []
    for k in range(TOP_K):
        m = jnp.max(work, axis=-1, keepdims=True)
        idx = jnp.min(jnp.where(work == m, lane_e, ne), axis=-1, keepdims=True)
        work = jnp.where(lane_e == idx, -jnp.inf, work)
        idx_slab = jnp.where(lane == k, idx, idx_slab)
        vals.append(m)
    es = [jnp.exp(v - vals[0]) for v in vals]
    tot = es[0]
    for e in es[1:]:
        tot = tot + e
    for k in range(TOP_K):
        val_slab = jnp.where(lane == k, es[k] / tot, val_slab)
    idx_ref[...] = idx_slab
    gate_ref[...] = val_slab


def _merge_route(o_nsa, o_fox, o_mem, proj, b_merge, wn, wf, wm, wo, x2, g_ffn, w_router, b_router, tm=512):
    t, d = x2.shape
    wb = o_nsa.shape[1]
    row = lambda w: pl.BlockSpec((tm, w), lambda i: (i, 0))
    full = lambda a: pl.BlockSpec(a.shape, lambda i: (0,) * a.ndim)
    return pl.pallas_call(
        _merge_kernel,
        out_shape=(jax.ShapeDtypeStruct((t, d), F32), jax.ShapeDtypeStruct((t, d // LANES, LANES), F32),
                   jax.ShapeDtypeStruct((t, LANES), jnp.int32), jax.ShapeDtypeStruct((t, LANES), F32)),
        grid=(t // tm,),
        in_specs=[row(wb), row(wb), row(wb),
                  pl.BlockSpec((tm, d), lambda i: (i, 0)),
                  pl.BlockSpec((tm, d), lambda i: (i, 1)),
                  pl.BlockSpec((tm, d), lambda i: (i, 2)),
                  full(b_merge), full(wn), full(wf), full(wm), full(wo),
                  row(d), full(g_ffn), full(w_router), full(b_router)],
        out_specs=(row(d), pl.BlockSpec((tm, d // LANES, LANES), lambda i: (i, 0, 0)), row(LANES), row(LANES)),
        compiler_params=_cparams(("parallel",)),
        name="merge_route",
    )(o_nsa, o_fox, o_mem, proj, proj, proj, b_merge, wn, wf, wm, wo, x2, g_ffn, w_router, b_router)


def _rank_kernel(idx_ref, rank_ref, cnt_ref, carry_ref, *, tm):
    @pl.when(pl.program_id(0) == 0)
    def _():
        carry_ref[...] = jnp.zeros(carry_ref.shape, F32)

    idx = idx_ref[...]
    lane_e = lax.broadcasted_iota(jnp.int32, (tm, N_EXPERTS), 1)
    hots = [jnp.where(idx[:, k:k + 1] == lane_e, 1.0, 0.0) for k in range(TOP_K)]
    cnt = hots[0]
    for hk in hots[1:]:
        cnt = cnt + hk
    r = lax.broadcasted_iota(jnp.int32, (tm, tm), 0)
    c = lax.broadcasted_iota(jnp.int32, (tm, tm), 1)
    strict = jnp.where(c < r, 1.0, 0.0).astype(BF16)
    before = _dot(strict, cnt.astype(BF16)) + carry_ref[...]
    lane = lax.broadcasted_iota(jnp.int32, (tm, LANES), 1)
    slab = jnp.zeros((tm, LANES), F32)
    for k in range(TOP_K):
        slab = jnp.where(lane == k, jnp.sum(hots[k] * before, axis=-1, keepdims=True), slab)
    rank_ref[...] = slab.astype(jnp.int32)
    total = carry_ref[...] + jnp.sum(cnt, axis=0, keepdims=True)
    carry_ref[...] = total
    cnt_ref[...] = jnp.broadcast_to(total, cnt_ref.shape).astype(jnp.int32)


def _moe_rank(idx_slab, tm=256):
    t = idx_slab.shape[0]
    return pl.pallas_call(
        functools.partial(_rank_kernel, tm=tm),
        out_shape=(jax.ShapeDtypeStruct((t, LANES), jnp.int32),
                   jax.ShapeDtypeStruct((8, N_EXPERTS), jnp.int32)),
        grid=(t // tm,),
        in_specs=[pl.BlockSpec((tm, LANES), lambda i: (i, 0))],
        out_specs=(pl.BlockSpec((tm, LANES), lambda i: (i, 0)),
                   pl.BlockSpec((8, N_EXPERTS), lambda i: (0, 0))),
        scratch_shapes=[pltpu.VMEM((1, N_EXPERTS), F32)],
        compiler_params=_cparams(("arbitrary",)),
        name="moe_rank",
    )(idx_slab)


def _expert_kernel(be_ref, nu_ref, nx_ref, tok_ref, hn_ref, wgu_ref, wdn_ref, bg_ref, bu_ref, bd_ref, o_ref,
                   buf_ref, act_ref, sgu_ref, sdn_ref, wg_ref, wu_ref, wd_ref, sem, wsem, *, unroll, chunk):
    i = pl.program_id(0)
    n_used = nu_ref[0]
    e = be_ref[i]
    f = wg_ref.shape[1]
    d = wd_ref.shape[1]

    def row_copy(tok, r, s):
        return pltpu.make_async_copy(hn_ref.at[tok], buf_ref.at[s, r], sem.at[s])

    def weight_copies(ex):
        return (pltpu.make_async_copy(wgu_ref.at[ex], sgu_ref, wsem.at[0]),
                pltpu.make_async_copy(wdn_ref.at[ex], sdn_ref, wsem.at[1]))

    def wait_rows(s):
        def wait(r, c):
            row_copy(0, 0, s).wait()
            return c

        lax.fori_loop(0, MOE_ROWS, wait, 0, unroll=unroll)

    @pl.when((i == 0) & (n_used > 0))
    def _():
        for cp in weight_copies(e):
            cp.start(priority=1)

        def body(r2, c):
            for j in range(2):
                row_copy(tok_ref[2 * r2 + j], 2 * r2 + j, 0).start()
            return c

        lax.fori_loop(0, MOE_ROWS // 2, body, 0, unroll=unroll)

    first = (i == 0) | (e != be_ref[jnp.maximum(i - 1, 0)])

    @pl.when((i < n_used) & first)
    def _():
        for cp in weight_copies(e):
            cp.wait()
        half = chunk // 2
        pr = lax.broadcasted_iota(jnp.int32, (chunk, chunk), 0)
        pc = lax.broadcasted_iota(jnp.int32, (chunk, chunk), 1)
        perm = jnp.where(pr == jnp.where(pc < half, 2 * pc, 2 * (pc - half) + 1), 1.0, 0.0).astype(BF16)
        for c in range(2 * f // chunk):
            sep = _dot(sgu_ref[:, c * chunk:(c + 1) * chunk].astype(BF16), perm).astype(BF16)
            wg_ref[:, c * half:(c + 1) * half] = sep[:, :half]
            wu_ref[:, c * half:(c + 1) * half] = sep[:, half:]
        wd_ref[...] = sdn_ref[...].astype(BF16)

        @pl.when(nx_ref[i] >= 0)
        def _():
            for cp in weight_copies(nx_ref[i]):
                cp.start(priority=1)

    @pl.when(i < n_used)
    def _():
        cur = i % 2
        nxt = 1 - cur
        base = jnp.minimum(i + 1, n_used - 1) * MOE_ROWS
        n_groups = f // chunk + d // chunk
        per = MOE_ROWS // n_groups

        def issue_group(k):
            for r in range(k * per, (k + 1) * per):
                row_copy(tok_ref[base + r], r, nxt).start()

        wait_rows(cur)
        x = jnp.concatenate([buf_ref[cur, :, a, :] for a in range(d // LANES)], axis=1).astype(BF16)
        for c in range(f // chunk):
            cols = slice(c * chunk, (c + 1) * chunk)
            g = jnp.minimum(_dot(x, wg_ref[:, cols]) + bg_ref[0, :, cols], SWIGLU_LIMIT)
            u = jnp.clip(_dot(x, wu_ref[:, cols]) + bu_ref[0, :, cols], -SWIGLU_LIMIT, SWIGLU_LIMIT)
            act_ref[:, cols] = ((u + 1.0) * g * jax.nn.sigmoid(SWIGLU_ALPHA * g)).astype(BF16)
            issue_group(c)
        act = act_ref[...]
        for c in range(d // chunk):
            cols = slice(c * chunk, (c + 1) * chunk)
            y = _dot(act, wd_ref[:, cols]) + bd_ref[0, :, cols]
            for a in range(chunk // LANES):
                o_ref[:, c * (chunk // LANES) + a, :] = y[:, a * LANES:(a + 1) * LANES]
            issue_group(f // chunk + c)

        @pl.when(i == n_used - 1)
        def _():
            wait_rows(nxt)

    @pl.when(i >= n_used)
    def _():
        o_ref[...] = jnp.zeros(o_ref.shape, o_ref.dtype)


def _moe_experts(block_exp, n_used, next_exp, slot_tok, hn, w_gate_up, w_down, bg, bu, bd):
    n_slots = slot_tok.shape[0]
    d = w_down.shape[2]
    tile = hn.shape[1:]
    f = w_down.shape[1]
    nblk = n_slots // MOE_ROWS
    bspec = lambda w: pl.BlockSpec((1, 1, w), lambda i, be, nu, nx, tk: (be[i], 0, 0))
    return pl.pallas_call(
        functools.partial(_expert_kernel, unroll=8, chunk=256),
        out_shape=jax.ShapeDtypeStruct((n_slots,) + tile, F32),
        grid_spec=pltpu.PrefetchScalarGridSpec(
            num_scalar_prefetch=4,
            grid=(nblk,),
            in_specs=[pl.BlockSpec(memory_space=pl.ANY), pl.BlockSpec(memory_space=pl.ANY),
                      pl.BlockSpec(memory_space=pl.ANY), bspec(f), bspec(f), bspec(d)],
            out_specs=pl.BlockSpec((MOE_ROWS,) + tile, lambda i, be, nu, nx, tk: (i, 0, 0)),
            scratch_shapes=[pltpu.VMEM((2, MOE_ROWS) + tile, hn.dtype), pltpu.VMEM((MOE_ROWS, f), BF16),
                            pltpu.VMEM((d, 2 * f), F32), pltpu.VMEM((f, d), F32),
                            pltpu.VMEM((d, f), BF16), pltpu.VMEM((d, f), BF16), pltpu.VMEM((f, d), BF16),
                            pltpu.SemaphoreType.DMA((2,)), pltpu.SemaphoreType.DMA((2,))]),
        compiler_params=_cparams(("arbitrary",)),
        name="moe_experts",
    )(block_exp, n_used, next_exp, slot_tok, hn, w_gate_up, w_down, bg, bu, bd)


def _combine_kernel(dest_ref, gate_ref, x1_ref, g_ref, ys_ref, o_ref, buf_ref, sem, *, tm, unroll):
    i = pl.program_id(0)
    n = pl.num_programs(0)

    def row_copy(slot_row, k, r, s):
        return pltpu.make_async_copy(ys_ref.at[slot_row], buf_ref.at[s, k, r], sem.at[s])

    def issue(step, s):
        base = step * (tm * TOP_K)
        for r in range(tm):
            for k in range(TOP_K):
                row_copy(dest_ref[base + r * TOP_K + k], k, r, s).start()

    @pl.when(i == 0)
    def _():
        issue(0, 0)

    @pl.when(i + 1 < n)
    def _():
        issue(i + 1, (i + 1) % 2)

    cur = i % 2

    def wait(r, c):
        row_copy(0, 0, 0, cur).wait()
        return c

    lax.fori_loop(0, tm * TOP_K, wait, 0, unroll=unroll)

    gate = gate_ref[...]
    parts = []
    for a in range(buf_ref.shape[3]):
        acc = gate[:, 0:1] * buf_ref[cur, 0, :, a, :]
        for k in range(1, TOP_K):
            acc = acc + gate[:, k:k + 1] * buf_ref[cur, k, :, a, :]
        parts.append(acc)
    y = x1_ref[...] + jnp.concatenate(parts, axis=1)
    ms = jnp.mean(y * y, axis=-1, keepdims=True)
    o_ref[...] = y * lax.rsqrt(ms + RMS_EPS) * g_ref[...]


def _moe_combine(dest_flat, gate_slab, x1, g_final, ys, tm=128):
    t, d = x1.shape
    return pl.pallas_call(
        functools.partial(_combine_kernel, tm=tm, unroll=8),
        out_shape=jax.ShapeDtypeStruct((t, d), F32),
        grid_spec=pltpu.PrefetchScalarGridSpec(
            num_scalar_prefetch=1,
            grid=(t // tm,),
            in_specs=[pl.BlockSpec((tm, LANES), lambda i, dst: (i, 0)),
                      pl.BlockSpec((tm, d), lambda i, dst: (i, 0)),
                      pl.BlockSpec((1, d), lambda i, dst: (0, 0)),
                      pl.BlockSpec(memory_space=pl.ANY)],
            out_specs=pl.BlockSpec((tm, d), lambda i, dst: (i, 0)),
            scratch_shapes=[pltpu.VMEM((2, TOP_K, tm) + ys.shape[1:], F32), pltpu.SemaphoreType.DMA((2,))]),
        compiler_params=_cparams(("arbitrary",)),
        name="moe_combine",
    )(dest_flat, gate_slab, x1, g_final, ys)


def _rope_tables(s):
    inv = ROPE_THETA ** (-jnp.arange(0, HEAD_DIM, 2, dtype=F32) / HEAD_DIM)
    ang = jnp.arange(s, dtype=F32)[:, None] * inv[None, :]
    cos, sin = jnp.cos(ang), jnp.sin(ang)
    return jnp.concatenate([cos, cos], axis=-1), jnp.concatenate([-sin, sin], axis=-1)


def _permute_w_in(w):
    d = w.shape[0]
    pad = N_PROJ - (C_SMALL + 32)
    cols = [w[:, 3360:6432], w[:, 0:512], w[:, 1304:1816], w[:, 1816:2328], w[:, 2328:2840],
            w[:, 2848:3360], w[:, 512:1280], w[:, 1280:1304], w[:, 2840:2848], jnp.zeros((d, pad), w.dtype)]
    return jnp.concatenate(cols, axis=1).astype(BF16)


def _layer(x, mem, g_mix, w_in, b_forget, b_merge, pe_k, w1_k, w2_k, pe_v, w1_v, w2_v, g_mem, w_mem_kv,
           w_br_nsa, w_br_fox, w_br_mem, w_out, g_ffn, w_router, b_router, w_gate_up, b_gate_up,
           w_down, b_down, g_final):
    b, s, d = x.shape
    t = b * s
    x2 = x.reshape(t, d)
    cos64, sin64 = _rope_tables(s)
    cos2 = jnp.concatenate([cos64, cos64], axis=-1)
    sin2 = jnp.concatenate([sin64, sin64], axis=-1)

    proj = _inproj(x2, g_mix.reshape(1, d), _permute_w_in(w_in), tm=min(1024, t))
    proj3 = proj.reshape(b, s, N_PROJ)
    b_row = jnp.zeros((1, LANES), F32).at[0, FL_OFF:FL_OFF + FOX_HEADS].set(b_forget)
    cum, cum_t = _fox_cum(proj3, b_row)

    nb = s // CMP_STRIDE
    wide = CMP_STRIDE * HEAD_DIM

    def to_blocks(c0):
        a = proj3[:, :, c0:c0 + NSA_GROUPS * HEAD_DIM].reshape(b, nb, CMP_STRIDE, NSA_GROUPS, HEAD_DIM)
        return a.transpose(0, 3, 1, 2, 4).reshape(b, NSA_GROUPS, nb, wide)

    kcmp, vcmp_t = _nsa_compress(
        to_blocks(C_NSAKV), to_blocks(C_NSAKV + LANES), cos64.reshape(nb, wide), sin64.reshape(nb, wide),
        pe_k.reshape(2, wide), pe_v.reshape(2, wide),
        w1_k.reshape(CMP_LEN * HEAD_DIM, HEAD_DIM).astype(BF16), w1_v.reshape(CMP_LEN * HEAD_DIM, HEAD_DIM).astype(BF16),
        w2_k.astype(BF16), w2_v.astype(BF16))
    fox_tile = min(FOX_TILE, s)
    kf, vft, ks, vst, kw, vwt = _attn_prep(proj3, cos2, sin2, fox_tile, NSA_TILE)
    o_nsa = _nsa_attend(proj3, cos2, sin2, kcmp, vcmp_t, ks, vst, kw, vwt, NSA_TILE)
    o_fox = _fox_attend(proj3, kf, vft, cum, cum_t, fox_tile)
    mem_kv = _mem_kv(mem, g_mem.reshape(1, d), w_mem_kv.astype(BF16))
    o_mem = _mem_attend(proj3, mem_kv)

    x1, hn, idx_slab, gate_slab = _merge_route(
        o_nsa.reshape(t, -1), o_fox.reshape(t, -1), o_mem.reshape(t, -1), proj, b_merge.reshape(1, -1),
        w_br_nsa.astype(BF16), w_br_fox.astype(BF16), w_br_mem.astype(BF16), w_out.astype(BF16),
        x2, g_ffn.reshape(1, d), w_router, b_router.reshape(1, -1))

    rank_slab, cnt8 = _moe_rank(idx_slab)
    counts = cnt8[0]
    padded = (counts + MOE_ROWS - 1) // MOE_ROWS * MOE_ROWS
    pad_end = jnp.cumsum(padded)
    pad_start = pad_end - padded
    n_assign = t * TOP_K
    nblk = -(-(n_assign + N_EXPERTS * (MOE_ROWS - 1)) // MOE_ROWS)
    n_slots = nblk * MOE_ROWS
    top_idx = idx_slab[:, :TOP_K]
    dest = (pad_start[top_idx] + rank_slab[:, :TOP_K]).reshape(-1).astype(jnp.int32)
    tok_flat = jnp.repeat(jnp.arange(t, dtype=jnp.int32), TOP_K)
    slot_tok = jnp.zeros((n_slots,), jnp.int32).at[dest].set(tok_flat, unique_indices=True)
    blk_start = jnp.arange(nblk, dtype=jnp.int32) * MOE_ROWS
    block_exp = jnp.minimum(jnp.sum(blk_start[:, None] >= pad_end[None, :], axis=1),
                            N_EXPERTS - 1).astype(jnp.int32)
    n_used = (pad_end[-1] // MOE_ROWS).astype(jnp.int32).reshape(1)

    e_ids = jnp.arange(N_EXPERTS, dtype=jnp.int32)
    later = (e_ids[None, :] > block_exp[:, None]) & (counts[None, :] > 0)
    next_exp = jnp.min(jnp.where(later, e_ids[None, :], N_EXPERTS), axis=1)
    next_exp = jnp.where(next_exp < N_EXPERTS, next_exp, -1).astype(jnp.int32)

    f = w_down.shape[1]
    bgu = b_gate_up.reshape(N_EXPERTS, 1, f, 2)
    ys = _moe_experts(block_exp, n_used, next_exp, slot_tok, hn, w_gate_up, w_down,
                      bgu[..., 0], bgu[..., 1], b_down.reshape(N_EXPERTS, 1, d))
    out = _moe_combine(dest, gate_slab, x1, g_final.reshape(1, d), ys)
    return out.reshape(b, s, d)


def kernel(x, mem, g_mix, w_in, b_forget, b_merge, nsa_pe_k, nsa_w1_k, nsa_w2_k, nsa_pe_v, nsa_w1_v, nsa_w2_v, g_mem, w_mem_kv, w_branch_nsa, w_branch_fox, w_branch_mem, w_out, g_ffn, w_router, b_router, w_gate_up, b_gate_up, w_down, b_down, g_final):
    assert g_mix.shape[0] == 1, "single-layer trunk"
    return _layer(x, mem, g_mix[0], w_in[0], b_forget[0], b_merge[0], nsa_pe_k[0], nsa_w1_k[0], nsa_w2_k[0],
                  nsa_pe_v[0], nsa_w1_v[0], nsa_w2_v[0], g_mem[0], w_mem_kv[0], w_branch_nsa[0],
                  w_branch_fox[0], w_branch_mem[0], w_out[0], g_ffn[0], w_router[0], b_router[0],
                  w_gate_up[0], b_gate_up[0], w_down[0], b_down[0], g_final)
```

```python
import functools

import jax
import jax.numpy as jnp
import numpy as np
from jax import lax
from jax.experimental import pallas as pl
from jax.experimental.pallas import tpu as pltpu

F32 = jnp.float32
BF16 = jnp.bfloat16

D_MODEL = 1024
HEAD_DIM = 64
ROPE_THETA = 10000.0
RMS_EPS = 1e-5
NEG_INF = -1e30
FORCE_SCORE = 1e4
NSA_HEADS = 8
NSA_GROUPS = 2
NSA_REP = NSA_HEADS // NSA_GROUPS
CMP_LEN = 32
CMP_STRIDE = 16
SEL_BLOCK = 64
SEL_TOPK = 16
WINDOW = 512
FOX_HEADS = 8
MEM_HEADS = 4
MEM_HEAD_DIM = 128
N_EXPERTS = 32
TOP_K = 4
SWIGLU_LIMIT = 7.0
SWIGLU_ALPHA = 1.702

LANES = 128
VMEM_LIMIT = 48 * 1024 * 1024

C_MERGE = 0
C_QA = 3072
C_QF = 3584
C_KF = 4096
C_VF = 4608
C_QM = 5120
C_NSAKV = 5632
C_SMALL = 6400
N_PROJ = 6656
GL_OFF = 0
FL_OFF = 24

MOE_ROWS = 256
NSA_TILE = 256
FOX_TILE = 512


def _cparams(sem, vmem=VMEM_LIMIT):
    return pltpu.CompilerParams(dimension_semantics=sem, vmem_limit_bytes=vmem)


def _dot(a, b):
    return jnp.dot(a, b, preferred_element_type=F32)


def _dot_nt(a, b):
    return lax.dot_general(a, b, (((1,), (1,)), ((), ())), preferred_element_type=F32)


def _rope(x, cos, sin_signed):
    w = x.shape[-1]
    lane = lax.broadcasted_iota(jnp.int32, x.shape, x.ndim - 1)
    first = (lane & (HEAD_DIM - 1)) < (HEAD_DIM // 2)
    rot = jnp.where(first, pltpu.roll(x, w - HEAD_DIM // 2, x.ndim - 1),
                    pltpu.roll(x, HEAD_DIM // 2, x.ndim - 1))
    return x * cos + rot * sin_signed


def _split3(x):
    hi = x.astype(BF16)
    r1 = x - hi.astype(F32)
    mid = r1.astype(BF16)
    lo = (r1 - mid.astype(F32)).astype(BF16)
    return hi, mid, lo


def _inproj_kernel(x_ref, g_ref, w_ref, o_ref, hn_ref):
    @pl.when(pl.program_id(1) == 0)
    def _():
        x = x_ref[...]
        ms = jnp.mean(x * x, axis=-1, keepdims=True)
        hn_ref[...] = (x * lax.rsqrt(ms + RMS_EPS) * g_ref[...]).astype(BF16)

    o_ref[...] = _dot(hn_ref[...], w_ref[...])


def _inproj(x2, g, w_bf16, tm=1024, tn=512):
    t, d = x2.shape
    n = w_bf16.shape[1]
    return pl.pallas_call(
        _inproj_kernel,
        out_shape=jax.ShapeDtypeStruct((t, n), F32),
        grid=(t // tm, n // tn),
        in_specs=[pl.BlockSpec((tm, d), lambda i, j: (i, 0)),
                  pl.BlockSpec((1, d), lambda i, j: (0, 0)),
                  pl.BlockSpec((d, tn), lambda i, j: (0, j))],
        out_specs=pl.BlockSpec((tm, tn), lambda i, j: (i, j)),
        scratch_shapes=[pltpu.VMEM((tm, d), BF16)],
        compiler_params=_cparams(("parallel", "arbitrary")),
        name="inproj",
    )(x2, g, w_bf16)


def _cum_kernel(s_ref, b_ref, c_ref, ct_ref, *, blk):
    s = s_ref.shape[1]
    z = s_ref[0] + b_ref[...]
    logf = jnp.minimum(z, 0.0) - jnp.log1p(jnp.exp(-jnp.abs(z)))
    r = lax.broadcasted_iota(jnp.int32, (blk, blk), 0)
    c = lax.broadcasted_iota(jnp.int32, (blk, blk), 1)
    tri = jnp.where(c <= r, 1.0, 0.0).astype(BF16)
    carry = jnp.zeros((1, LANES), F32)
    for i in range(s // blk):
        hi, mid, lo = _split3(logf[i * blk:(i + 1) * blk])
        loc = (_dot(tri, hi) + _dot(tri, mid)) + _dot(tri, lo)
        out = loc + carry
        c_ref[0, i * blk:(i + 1) * blk, :] = out
        carry = out[blk - 1:blk, :]
    ct_ref[0] = c_ref[0].T


def _fox_cum(proj3, b_row, blk=256):
    b, s, _ = proj3.shape
    return pl.pallas_call(
        functools.partial(_cum_kernel, blk=blk),
        out_shape=(jax.ShapeDtypeStruct((b, s, LANES), F32),
                   jax.ShapeDtypeStruct((b, LANES, s), F32)),
        grid=(b,),
        in_specs=[pl.BlockSpec((1, s, LANES), lambda i: (i, 0, C_SMALL // LANES)),
                  pl.BlockSpec((1, LANES), lambda i: (0, 0))],
        out_specs=(pl.BlockSpec((1, s, LANES), lambda i: (i, 0, 0)),
                   pl.BlockSpec((1, LANES, s), lambda i: (i, 0, 0))),
        compiler_params=_cparams(("parallel",)),
        name="fox_cum",
    )(proj3, b_row)


def _cmp_kernel(kc_ref, vc_ref, cos_ref, sin_ref, pek_ref, pev_ref, w1k_ref, w1v_ref, w2k_ref, w2v_ref,
                ko_ref, vto_ref, kr_ref):
    s = kc_ref.shape[1]
    nb = s // CMP_STRIDE
    kr_ref[...] = _rope(kc_ref[0], cos_ref[...], sin_ref[...])

    def mlp(rows, pe_ref, w1_ref, w2_ref):
        pa = jnp.zeros((nb, LANES), F32)
        pb = jnp.zeros((nb, LANES), F32)
        for l in range(CMP_STRIDE):
            x = rows(l)
            pa = pa + _dot((x + pe_ref[l:l + 1, :]).astype(BF16), w1_ref[l])
            pb = pb + _dot((x + pe_ref[CMP_STRIDE + l:CMP_STRIDE + l + 1, :]).astype(BF16), w1_ref[CMP_STRIDE + l])
        z = pa + pltpu.roll(pb, nb - 1, 0)
        h = z * jax.nn.sigmoid(z)
        return _dot(h.astype(BF16), w2_ref[...])

    ko_ref[0] = mlp(lambda l: kr_ref[pl.ds(l, nb, stride=CMP_STRIDE), :], pek_ref, w1k_ref, w2k_ref).astype(BF16)
    kr_ref[0:nb, :] = mlp(lambda l: vc_ref[0, pl.ds(l, nb, stride=CMP_STRIDE), :], pev_ref, w1v_ref, w2v_ref)
    vto_ref[0] = kr_ref[0:nb, :].T.astype(BF16)


def _nsa_compress(proj3, cos2, sin2, pek, pev, w1k, w1v, w2k, w2v):
    b, s, _ = proj3.shape
    nb = s // CMP_STRIDE
    kvb = C_NSAKV // LANES
    full = lambda a: pl.BlockSpec(a.shape, lambda i: (0,) * a.ndim)
    return pl.pallas_call(
        _cmp_kernel,
        out_shape=(jax.ShapeDtypeStruct((b, nb, LANES), BF16), jax.ShapeDtypeStruct((b, LANES, nb), BF16)),
        grid=(b,),
        in_specs=[pl.BlockSpec((1, s, LANES), lambda i: (i, 0, kvb)),
                  pl.BlockSpec((1, s, LANES), lambda i: (i, 0, kvb + 1)),
                  full(cos2), full(sin2), full(pek), full(pev), full(w1k), full(w1v), full(w2k), full(w2v)],
        out_specs=(pl.BlockSpec((1, nb, LANES), lambda i: (i, 0, 0)),
                   pl.BlockSpec((1, LANES, nb), lambda i: (i, 0, 0))),
        scratch_shapes=[pltpu.VMEM((s, LANES), F32)],
        compiler_params=_cparams(("parallel",)),
        name="nsa_compress",
    )(proj3, proj3, cos2, sin2, pek, pev, w1k, w1v, w2k, w2v)


def _prep_kernel(kf_ref, vf_ref, ks_ref, vs_ref, kw_ref, vw_ref, cos_ref, sin_ref,
                 okf_ref, ovf_ref, oks_ref, ovs_ref, okw_ref, ovw_ref):
    okf_ref[0] = kf_ref[0].astype(BF16)
    ovf_ref[0, 0] = vf_ref[0].T.astype(BF16)
    oks_ref[0] = _rope(ks_ref[0], cos_ref[...], sin_ref[...]).astype(BF16)
    okw_ref[0] = _rope(kw_ref[0], cos_ref[...], sin_ref[...]).astype(BF16)
    tn = ovs_ref.shape[3]
    for i in range(ovs_ref.shape[1]):
        ovs_ref[0, i] = vs_ref[0, i * tn:(i + 1) * tn, :].T.astype(BF16)
        ovw_ref[0, i] = vw_ref[0, i * tn:(i + 1) * tn, :].T.astype(BF16)


def _attn_prep(proj3, cos2, sin2, tk, tn):
    b, s, _ = proj3.shape
    wf = FOX_HEADS * HEAD_DIM
    kvb = C_NSAKV // LANES
    nk = s // tk
    sub = tk // tn
    col = lambda w, c: pl.BlockSpec((1, tk, w), lambda i, j: (i, j, c))
    rows = lambda w: pl.BlockSpec((1, tk, w), lambda i, j: (i, j, 0))
    ntile = pl.BlockSpec((1, sub, LANES, tn), lambda i, j: (i, j, 0, 0))
    return pl.pallas_call(
        _prep_kernel,
        out_shape=(jax.ShapeDtypeStruct((b, s, wf), BF16), jax.ShapeDtypeStruct((b, nk, wf, tk), BF16),
                   jax.ShapeDtypeStruct((b, s, LANES), BF16), jax.ShapeDtypeStruct((b, nk * sub, LANES, tn), BF16),
                   jax.ShapeDtypeStruct((b, s, LANES), BF16), jax.ShapeDtypeStruct((b, nk * sub, LANES, tn), BF16)),
        grid=(b, nk),
        in_specs=[col(wf, C_KF // wf), col(wf, C_VF // wf), col(LANES, kvb + 2), col(LANES, kvb + 3),
                  col(LANES, kvb + 4), col(LANES, kvb + 5),
                  pl.BlockSpec((tk, LANES), lambda i, j: (j, 0)), pl.BlockSpec((tk, LANES), lambda i, j: (j, 0))],
        out_specs=(rows(wf), pl.BlockSpec((1, 1, wf, tk), lambda i, j: (i, j, 0, 0)),
                   rows(LANES), ntile, rows(LANES), ntile),
        compiler_params=_cparams(("parallel", "parallel")),
        name="attn_prep",
    )(proj3, proj3, proj3, proj3, proj3, proj3, cos2, sin2)


def _online_update(s, vt, m_ref, l_ref, acc_ref, idx, mask=None):
    m_old = m_ref[idx]
    m_new = jnp.maximum(m_old, jnp.max(s, axis=0, keepdims=True))
    alpha = jnp.exp(m_old - m_new)
    p = jnp.exp(s - m_new)
    if mask is not None:
        p = jnp.where(mask, p, 0.0)
    l_ref[idx] = alpha * l_ref[idx] + jnp.sum(p, axis=0, keepdims=True)
    acc_ref[idx] = alpha * acc_ref[idx] + _dot(vt, p.astype(BF16))
    m_ref[idx] = m_new


def _padded_qt(q_ref, cos, sin, qt_ref, tq, heads_per_slot):
    qt_ref[...] = jnp.zeros(qt_ref.shape, qt_ref.dtype)
    n_heads = q_ref.shape[2] // HEAD_DIM
    for hp in range(n_heads // 2):
        q2 = q_ref[0, :, hp * LANES:(hp + 1) * LANES]
        if cos is not None:
            q2 = _rope(q2, cos, sin)
        qt = (q2 * (HEAD_DIM ** -0.5)).T.astype(BF16)
        for sub in range(2):
            h = 2 * hp + sub
            slot, r = h // heads_per_slot, h % heads_per_slot
            half = (slot % 2) if heads_per_slot > 1 else sub
            qt_ref[slot, half * HEAD_DIM:(half + 1) * HEAD_DIM, r * tq:(r + 1) * tq] = (
                qt[sub * HEAD_DIM:(sub + 1) * HEAD_DIM, :])


def _nsa_kernel(q_ref, cos_ref, sin_ref, kc_ref, vct_ref, ks_ref, vst_ref, kw_ref, vwt_ref, sm_ref, o_ref,
                qt_ref, oct_ref, m_ref, l_ref, acc_ref, *, tq, n_sel):
    qi = pl.program_id(1)
    tk = tq
    G, R = NSA_GROUPS, NSA_REP
    nb = kc_ref.shape[1]
    n_prev = WINDOW // tk
    _padded_qt(q_ref, cos_ref[...], sin_ref[...], qt_ref, tq, R)
    m_ref[...] = jnp.full(m_ref.shape, NEG_INF, F32)
    l_ref[...] = jnp.zeros(l_ref.shape, F32)
    acc_ref[...] = jnp.zeros(acc_ref.shape, F32)

    tpos = qi * tq + (lax.broadcasted_iota(jnp.int32, (nb, R * tq), 1) & (tq - 1))
    nrow = lax.broadcasted_iota(jnp.int32, (nb, R * tq), 0)
    cmask = (nrow * CMP_STRIDE + (CMP_LEN - 1) <= tpos) & (nrow < nb - 1)
    oj = lax.broadcasted_iota(jnp.int32, (n_sel, nb), 0)
    on = lax.broadcasted_iota(jnp.int32, (n_sel, nb), 1) * CMP_STRIDE
    overlap_t = jnp.where((on < (oj + 1) * SEL_BLOCK) & (on + CMP_LEN > oj * SEL_BLOCK), 1.0, 0.0).astype(BF16)
    blk = lax.broadcasted_iota(jnp.int32, (n_sel, tq), 0)
    trow = qi * tq + lax.broadcasted_iota(jnp.int32, (n_sel, tq), 1)
    cur = trow // SEL_BLOCK
    forced = (blk == 0) | (blk == cur) | (blk == cur - 1)
    future = blk * SEL_BLOCK > trow
    kc = kc_ref[0]
    for g in range(G):
        s = jnp.where(cmask, _dot(kc, qt_ref[g, 0:LANES, :]), NEG_INF)
        mx = jnp.max(s, axis=0, keepdims=True)
        e = jnp.where(cmask, jnp.exp(s - mx), 0.0)
        p = e / jnp.maximum(jnp.sum(e, axis=0, keepdims=True), 1e-30)
        oct_ref[g] = _dot(vct_ref[0, g * HEAD_DIM:(g + 1) * HEAD_DIM, :], p.astype(BF16))
        psum = p[:, 0:tq]
        for r in range(1, R):
            psum = psum + p[:, r * tq:(r + 1) * tq]
        hi, mid, lo = _split3(psum)
        imp = (_dot(overlap_t, hi) + _dot(overlap_t, mid)) + _dot(overlap_t, lo)
        imp = jnp.where(forced, FORCE_SCORE, jnp.where(future, -FORCE_SCORE, imp))
        rank = jnp.zeros((n_sel, tq), F32)
        for i in range(n_sel):
            ri = imp[i:i + 1, :]
            ahead = (ri > imp) | ((ri == imp) & (blk > i))
            rank = rank + jnp.where(ahead, 1.0, 0.0)
        bias = jnp.where(rank < float(min(SEL_TOPK, n_sel)), 0.0, NEG_INF).astype(BF16)
        for r in range(R):
            qt_ref[g, LANES:LANES + n_sel, r * tq:(r + 1) * tq] = bias

    qpos = qi * tq + (lax.broadcasted_iota(jnp.int32, (tk, R * tq), 1) & (tq - 1))
    krow = lax.broadcasted_iota(jnp.int32, (tk, R * tq), 0)
    er = lax.broadcasted_iota(jnp.int32, (tk, LANES), 0)
    ec = lax.broadcasted_iota(jnp.int32, (tk, LANES), 1)

    def sel_tile(j, diagonal):
        k = ks_ref[0, pl.ds(pl.multiple_of(j * tk, tk), tk), :]
        onehot = jnp.where((j * tk + er) // SEL_BLOCK == ec, 1.0, 0.0).astype(BF16)
        kx = jnp.concatenate([k, onehot], axis=1)
        for g in range(G):
            s = _dot(kx, qt_ref[g])
            if diagonal:
                s = jnp.where(j * tk + krow <= qpos, s, NEG_INF)
            _online_update(s, vst_ref[0, j, g * HEAD_DIM:(g + 1) * HEAD_DIM, :], m_ref, l_ref, acc_ref, g)

    def sel_body(j, c):
        sel_tile(j, False)
        return c

    lax.fori_loop(0, qi, sel_body, 0)
    sel_tile(qi, True)

    for back in range(n_prev, -1, -1):
        @pl.when(qi - back >= 0)
        def _(back=back):
            j = qi - back
            k = kw_ref[0, pl.ds(pl.multiple_of(j * tk, tk), tk), :]
            diff = qpos - (j * tk + krow)
            if back == 0:
                mask = diff >= 0
            elif back == n_prev:
                mask = diff < WINDOW
            else:
                mask = None
            for g in range(G):
                s = _dot(k, qt_ref[g, 0:LANES, :])
                if mask is not None:
                    s = jnp.where(mask, s, NEG_INF)
                _online_update(s, vwt_ref[0, j, g * HEAD_DIM:(g + 1) * HEAD_DIM, :], m_ref, l_ref, acc_ref,
                               G + g, mask if back == n_prev else None)

    gates_t = jax.nn.sigmoid(sm_ref[0]).T
    rows = []
    for h in range(NSA_HEADS):
        g, r = h // R, h % R
        cols = slice(r * tq, (r + 1) * tq)
        gate = lambda br: gates_t[GL_OFF + 3 * h + br:GL_OFF + 3 * h + br + 1, :]
        o_sel = acc_ref[g, :, cols] / jnp.maximum(l_ref[g, :, cols], 1e-30)
        o_win = acc_ref[G + g, :, cols] / jnp.maximum(l_ref[G + g, :, cols], 1e-30)
        rows.append(gate(0) * oct_ref[g, :, cols] + gate(1) * o_sel + gate(2) * o_win)
    o_ref[0] = jnp.concatenate(rows, axis=0).T


def _nsa_attend(proj3, cos2, sin2, kcmp, vcmp_t, ks, vst, kw, vwt, tq):
    b, s, _ = proj3.shape
    n_sel = s // SEL_BLOCK
    wq = NSA_HEADS * HEAD_DIM
    whole = lambda a: pl.BlockSpec((1,) + a.shape[1:], lambda i, j: (i,) + (0,) * (a.ndim - 1))
    return pl.pallas_call(
        functools.partial(_nsa_kernel, tq=tq, n_sel=n_sel),
        out_shape=jax.ShapeDtypeStruct((b, s, wq), F32),
        grid=(b, s // tq),
        in_specs=[pl.BlockSpec((1, tq, wq), lambda i, j: (i, j, C_QA // wq)),
                  pl.BlockSpec((tq, LANES), lambda i, j: (j, 0)),
                  pl.BlockSpec((tq, LANES), lambda i, j: (j, 0)),
                  whole(kcmp), whole(vcmp_t), whole(ks), whole(vst), whole(kw), whole(vwt),
                  pl.BlockSpec((1, tq, LANES), lambda i, j: (i, j, C_SMALL // LANES))],
        out_specs=pl.BlockSpec((1, tq, wq), lambda i, j: (i, j, 0)),
        scratch_shapes=[pltpu.VMEM((NSA_GROUPS, 2 * LANES, NSA_REP * tq), BF16),
                        pltpu.VMEM((NSA_GROUPS, HEAD_DIM, NSA_REP * tq), F32),
                        pltpu.VMEM((2 * NSA_GROUPS, 1, NSA_REP * tq), F32),
                        pltpu.VMEM((2 * NSA_GROUPS, 1, NSA_REP * tq), F32),
                        pltpu.VMEM((2 * NSA_GROUPS, HEAD_DIM, NSA_REP * tq), F32)],
        compiler_params=_cparams(("parallel", "parallel")),
        name="nsa_attend",
    )(proj3, cos2, sin2, kcmp, vcmp_t, ks, vst, kw, vwt, proj3)


def _fox_kernel(q_ref, k_ref, vt_ref, cq_ref, ck_ref, o_ref, qt_ref, m_ref, l_ref, acc_ref, *, tq):
    qi = pl.program_id(1)
    tk = tq
    _padded_qt(q_ref, None, None, qt_ref, tq, 1)
    m_ref[...] = jnp.full(m_ref.shape, NEG_INF, F32)
    l_ref[...] = jnp.zeros(l_ref.shape, F32)
    acc_ref[...] = jnp.zeros(acc_ref.shape, F32)
    cq = cq_ref[0]
    causal = lax.broadcasted_iota(jnp.int32, (tk, tq), 0) <= lax.broadcasted_iota(jnp.int32, (tk, tq), 1)

    def tile(j, diagonal):
        rows = pl.ds(pl.multiple_of(j * tk, tk), tk)
        ck = ck_ref[0, rows, :]
        for h in range(FOX_HEADS):
            hp = h // 2
            s = _dot(k_ref[0, rows, hp * LANES:(hp + 1) * LANES], qt_ref[h])
            s = (s - ck[:, FL_OFF + h:FL_OFF + h + 1]) + cq[FL_OFF + h:FL_OFF + h + 1, :]
            if diagonal:
                s = jnp.where(causal, s, NEG_INF)
            _online_update(s, vt_ref[0, j, h * HEAD_DIM:(h + 1) * HEAD_DIM, :], m_ref, l_ref, acc_ref, h)

    def body(j, c):
        tile(j, False)
        return c

    lax.fori_loop(0, qi, body, 0)
    tile(qi, True)
    rows = [acc_ref[h] / jnp.maximum(l_ref[h], 1e-30) for h in range(FOX_HEADS)]
    o_ref[0] = jnp.concatenate(rows, axis=0).T


def _fox_attend(proj3, kf, vft, cum, cum_t, tq):
    b, s, _ = proj3.shape
    w = FOX_HEADS * HEAD_DIM
    whole = lambda a: pl.BlockSpec((1,) + a.shape[1:], lambda i, j: (i,) + (0,) * (a.ndim - 1))
    return pl.pallas_call(
        functools.partial(_fox_kernel, tq=tq),
        out_shape=jax.ShapeDtypeStruct((b, s, w), F32),
        grid=(b, s // tq),
        in_specs=[pl.BlockSpec((1, tq, w), lambda i, j: (i, j, C_QF // w)),
                  whole(kf), whole(vft),
                  pl.BlockSpec((1, LANES, tq), lambda i, j: (i, 0, j)),
                  whole(cum)],
        out_specs=pl.BlockSpec((1, tq, w), lambda i, j: (i, j, 0)),
        scratch_shapes=[pltpu.VMEM((FOX_HEADS, LANES, tq), BF16),
                        pltpu.VMEM((FOX_HEADS, 1, tq), F32),
                        pltpu.VMEM((FOX_HEADS, 1, tq), F32),
                        pltpu.VMEM((FOX_HEADS, HEAD_DIM, tq), F32)],
        compiler_params=_cparams(("parallel", "parallel")),
        name="fox_attend",
    )(proj3, kf, vft, cum_t, cum)


def _memkv_kernel(m_ref, g_ref, w_ref, o_ref):
    x = m_ref[0]
    ms = jnp.mean(x * x, axis=-1, keepdims=True)
    hn = (x * lax.rsqrt(ms + RMS_EPS) * g_ref[...]).astype(BF16)
    o_ref[0] = _dot(hn, w_ref[...]).astype(BF16)


def _mem_kv(mem, g, w_bf16):
    b, m, d = mem.shape
    n = w_bf16.shape[1]
    return pl.pallas_call(
        _memkv_kernel,
        out_shape=jax.ShapeDtypeStruct((b, m, n), BF16),
        grid=(b,),
        in_specs=[pl.BlockSpec((1, m, d), lambda i: (i, 0, 0)),
                  pl.BlockSpec((1, d), lambda i: (0, 0)),
                  pl.BlockSpec((d, n), lambda i: (0, 0))],
        out_specs=pl.BlockSpec((1, m, n), lambda i: (i, 0, 0)),
        compiler_params=_cparams(("parallel",)),
        name="mem_kv",
    )(mem, g, w_bf16)


def _memattn_kernel(q_ref, kv_ref, o_ref):
    w = MEM_HEADS * MEM_HEAD_DIM
    for h in range(MEM_HEADS):
        lo, hi = h * MEM_HEAD_DIM, (h + 1) * MEM_HEAD_DIM
        s = _dot_nt(q_ref[0, :, lo:hi].astype(BF16), kv_ref[0, :, lo:hi]) * (MEM_HEAD_DIM ** -0.5)
        e = jnp.exp(s - jnp.max(s, axis=-1, keepdims=True))
        p = e / jnp.sum(e, axis=-1, keepdims=True)
        o_ref[0, :, lo:hi] = _dot(p.astype(BF16), kv_ref[0, :, w + lo:w + hi])


def _mem_attend(proj3, kv, tq=512):
    b, s, _ = proj3.shape
    m = kv.shape[1]
    w = MEM_HEADS * MEM_HEAD_DIM
    return pl.pallas_call(
        _memattn_kernel,
        out_shape=jax.ShapeDtypeStruct((b, s, w), F32),
        grid=(b, s // tq),
        in_specs=[pl.BlockSpec((1, tq, w), lambda i, j: (i, j, C_QM // w)),
                  pl.BlockSpec((1, m, 2 * w), lambda i, j: (i, 0, 0))],
        out_specs=pl.BlockSpec((1, tq, w), lambda i, j: (i, j, 0)),
        compiler_params=_cparams(("parallel", "parallel")),
        name="mem_attend",
    )(proj3, kv)


def _merge_kernel(on_ref, of_ref, om_ref, ln_ref, lf_ref, lm_ref, bm_ref, wn_ref, wf_ref, wm_ref,
                  wo_ref, x_ref, g_ref, wr_ref, br_ref, x1_ref, hn_ref, idx_ref, gate_ref):
    d = x_ref.shape[1]

    def branch(o_ref, l_ref, w_ref, k):
        gate = jax.nn.sigmoid(l_ref[...] + bm_ref[:, k * d:(k + 1) * d])
        return gate * _dot(o_ref[...].astype(BF16), w_ref[...])

    merged = branch(on_ref, ln_ref, wn_ref, 0) + branch(of_ref, lf_ref, wf_ref, 1) + branch(om_ref, lm_ref, wm_ref, 2)
    x1 = x_ref[...] + _dot(merged.astype(BF16), wo_ref[...])
    x1_ref[...] = x1
    ms = jnp.mean(x1 * x1, axis=-1, keepdims=True)
    hn = x1 * lax.rsqrt(ms + RMS_EPS) * g_ref[...]
    hn_ref[...] = hn
    h_hi = hn.astype(BF16)
    h_lo = (hn - h_hi.astype(F32)).astype(BF16)
    wr = wr_ref[...]
    w_hi = wr.astype(BF16)
    w_lo = (wr - w_hi.astype(F32)).astype(BF16)
    logits = (_dot(h_hi, w_hi) + (_dot(h_lo, w_hi) + _dot(h_hi, w_lo))) + br_ref[...]
    tm, ne = logits.shape
    lane_e = lax.broadcasted_iota(jnp.int32, (tm, ne), 1)
    lane = lax.broadcasted_iota(jnp.int32, (tm, LANES), 1)
    idx_slab = jnp.zeros((tm, LANES), jnp.int32)
    val_slab = jnp.zeros((tm, LANES), F32)
    work = logits
    vals = []
    for k in range(TOP_K):
        m = jnp.max(work, axis=-1, keepdims=True)
        idx = jnp.min(jnp.where(work == m, lane_e, ne), axis=-1, keepdims=True)
        work = jnp.where(lane_e == idx, -jnp.inf, work)
        idx_slab = jnp.where(lane == k, idx, idx_slab)
        vals.append(m)
    es = [jnp.exp(v - vals[0]) for v in vals]
    tot = es[0]
    for e in es[1:]:
        tot = tot + e
    for k in range(TOP_K):
        val_slab = jnp.where(lane == k, es[k] / tot, val_slab)
    idx_ref[...] = idx_slab
    gate_ref[...] = val_slab


def _merge_route(o_nsa, o_fox, o_mem, proj, b_merge, wn, wf, wm, wo, x2, g_ffn, w_router, b_router, tm=512):
    t, d = x2.shape
    wb = o_nsa.shape[1]
    row = lambda w: pl.BlockSpec((tm, w), lambda i: (i, 0))
    full = lambda a: pl.BlockSpec(a.shape, lambda i: (0,) * a.ndim)
    return pl.pallas_call(
        _merge_kernel,
        out_shape=(jax.ShapeDtypeStruct((t, d), F32), jax.ShapeDtypeStruct((t, d), F32),
                   jax.ShapeDtypeStruct((t, LANES), jnp.int32), jax.ShapeDtypeStruct((t, LANES), F32)),
        grid=(t // tm,),
        in_specs=[row(wb), row(wb), row(wb),
                  pl.BlockSpec((tm, d), lambda i: (i, 0)),
                  pl.BlockSpec((tm, d), lambda i: (i, 1)),
                  pl.BlockSpec((tm, d), lambda i: (i, 2)),
                  full(b_merge), full(wn), full(wf), full(wm), full(wo),
                  row(d), full(g_ffn), full(w_router), full(b_router)],
        out_specs=(row(d), row(d), row(LANES), row(LANES)),
        compiler_params=_cparams(("parallel",)),
        name="merge_route",
    )(o_nsa, o_fox, o_mem, proj, proj, proj, b_merge, wn, wf, wm, wo, x2, g_ffn, w_router, b_router)


def _rank_kernel(idx_ref, rank_ref, cnt_ref, carry_ref, *, tm):
    @pl.when(pl.program_id(0) == 0)
    def _():
        carry_ref[...] = jnp.zeros(carry_ref.shape, F32)

    idx = idx_ref[...]
    lane_e = lax.broadcasted_iota(jnp.int32, (tm, N_EXPERTS), 1)
    hots = [jnp.where(idx[:, k:k + 1] == lane_e, 1.0, 0.0) for k in range(TOP_K)]
    cnt = hots[0]
    for hk in hots[1:]:
        cnt = cnt + hk
    r = lax.broadcasted_iota(jnp.int32, (tm, tm), 0)
    c = lax.broadcasted_iota(jnp.int32, (tm, tm), 1)
    strict = jnp.where(c < r, 1.0, 0.0).astype(BF16)
    before = _dot(strict, cnt.astype(BF16)) + carry_ref[...]
    lane = lax.broadcasted_iota(jnp.int32, (tm, LANES), 1)
    slab = jnp.zeros((tm, LANES), F32)
    for k in range(TOP_K):
        slab = jnp.where(lane == k, jnp.sum(hots[k] * before, axis=-1, keepdims=True), slab)
    rank_ref[...] = slab.astype(jnp.int32)
    total = carry_ref[...] + jnp.sum(cnt, axis=0, keepdims=True)
    carry_ref[...] = total
    cnt_ref[...] = jnp.broadcast_to(total, cnt_ref.shape).astype(jnp.int32)


def _moe_rank(idx_slab, tm=256):
    t = idx_slab.shape[0]
    return pl.pallas_call(
        functools.partial(_rank_kernel, tm=tm),
        out_shape=(jax.ShapeDtypeStruct((t, LANES), jnp.int32),
                   jax.ShapeDtypeStruct((8, N_EXPERTS), jnp.int32)),
        grid=(t // tm,),
        in_specs=[pl.BlockSpec((tm, LANES), lambda i: (i, 0))],
        out_specs=(pl.BlockSpec((tm, LANES), lambda i: (i, 0)),
                   pl.BlockSpec((8, N_EXPERTS), lambda i: (0, 0))),
        scratch_shapes=[pltpu.VMEM((1, N_EXPERTS), F32)],
        compiler_params=_cparams(("arbitrary",)),
        name="moe_rank",
    )(idx_slab)


def _expert_kernel(be_ref, nu_ref, nx_ref, tok_ref, hn_ref, wgu_ref, wdn_ref, bg_ref, bu_ref, bd_ref, o_ref,
                   buf_ref, act_ref, sgu_ref, sdn_ref, wg_ref, wu_ref, wd_ref, sem, wsem, *, unroll, chunk):
    i = pl.program_id(0)
    n_used = nu_ref[0]
    e = be_ref[i]
    f = wg_ref.shape[1]
    d = wd_ref.shape[1]

    def row_copy(tok, r, s):
        return pltpu.make_async_copy(hn_ref.at[pl.ds(tok, 1)], buf_ref.at[s, pl.ds(r, 1)], sem.at[s])

    def weight_copies(ex):
        return (pltpu.make_async_copy(wgu_ref.at[ex], sgu_ref, wsem.at[0]),
                pltpu.make_async_copy(wdn_ref.at[ex], sdn_ref, wsem.at[1]))

    def wait_rows(s):
        def wait(r, c):
            row_copy(0, 0, s).wait()
            return c

        lax.fori_loop(0, MOE_ROWS, wait, 0, unroll=unroll)

    @pl.when((i == 0) & (n_used > 0))
    def _():
        for cp in weight_copies(e):
            cp.start(priority=1)

        def body(r2, c):
            for j in range(2):
                row_copy(tok_ref[2 * r2 + j], 2 * r2 + j, 0).start(priority=j)
            return c

        lax.fori_loop(0, MOE_ROWS // 2, body, 0, unroll=unroll)

    first = (i == 0) | (e != be_ref[jnp.maximum(i - 1, 0)])

    @pl.when((i < n_used) & first)
    def _():
        for cp in weight_copies(e):
            cp.wait()
        half = chunk // 2
        pr = lax.broadcasted_iota(jnp.int32, (chunk, chunk), 0)
        pc = lax.broadcasted_iota(jnp.int32, (chunk, chunk), 1)
        perm = jnp.where(pr == jnp.where(pc < half, 2 * pc, 2 * (pc - half) + 1), 1.0, 0.0).astype(BF16)
        for c in range(2 * f // chunk):
            sep = _dot(sgu_ref[:, c * chunk:(c + 1) * chunk].astype(BF16), perm).astype(BF16)
            wg_ref[:, c * half:(c + 1) * half] = sep[:, :half]
            wu_ref[:, c * half:(c + 1) * half] = sep[:, half:]
        wd_ref[...] = sdn_ref[...].astype(BF16)

        @pl.when(nx_ref[i] >= 0)
        def _():
            for cp in weight_copies(nx_ref[i]):
                cp.start(priority=1)

    @pl.when(i < n_used)
    def _():
        cur = i % 2
        nxt = 1 - cur
        base = jnp.minimum(i + 1, n_used - 1) * MOE_ROWS
        n_groups = f // chunk + d // chunk
        per = MOE_ROWS // n_groups

        def issue_group(k):
            for r in range(k * per, (k + 1) * per):
                row_copy(tok_ref[base + r], r, nxt).start(priority=r % 2)

        wait_rows(cur)
        x = buf_ref[cur].astype(BF16)
        for c in range(f // chunk):
            cols = slice(c * chunk, (c + 1) * chunk)
            g = jnp.minimum(_dot(x, wg_ref[:, cols]) + bg_ref[0, :, cols], SWIGLU_LIMIT)
            u = jnp.clip(_dot(x, wu_ref[:, cols]) + bu_ref[0, :, cols], -SWIGLU_LIMIT, SWIGLU_LIMIT)
            act_ref[:, cols] = ((u + 1.0) * g * jax.nn.sigmoid(SWIGLU_ALPHA * g)).astype(BF16)
            issue_group(c)
        act = act_ref[...]
        for c in range(d // chunk):
            cols = slice(c * chunk, (c + 1) * chunk)
            o_ref[:, cols] = _dot(act, wd_ref[:, cols]) + bd_ref[0, :, cols]
            issue_group(f // chunk + c)

        @pl.when(i == n_used - 1)
        def _():
            wait_rows(nxt)

    @pl.when(i >= n_used)
    def _():
        o_ref[...] = jnp.zeros(o_ref.shape, o_ref.dtype)


def _moe_experts(block_exp, n_used, next_exp, slot_tok, hn, w_gate_up, w_down, bg, bu, bd):
    n_slots = slot_tok.shape[0]
    d = hn.shape[1]
    f = w_down.shape[1]
    nblk = n_slots // MOE_ROWS
    bspec = lambda w: pl.BlockSpec((1, 1, w), lambda i, be, nu, nx, tk: (be[i], 0, 0))
    return pl.pallas_call(
        functools.partial(_expert_kernel, unroll=8, chunk=256),
        out_shape=jax.ShapeDtypeStruct((n_slots, d), F32),
        grid_spec=pltpu.PrefetchScalarGridSpec(
            num_scalar_prefetch=4,
            grid=(nblk,),
            in_specs=[pl.BlockSpec(memory_space=pl.ANY), pl.BlockSpec(memory_space=pl.ANY),
                      pl.BlockSpec(memory_space=pl.ANY), bspec(f), bspec(f), bspec(d)],
            out_specs=pl.BlockSpec((MOE_ROWS, d), lambda i, be, nu, nx, tk: (i, 0)),
            scratch_shapes=[pltpu.VMEM((2, MOE_ROWS, d), hn.dtype), pltpu.VMEM((MOE_ROWS, f), BF16),
                            pltpu.VMEM((d, 2 * f), F32), pltpu.VMEM((f, d), F32),
                            pltpu.VMEM((d, f), BF16), pltpu.VMEM((d, f), BF16), pltpu.VMEM((f, d), BF16),
                            pltpu.SemaphoreType.DMA((2,)), pltpu.SemaphoreType.DMA((2,))]),
        compiler_params=_cparams(("arbitrary",)),
        name="moe_experts",
    )(block_exp, n_used, next_exp, slot_tok, hn, w_gate_up, w_down, bg, bu, bd)


def _combine_kernel(dest_ref, gate_ref, x1_ref, g_ref, ys_ref, o_ref, buf_ref, sem, *, tm, unroll):
    i = pl.program_id(0)
    n = pl.num_programs(0)

    def row_copy(slot_row, k, r, s):
        return pltpu.make_async_copy(ys_ref.at[pl.ds(slot_row, 1)], buf_ref.at[s, k, pl.ds(r, 1)], sem.at[s])

    def issue(step, s):
        base = step * (tm * TOP_K)
        for r in range(tm):
            for k in range(TOP_K):
                row_copy(dest_ref[base + r * TOP_K + k], k, r, s).start(priority=k % 2)

    @pl.when(i == 0)
    def _():
        issue(0, 0)

    @pl.when(i + 1 < n)
    def _():
        issue(i + 1, (i + 1) % 2)

    cur = i % 2

    def wait(r, c):
        row_copy(0, 0, 0, cur).wait()
        return c

    lax.fori_loop(0, tm * TOP_K, wait, 0, unroll=unroll)

    gate = gate_ref[...]
    y = x1_ref[...]
    for k in range(TOP_K):
        y = y + gate[:, k:k + 1] * buf_ref[cur, k]
    ms = jnp.mean(y * y, axis=-1, keepdims=True)
    o_ref[...] = y * lax.rsqrt(ms + RMS_EPS) * g_ref[...]


def _moe_combine(dest_flat, gate_slab, x1, g_final, ys, tm=128):
    t, d = x1.shape
    return pl.pallas_call(
        functools.partial(_combine_kernel, tm=tm, unroll=8),
        out_shape=jax.ShapeDtypeStruct((t, d), F32),
        grid_spec=pltpu.PrefetchScalarGridSpec(
            num_scalar_prefetch=1,
            grid=(t // tm,),
            in_specs=[pl.BlockSpec((tm, LANES), lambda i, dst: (i, 0)),
                      pl.BlockSpec((tm, d), lambda i, dst: (i, 0)),
                      pl.BlockSpec((1, d), lambda i, dst: (0, 0)),
                      pl.BlockSpec(memory_space=pl.ANY)],
            out_specs=pl.BlockSpec((tm, d), lambda i, dst: (i, 0)),
            scratch_shapes=[pltpu.VMEM((2, TOP_K, tm, d), F32), pltpu.SemaphoreType.DMA((2,))]),
        compiler_params=_cparams(("arbitrary",)),
        name="moe_combine",
    )(dest_flat, gate_slab, x1, g_final, ys)


def _rope_tables(s):
    inv = ROPE_THETA ** (-jnp.arange(0, HEAD_DIM, 2, dtype=F32) / HEAD_DIM)
    ang = jnp.arange(s, dtype=F32)[:, None] * inv[None, :]
    cos, sin = jnp.cos(ang), jnp.sin(ang)
    return jnp.concatenate([cos, cos], axis=-1), jnp.concatenate([-sin, sin], axis=-1)


def _permute_w_in(w):
    d = w.shape[0]
    pad = N_PROJ - (C_SMALL + 32)
    cols = [w[:, 3360:6432], w[:, 0:512], w[:, 1304:1816], w[:, 1816:2328], w[:, 2328:2840],
            w[:, 2848:3360], w[:, 512:1280], w[:, 1280:1304], w[:, 2840:2848], jnp.zeros((d, pad), w.dtype)]
    return jnp.concatenate(cols, axis=1).astype(BF16)


def _layer(x, mem, g_mix, w_in, b_forget, b_merge, pe_k, w1_k, w2_k, pe_v, w1_v, w2_v, g_mem, w_mem_kv,
           w_br_nsa, w_br_fox, w_br_mem, w_out, g_ffn, w_router, b_router, w_gate_up, b_gate_up,
           w_down, b_down, g_final):
    b, s, d = x.shape
    t = b * s
    x2 = x.reshape(t, d)
    cos64, sin64 = _rope_tables(s)
    cos2 = jnp.concatenate([cos64, cos64], axis=-1)
    sin2 = jnp.concatenate([sin64, sin64], axis=-1)

    proj = _inproj(x2, g_mix.reshape(1, d), _permute_w_in(w_in), tm=min(1024, t))
    proj3 = proj.reshape(b, s, N_PROJ)
    b_row = jnp.zeros((1, LANES), F32).at[0, FL_OFF:FL_OFF + FOX_HEADS].set(b_forget)
    cum, cum_t = _fox_cum(proj3, b_row)

    def both_groups(w):
        z = jnp.zeros_like(w)
        return jnp.concatenate([jnp.concatenate([w, z], axis=-1), jnp.concatenate([z, w], axis=-1)], axis=-2).astype(BF16)

    kcmp, vcmp_t = _nsa_compress(
        proj3, cos2, sin2, jnp.concatenate([pe_k, pe_k], axis=-1), jnp.concatenate([pe_v, pe_v], axis=-1),
        both_groups(w1_k), both_groups(w1_v), both_groups(w2_k), both_groups(w2_v))
    fox_tile = min(FOX_TILE, s)
    kf, vft, ks, vst, kw, vwt = _attn_prep(proj3, cos2, sin2, fox_tile, NSA_TILE)
    o_nsa = _nsa_attend(proj3, cos2, sin2, kcmp, vcmp_t, ks, vst, kw, vwt, NSA_TILE)
    o_fox = _fox_attend(proj3, kf, vft, cum, cum_t, fox_tile)
    mem_kv = _mem_kv(mem, g_mem.reshape(1, d), w_mem_kv.astype(BF16))
    o_mem = _mem_attend(proj3, mem_kv)

    x1, hn, idx_slab, gate_slab = _merge_route(
        o_nsa.reshape(t, -1), o_fox.reshape(t, -1), o_mem.reshape(t, -1), proj, b_merge.reshape(1, -1),
        w_br_nsa.astype(BF16), w_br_fox.astype(BF16), w_br_mem.astype(BF16), w_out.astype(BF16),
        x2, g_ffn.reshape(1, d), w_router, b_router.reshape(1, -1))

    rank_slab, cnt8 = _moe_rank(idx_slab)
    counts = cnt8[0]
    padded = (counts + MOE_ROWS - 1) // MOE_ROWS * MOE_ROWS
    pad_end = jnp.cumsum(padded)
    pad_start = pad_end - padded
    n_assign = t * TOP_K
    nblk = -(-(n_assign + N_EXPERTS * (MOE_ROWS - 1)) // MOE_ROWS)
    n_slots = nblk * MOE_ROWS
    top_idx = idx_slab[:, :TOP_K]
    dest = (pad_start[top_idx] + rank_slab[:, :TOP_K]).reshape(-1).astype(jnp.int32)
    tok_flat = jnp.repeat(jnp.arange(t, dtype=jnp.int32), TOP_K)
    slot_tok = jnp.zeros((n_slots,), jnp.int32).at[dest].set(tok_flat, unique_indices=True)
    blk_start = jnp.arange(nblk, dtype=jnp.int32) * MOE_ROWS
    block_exp = jnp.minimum(jnp.sum(blk_start[:, None] >= pad_end[None, :], axis=1),
                            N_EXPERTS - 1).astype(jnp.int32)
    n_used = (pad_end[-1] // MOE_ROWS).astype(jnp.int32).reshape(1)

    e_ids = jnp.arange(N_EXPERTS, dtype=jnp.int32)
    later = (e_ids[None, :] > block_exp[:, None]) & (counts[None, :] > 0)
    next_exp = jnp.min(jnp.where(later, e_ids[None, :], N_EXPERTS), axis=1)
    next_exp = jnp.where(next_exp < N_EXPERTS, next_exp, -1).astype(jnp.int32)

    f = w_down.shape[1]
    bgu = b_gate_up.reshape(N_EXPERTS, 1, f, 2)
    ys = _moe_experts(block_exp, n_used, next_exp, slot_tok, hn, w_gate_up, w_down,
                      bgu[..., 0], bgu[..., 1], b_down.reshape(N_EXPERTS, 1, d))
    out = _moe_combine(dest, gate_slab, x1, g_final.reshape(1, d), ys)
    return out.reshape(b, s, d)


def kernel(x, mem, g_mix, w_in, b_forget, b_merge, nsa_pe_k, nsa_w1_k, nsa_w2_k, nsa_pe_v, nsa_w1_v, nsa_w2_v, g_mem, w_mem_kv, w_branch_nsa, w_branch_fox, w_branch_mem, w_out, g_ffn, w_router, b_router, w_gate_up, b_gate_up, w_down, b_down, g_final):
    assert g_mix.shape[0] == 1, "single-layer trunk"
    return _layer(x, mem, g_mix[0], w_in[0], b_forget[0], b_merge[0], nsa_pe_k[0], nsa_w1_k[0], nsa_w2_k[0],
                  nsa_pe_v[0], nsa_w1_v[0], nsa_w2_v[0], g_mem[0], w_mem_kv[0], w_branch_nsa[0],
                  w_branch_fox[0], w_branch_mem[0], w_out[0], g_ffn[0], w_router[0], b_router[0],
                  w_gate_up[0], b_gate_up[0], w_down[0], b_down[0], g_final)
```

```python
import functools

import jax
import jax.numpy as jnp
import numpy as np
from jax import lax
from jax.experimental import pallas as pl
from jax.experimental.pallas import tpu as pltpu

F32 = jnp.float32
BF16 = jnp.bfloat16

D_MODEL = 1024
HEAD_DIM = 64
ROPE_THETA = 10000.0
RMS_EPS = 1e-5
NEG_INF = -1e30
FORCE_SCORE = 1e4
NSA_HEADS = 8
NSA_GROUPS = 2
NSA_REP = NSA_HEADS // NSA_GROUPS
CMP_LEN = 32
CMP_STRIDE = 16
SEL_BLOCK = 64
SEL_TOPK = 16
WINDOW = 512
FOX_HEADS = 8
MEM_HEADS = 4
MEM_HEAD_DIM = 128
N_EXPERTS = 32
TOP_K = 4
SWIGLU_LIMIT = 7.0
SWIGLU_ALPHA = 1.702

LANES = 128
VMEM_LIMIT = 48 * 1024 * 1024

C_MERGE = 0
C_QA = 3072
C_QF = 3584
C_KF = 4096
C_VF = 4608
C_QM = 5120
C_NSAKV = 5632
C_SMALL = 6400
N_PROJ = 6656
GL_OFF = 0
FL_OFF = 24

MOE_ROWS = 256
NSA_Q_TILE = 512
NSA_K_TILE = 256
FOX_TILE = 512


def _cparams(sem, vmem=VMEM_LIMIT):
    return pltpu.CompilerParams(dimension_semantics=sem, vmem_limit_bytes=vmem)


def _dot(a, b):
    return jnp.dot(a, b, preferred_element_type=F32)


def _dot_nt(a, b):
    return lax.dot_general(a, b, (((1,), (1,)), ((), ())), preferred_element_type=F32)


def _rope(x, cos, sin_signed):
    w = x.shape[-1]
    lane = lax.broadcasted_iota(jnp.int32, x.shape, x.ndim - 1)
    first = (lane & (HEAD_DIM - 1)) < (HEAD_DIM // 2)
    rot = jnp.where(first, pltpu.roll(x, w - HEAD_DIM // 2, x.ndim - 1),
                    pltpu.roll(x, HEAD_DIM // 2, x.ndim - 1))
    return x * cos + rot * sin_signed


def _split3(x):
    hi = x.astype(BF16)
    r1 = x - hi.astype(F32)
    mid = r1.astype(BF16)
    lo = (r1 - mid.astype(F32)).astype(BF16)
    return hi, mid, lo


def _inproj_kernel(x_ref, g_ref, w_ref, o_ref, hn_ref):
    @pl.when(pl.program_id(1) == 0)
    def _():
        x = x_ref[...]
        ms = jnp.mean(x * x, axis=-1, keepdims=True)
        hn_ref[...] = (x * lax.rsqrt(ms + RMS_EPS) * g_ref[...]).astype(BF16)

    o_ref[...] = _dot(hn_ref[...], w_ref[...])


def _inproj(x2, g, w_bf16, tm=1024, tn=512):
    t, d = x2.shape
    n = w_bf16.shape[1]
    return pl.pallas_call(
        _inproj_kernel,
        out_shape=jax.ShapeDtypeStruct((t, n), F32),
        grid=(t // tm, n // tn),
        in_specs=[pl.BlockSpec((tm, d), lambda i, j: (i, 0)),
                  pl.BlockSpec((1, d), lambda i, j: (0, 0)),
                  pl.BlockSpec((d, tn), lambda i, j: (0, j))],
        out_specs=pl.BlockSpec((tm, tn), lambda i, j: (i, j)),
        scratch_shapes=[pltpu.VMEM((tm, d), BF16)],
        compiler_params=_cparams(("parallel", "arbitrary")),
        name="inproj",
    )(x2, g, w_bf16)


def _cum_kernel(s_ref, b_ref, c_ref, ct_ref, *, blk):
    s = s_ref.shape[1]
    z = s_ref[0] + b_ref[...]
    logf = jnp.minimum(z, 0.0) - jnp.log1p(jnp.exp(-jnp.abs(z)))
    r = lax.broadcasted_iota(jnp.int32, (blk, blk), 0)
    c = lax.broadcasted_iota(jnp.int32, (blk, blk), 1)
    tri = jnp.where(c <= r, 1.0, 0.0).astype(BF16)
    carry = jnp.zeros((1, LANES), F32)
    for i in range(s // blk):
        hi, mid, lo = _split3(logf[i * blk:(i + 1) * blk])
        loc = (_dot(tri, hi) + _dot(tri, mid)) + _dot(tri, lo)
        out = loc + carry
        c_ref[0, i * blk:(i + 1) * blk, :] = out
        carry = out[blk - 1:blk, :]
    ct_ref[0] = c_ref[0].T


def _fox_cum(proj3, b_row, blk=256):
    b, s, _ = proj3.shape
    return pl.pallas_call(
        functools.partial(_cum_kernel, blk=blk),
        out_shape=(jax.ShapeDtypeStruct((b, s, LANES), F32),
                   jax.ShapeDtypeStruct((b, LANES, s), F32)),
        grid=(b,),
        in_specs=[pl.BlockSpec((1, s, LANES), lambda i: (i, 0, C_SMALL // LANES)),
                  pl.BlockSpec((1, LANES), lambda i: (0, 0))],
        out_specs=(pl.BlockSpec((1, s, LANES), lambda i: (i, 0, 0)),
                   pl.BlockSpec((1, LANES, s), lambda i: (i, 0, 0))),
        compiler_params=_cparams(("parallel",)),
        name="fox_cum",
    )(proj3, b_row)


def _cmp_kernel(kc_ref, vc_ref, cos_ref, sin_ref, pek_ref, pev_ref, w1k_ref, w1v_ref, w2k_ref, w2v_ref,
                ko_ref, vto_ref, kr_ref):
    s = kc_ref.shape[1]
    nb = s // CMP_STRIDE
    kr_ref[...] = _rope(kc_ref[0], cos_ref[...], sin_ref[...])

    def mlp(rows, pe_ref, w1_ref, w2_ref):
        pa = jnp.zeros((nb, LANES), F32)
        pb = jnp.zeros((nb, LANES), F32)
        for l in range(CMP_STRIDE):
            x = rows(l)
            pa = pa + _dot((x + pe_ref[l:l + 1, :]).astype(BF16), w1_ref[l])
            pb = pb + _dot((x + pe_ref[CMP_STRIDE + l:CMP_STRIDE + l + 1, :]).astype(BF16), w1_ref[CMP_STRIDE + l])
        z = pa + pltpu.roll(pb, nb - 1, 0)
        h = z * jax.nn.sigmoid(z)
        return _dot(h.astype(BF16), w2_ref[...])

    ko_ref[0] = mlp(lambda l: kr_ref[pl.ds(l, nb, stride=CMP_STRIDE), :], pek_ref, w1k_ref, w2k_ref).astype(BF16)
    kr_ref[0:nb, :] = mlp(lambda l: vc_ref[0, pl.ds(l, nb, stride=CMP_STRIDE), :], pev_ref, w1v_ref, w2v_ref)
    vto_ref[0] = kr_ref[0:nb, :].T.astype(BF16)


def _nsa_compress(proj3, cos2, sin2, pek, pev, w1k, w1v, w2k, w2v):
    b, s, _ = proj3.shape
    nb = s // CMP_STRIDE
    kvb = C_NSAKV // LANES
    full = lambda a: pl.BlockSpec(a.shape, lambda i: (0,) * a.ndim)
    return pl.pallas_call(
        _cmp_kernel,
        out_shape=(jax.ShapeDtypeStruct((b, nb, LANES), BF16), jax.ShapeDtypeStruct((b, LANES, nb), BF16)),
        grid=(b,),
        in_specs=[pl.BlockSpec((1, s, LANES), lambda i: (i, 0, kvb)),
                  pl.BlockSpec((1, s, LANES), lambda i: (i, 0, kvb + 1)),
                  full(cos2), full(sin2), full(pek), full(pev), full(w1k), full(w1v), full(w2k), full(w2v)],
        out_specs=(pl.BlockSpec((1, nb, LANES), lambda i: (i, 0, 0)),
                   pl.BlockSpec((1, LANES, nb), lambda i: (i, 0, 0))),
        scratch_shapes=[pltpu.VMEM((s, LANES), F32)],
        compiler_params=_cparams(("parallel",)),
        name="nsa_compress",
    )(proj3, proj3, cos2, sin2, pek, pev, w1k, w1v, w2k, w2v)


def _prep_kernel(kf_ref, vf_ref, ks_ref, vs_ref, kw_ref, vw_ref, cos_ref, sin_ref,
                 okf_ref, ovf_ref, oks_ref, ovs_ref, okw_ref, ovw_ref):
    okf_ref[0] = kf_ref[0].astype(BF16)
    ovf_ref[0, 0] = vf_ref[0].T.astype(BF16)
    oks_ref[0] = _rope(ks_ref[0], cos_ref[...], sin_ref[...]).astype(BF16)
    okw_ref[0] = _rope(kw_ref[0], cos_ref[...], sin_ref[...]).astype(BF16)
    tn = ovs_ref.shape[3]
    for i in range(ovs_ref.shape[1]):
        ovs_ref[0, i] = vs_ref[0, i * tn:(i + 1) * tn, :].T.astype(BF16)
        ovw_ref[0, i] = vw_ref[0, i * tn:(i + 1) * tn, :].T.astype(BF16)


def _attn_prep(proj3, cos2, sin2, tk, tn):
    b, s, _ = proj3.shape
    wf = FOX_HEADS * HEAD_DIM
    kvb = C_NSAKV // LANES
    nk = s // tk
    sub = tk // tn
    col = lambda w, c: pl.BlockSpec((1, tk, w), lambda i, j: (i, j, c))
    rows = lambda w: pl.BlockSpec((1, tk, w), lambda i, j: (i, j, 0))
    ntile = pl.BlockSpec((1, sub, LANES, tn), lambda i, j: (i, j, 0, 0))
    return pl.pallas_call(
        _prep_kernel,
        out_shape=(jax.ShapeDtypeStruct((b, s, wf), BF16), jax.ShapeDtypeStruct((b, nk, wf, tk), BF16),
                   jax.ShapeDtypeStruct((b, s, LANES), BF16), jax.ShapeDtypeStruct((b, nk * sub, LANES, tn), BF16),
                   jax.ShapeDtypeStruct((b, s, LANES), BF16), jax.ShapeDtypeStruct((b, nk * sub, LANES, tn), BF16)),
        grid=(b, nk),
        in_specs=[col(wf, C_KF // wf), col(wf, C_VF // wf), col(LANES, kvb + 2), col(LANES, kvb + 3),
                  col(LANES, kvb + 4), col(LANES, kvb + 5),
                  pl.BlockSpec((tk, LANES), lambda i, j: (j, 0)), pl.BlockSpec((tk, LANES), lambda i, j: (j, 0))],
        out_specs=(rows(wf), pl.BlockSpec((1, 1, wf, tk), lambda i, j: (i, j, 0, 0)),
                   rows(LANES), ntile, rows(LANES), ntile),
        compiler_params=_cparams(("parallel", "parallel")),
        name="attn_prep",
    )(proj3, proj3, proj3, proj3, proj3, proj3, cos2, sin2)


def _online_update(s, vt, m_ref, l_ref, acc_ref, idx, mask=None):
    m_old = m_ref[idx]
    m_new = jnp.maximum(m_old, jnp.max(s, axis=0, keepdims=True))
    alpha = jnp.exp(m_old - m_new)
    p = jnp.exp(s - m_new)
    if mask is not None:
        p = jnp.where(mask, p, 0.0)
    l_ref[idx] = alpha * l_ref[idx] + jnp.sum(p, axis=0, keepdims=True)
    acc_ref[idx] = alpha * acc_ref[idx] + _dot(vt, p.astype(BF16))
    m_ref[idx] = m_new


def _padded_qt(q_ref, cos, sin, qt_ref, tq, heads_per_slot):
    qt_ref[...] = jnp.zeros(qt_ref.shape, qt_ref.dtype)
    n_heads = q_ref.shape[2] // HEAD_DIM
    for hp in range(n_heads // 2):
        q2 = q_ref[0, :, hp * LANES:(hp + 1) * LANES]
        if cos is not None:
            q2 = _rope(q2, cos, sin)
        qt = (q2 * (HEAD_DIM ** -0.5)).T.astype(BF16)
        for sub in range(2):
            h = 2 * hp + sub
            slot, r = h // heads_per_slot, h % heads_per_slot
            half = (slot % 2) if heads_per_slot > 1 else sub
            qt_ref[slot, half * HEAD_DIM:(half + 1) * HEAD_DIM, r * tq:(r + 1) * tq] = (
                qt[sub * HEAD_DIM:(sub + 1) * HEAD_DIM, :])


def _nsa_kernel(q_ref, cos_ref, sin_ref, kc_ref, vct_ref, ks_ref, vst_ref, kw_ref, vwt_ref, sm_ref, o_ref,
                qt_ref, oct_ref, m_ref, l_ref, acc_ref, *, tq, tk, n_sel):
    qi = pl.program_id(1)
    G, R = NSA_GROUPS, NSA_REP
    nb = kc_ref.shape[1]
    per_q = tq // tk
    _padded_qt(q_ref, cos_ref[...], sin_ref[...], qt_ref, tq, R)
    m_ref[...] = jnp.full(m_ref.shape, NEG_INF, F32)
    l_ref[...] = jnp.zeros(l_ref.shape, F32)
    acc_ref[...] = jnp.zeros(acc_ref.shape, F32)

    tpos = qi * tq + (lax.broadcasted_iota(jnp.int32, (nb, R * tq), 1) & (tq - 1))
    nrow = lax.broadcasted_iota(jnp.int32, (nb, R * tq), 0)
    cmask = (nrow * CMP_STRIDE + (CMP_LEN - 1) <= tpos) & (nrow < nb - 1)
    oj = lax.broadcasted_iota(jnp.int32, (n_sel, nb), 0)
    on = lax.broadcasted_iota(jnp.int32, (n_sel, nb), 1) * CMP_STRIDE
    overlap_t = jnp.where((on < (oj + 1) * SEL_BLOCK) & (on + CMP_LEN > oj * SEL_BLOCK), 1.0, 0.0).astype(BF16)
    blk = lax.broadcasted_iota(jnp.int32, (n_sel, tq), 0)
    trow = qi * tq + lax.broadcasted_iota(jnp.int32, (n_sel, tq), 1)
    cur = trow // SEL_BLOCK
    forced = (blk == 0) | (blk == cur) | (blk == cur - 1)
    future = blk * SEL_BLOCK > trow
    kc = kc_ref[0]
    for g in range(G):
        s = jnp.where(cmask, _dot(kc, qt_ref[g, 0:LANES, :]), NEG_INF)
        mx = jnp.max(s, axis=0, keepdims=True)
        e = jnp.where(cmask, jnp.exp(s - mx), 0.0)
        p = e / jnp.maximum(jnp.sum(e, axis=0, keepdims=True), 1e-30)
        oct_ref[g] = _dot(vct_ref[0, g * HEAD_DIM:(g + 1) * HEAD_DIM, :], p.astype(BF16))
        psum = p[:, 0:tq]
        for r in range(1, R):
            psum = psum + p[:, r * tq:(r + 1) * tq]
        hi, mid, lo = _split3(psum)
        imp = (_dot(overlap_t, hi) + _dot(overlap_t, mid)) + _dot(overlap_t, lo)
        imp = jnp.where(forced, FORCE_SCORE, jnp.where(future, -FORCE_SCORE, imp))
        rank = jnp.zeros((n_sel, tq), F32)
        for i in range(n_sel):
            ri = imp[i:i + 1, :]
            ahead = (ri > imp) | ((ri == imp) & (blk > i))
            rank = rank + jnp.where(ahead, 1.0, 0.0)
        bias = jnp.where(rank < float(min(SEL_TOPK, n_sel)), 0.0, NEG_INF).astype(BF16)
        for r in range(R):
            qt_ref[g, LANES:LANES + n_sel, r * tq:(r + 1) * tq] = bias

    qpos = qi * tq + (lax.broadcasted_iota(jnp.int32, (tk, R * tq), 1) & (tq - 1))
    krow = lax.broadcasted_iota(jnp.int32, (tk, R * tq), 0)
    er = lax.broadcasted_iota(jnp.int32, (tk, LANES), 0)
    ec = lax.broadcasted_iota(jnp.int32, (tk, LANES), 1)

    def sel_tile(j, diagonal):
        k = ks_ref[0, pl.ds(pl.multiple_of(j * tk, tk), tk), :]
        onehot = jnp.where((j * tk + er) // SEL_BLOCK == ec, 1.0, 0.0).astype(BF16)
        kx = jnp.concatenate([k, onehot], axis=1)
        for g in range(G):
            s = _dot(kx, qt_ref[g])
            if diagonal:
                s = jnp.where(j * tk + krow <= qpos, s, NEG_INF)
            _online_update(s, vst_ref[0, j, g * HEAD_DIM:(g + 1) * HEAD_DIM, :], m_ref, l_ref, acc_ref, g)

    def sel_body(j, c):
        sel_tile(j, False)
        return c

    lax.fori_loop(0, qi * per_q, sel_body, 0)
    for dd in range(per_q):
        sel_tile(qi * per_q + dd, True)

    for back in range(per_q + WINDOW // tk - 1, -1, -1):
        @pl.when((qi + 1) * per_q - 1 - back >= 0)
        def _(back=back):
            j = (qi + 1) * per_q - 1 - back
            k = kw_ref[0, pl.ds(pl.multiple_of(j * tk, tk), tk), :]
            diff = qpos - (j * tk + krow)
            mask = (diff >= 0) & (diff < WINDOW)
            for g in range(G):
                s = jnp.where(mask, _dot(k, qt_ref[g, 0:LANES, :]), NEG_INF)
                _online_update(s, vwt_ref[0, j, g * HEAD_DIM:(g + 1) * HEAD_DIM, :], m_ref, l_ref, acc_ref,
                               G + g, mask)

    gates_t = jax.nn.sigmoid(sm_ref[0]).T
    rows = []
    for h in range(NSA_HEADS):
        g, r = h // R, h % R
        cols = slice(r * tq, (r + 1) * tq)
        gate = lambda br: gates_t[GL_OFF + 3 * h + br:GL_OFF + 3 * h + br + 1, :]
        o_sel = acc_ref[g, :, cols] / jnp.maximum(l_ref[g, :, cols], 1e-30)
        o_win = acc_ref[G + g, :, cols] / jnp.maximum(l_ref[G + g, :, cols], 1e-30)
        rows.append(gate(0) * oct_ref[g, :, cols] + gate(1) * o_sel + gate(2) * o_win)
    o_ref[0] = jnp.concatenate(rows, axis=0).T


def _nsa_attend(proj3, cos2, sin2, kcmp, vcmp_t, ks, vst, kw, vwt, tq, tk):
    b, s, _ = proj3.shape
    n_sel = s // SEL_BLOCK
    wq = NSA_HEADS * HEAD_DIM
    whole = lambda a: pl.BlockSpec((1,) + a.shape[1:], lambda i, j: (i,) + (0,) * (a.ndim - 1))
    return pl.pallas_call(
        functools.partial(_nsa_kernel, tq=tq, tk=tk, n_sel=n_sel),
        out_shape=jax.ShapeDtypeStruct((b, s, wq), F32),
        grid=(b, s // tq),
        in_specs=[pl.BlockSpec((1, tq, wq), lambda i, j: (i, j, C_QA // wq)),
                  pl.BlockSpec((tq, LANES), lambda i, j: (j, 0)),
                  pl.BlockSpec((tq, LANES), lambda i, j: (j, 0)),
                  whole(kcmp), whole(vcmp_t), whole(ks), whole(vst), whole(kw), whole(vwt),
                  pl.BlockSpec((1, tq, LANES), lambda i, j: (i, j, C_SMALL // LANES))],
        out_specs=pl.BlockSpec((1, tq, wq), lambda i, j: (i, j, 0)),
        scratch_shapes=[pltpu.VMEM((NSA_GROUPS, 2 * LANES, NSA_REP * tq), BF16),
                        pltpu.VMEM((NSA_GROUPS, HEAD_DIM, NSA_REP * tq), F32),
                        pltpu.VMEM((2 * NSA_GROUPS, 1, NSA_REP * tq), F32),
                        pltpu.VMEM((2 * NSA_GROUPS, 1, NSA_REP * tq), F32),
                        pltpu.VMEM((2 * NSA_GROUPS, HEAD_DIM, NSA_REP * tq), F32)],
        compiler_params=_cparams(("parallel", "parallel")),
        name="nsa_attend",
    )(proj3, cos2, sin2, kcmp, vcmp_t, ks, vst, kw, vwt, proj3)


def _fox_kernel(q_ref, k_ref, vt_ref, cq_ref, ck_ref, o_ref, qt_ref, m_ref, l_ref, acc_ref, *, tq):
    qi = pl.program_id(1)
    tk = tq
    _padded_qt(q_ref, None, None, qt_ref, tq, 1)
    m_ref[...] = jnp.full(m_ref.shape, NEG_INF, F32)
    l_ref[...] = jnp.zeros(l_ref.shape, F32)
    acc_ref[...] = jnp.zeros(acc_ref.shape, F32)
    cq = cq_ref[0]
    causal = lax.broadcasted_iota(jnp.int32, (tk, tq), 0) <= lax.broadcasted_iota(jnp.int32, (tk, tq), 1)

    def tile(j, diagonal):
        rows = pl.ds(pl.multiple_of(j * tk, tk), tk)
        ck = ck_ref[0, rows, :]
        for h in range(FOX_HEADS):
            hp = h // 2
            s = _dot(k_ref[0, rows, hp * LANES:(hp + 1) * LANES], qt_ref[h])
            s = (s - ck[:, FL_OFF + h:FL_OFF + h + 1]) + cq[FL_OFF + h:FL_OFF + h + 1, :]
            if diagonal:
                s = jnp.where(causal, s, NEG_INF)
            _online_update(s, vt_ref[0, j, h * HEAD_DIM:(h + 1) * HEAD_DIM, :], m_ref, l_ref, acc_ref, h)

    def body(j, c):
        tile(j, False)
        return c

    lax.fori_loop(0, qi, body, 0)
    tile(qi, True)
    rows = [acc_ref[h] / jnp.maximum(l_ref[h], 1e-30) for h in range(FOX_HEADS)]
    o_ref[0] = jnp.concatenate(rows, axis=0).T


def _fox_attend(proj3, kf, vft, cum, cum_t, tq):
    b, s, _ = proj3.shape
    w = FOX_HEADS * HEAD_DIM
    whole = lambda a: pl.BlockSpec((1,) + a.shape[1:], lambda i, j: (i,) + (0,) * (a.ndim - 1))
    return pl.pallas_call(
        functools.partial(_fox_kernel, tq=tq),
        out_shape=jax.ShapeDtypeStruct((b, s, w), F32),
        grid=(b, s // tq),
        in_specs=[pl.BlockSpec((1, tq, w), lambda i, j: (i, j, C_QF // w)),
                  whole(kf), whole(vft),
                  pl.BlockSpec((1, LANES, tq), lambda i, j: (i, 0, j)),
                  whole(cum)],
        out_specs=pl.BlockSpec((1, tq, w), lambda i, j: (i, j, 0)),
        scratch_shapes=[pltpu.VMEM((FOX_HEADS, LANES, tq), BF16),
                        pltpu.VMEM((FOX_HEADS, 1, tq), F32),
                        pltpu.VMEM((FOX_HEADS, 1, tq), F32),
                        pltpu.VMEM((FOX_HEADS, HEAD_DIM, tq), F32)],
        compiler_params=_cparams(("parallel", "parallel")),
        name="fox_attend",
    )(proj3, kf, vft, cum_t, cum)


def _memkv_kernel(m_ref, g_ref, w_ref, o_ref):
    x = m_ref[0]
    ms = jnp.mean(x * x, axis=-1, keepdims=True)
    hn = (x * lax.rsqrt(ms + RMS_EPS) * g_ref[...]).astype(BF16)
    o_ref[0] = _dot(hn, w_ref[...]).astype(BF16)


def _mem_kv(mem, g, w_bf16):
    b, m, d = mem.shape
    n = w_bf16.shape[1]
    return pl.pallas_call(
        _memkv_kernel,
        out_shape=jax.ShapeDtypeStruct((b, m, n), BF16),
        grid=(b,),
        in_specs=[pl.BlockSpec((1, m, d), lambda i: (i, 0, 0)),
                  pl.BlockSpec((1, d), lambda i: (0, 0)),
                  pl.BlockSpec((d, n), lambda i: (0, 0))],
        out_specs=pl.BlockSpec((1, m, n), lambda i: (i, 0, 0)),
        compiler_params=_cparams(("parallel",)),
        name="mem_kv",
    )(mem, g, w_bf16)


def _memattn_kernel(q_ref, kv_ref, o_ref):
    w = MEM_HEADS * MEM_HEAD_DIM
    for h in range(MEM_HEADS):
        lo, hi = h * MEM_HEAD_DIM, (h + 1) * MEM_HEAD_DIM
        s = _dot_nt(q_ref[0, :, lo:hi].astype(BF16), kv_ref[0, :, lo:hi]) * (MEM_HEAD_DIM ** -0.5)
        e = jnp.exp(s - jnp.max(s, axis=-1, keepdims=True))
        p = e / jnp.sum(e, axis=-1, keepdims=True)
        o_ref[0, :, lo:hi] = _dot(p.astype(BF16), kv_ref[0, :, w + lo:w + hi])


def _mem_attend(proj3, kv, tq=512):
    b, s, _ = proj3.shape
    m = kv.shape[1]
    w = MEM_HEADS * MEM_HEAD_DIM
    return pl.pallas_call(
        _memattn_kernel,
        out_shape=jax.ShapeDtypeStruct((b, s, w), F32),
        grid=(b, s // tq),
        in_specs=[pl.BlockSpec((1, tq, w), lambda i, j: (i, j, C_QM // w)),
                  pl.BlockSpec((1, m, 2 * w), lambda i, j: (i, 0, 0))],
        out_specs=pl.BlockSpec((1, tq, w), lambda i, j: (i, j, 0)),
        compiler_params=_cparams(("parallel", "parallel")),
        name="mem_attend",
    )(proj3, kv)


def _merge_kernel(on_ref, of_ref, om_ref, ln_ref, lf_ref, lm_ref, bm_ref, wn_ref, wf_ref, wm_ref,
                  wo_ref, x_ref, g_ref, wr_ref, br_ref, x1_ref, hn_ref, idx_ref, gate_ref):
    d = x_ref.shape[1]

    def branch(o_ref, l_ref, w_ref, k):
        gate = jax.nn.sigmoid(l_ref[...] + bm_ref[:, k * d:(k + 1) * d])
        return gate * _dot(o_ref[...].astype(BF16), w_ref[...])

    merged = branch(on_ref, ln_ref, wn_ref, 0) + branch(of_ref, lf_ref, wf_ref, 1) + branch(om_ref, lm_ref, wm_ref, 2)
    x1 = x_ref[...] + _dot(merged.astype(BF16), wo_ref[...])
    x1_ref[...] = x1
    ms = jnp.mean(x1 * x1, axis=-1, keepdims=True)
    hn = x1 * lax.rsqrt(ms + RMS_EPS) * g_ref[...]
    hn_ref[...] = hn
    h_hi = hn.astype(BF16)
    h_lo = (hn - h_hi.astype(F32)).astype(BF16)
    wr = wr_ref[...]
    w_hi = wr.astype(BF16)
    w_lo = (wr - w_hi.astype(F32)).astype(BF16)
    logits = (_dot(h_hi, w_hi) + (_dot(h_lo, w_hi) + _dot(h_hi, w_lo))) + br_ref[...]
    tm, ne = logits.shape
    lane_e = lax.broadcasted_iota(jnp.int32, (tm, ne), 1)
    lane = lax.broadcasted_iota(jnp.int32, (tm, LANES), 1)
    idx_slab = jnp.zeros((tm, LANES), jnp.int32)
    val_slab = jnp.zeros((tm, LANES), F32)
    work = logits
    vals = []
    for k in range(TOP_K):
        m = jnp.max(work, axis=-1, keepdims=True)
        idx = jnp.min(jnp.where(work == m, lane_e, ne), axis=-1, keepdims=True)
        work = jnp.where(lane_e == idx, -jnp.inf, work)
        idx_slab = jnp.where(lane == k, idx, idx_slab)
        vals.append(m)
    es = [jnp.exp(v - vals[0]) for v in vals]
    tot = es[0]
    for e in es[1:]:
        tot = tot + e
    for k in range(TOP_K):
        val_slab = jnp.where(lane == k, es[k] / tot, val_slab)
    idx_ref[...] = idx_slab
    gate_ref[...] = val_slab


def _merge_route(o_nsa, o_fox, o_mem, proj, b_merge, wn, wf, wm, wo, x2, g_ffn, w_router, b_router, tm=512):
    t, d = x2.shape
    wb = o_nsa.shape[1]
    row = lambda w: pl.BlockSpec((tm, w), lambda i: (i, 0))
    full = lambda a: pl.BlockSpec(a.shape, lambda i: (0,) * a.ndim)
    return pl.pallas_call(
        _merge_kernel,
        out_shape=(jax.ShapeDtypeStruct((t, d), F32), jax.ShapeDtypeStruct((t, d), F32),
                   jax.ShapeDtypeStruct((t, LANES), jnp.int32), jax.ShapeDtypeStruct((t, LANES), F32)),
        grid=(t // tm,),
        in_specs=[row(wb), row(wb), row(wb),
                  pl.BlockSpec((tm, d), lambda i: (i, 0)),
                  pl.BlockSpec((tm, d), lambda i: (i, 1)),
                  pl.BlockSpec((tm, d), lambda i: (i, 2)),
                  full(b_merge), full(wn), full(wf), full(wm), full(wo),
                  row(d), full(g_ffn), full(w_router), full(b_router)],
        out_specs=(row(d), row(d), row(LANES), row(LANES)),
        compiler_params=_cparams(("parallel",)),
        name="merge_route",
    )(o_nsa, o_fox, o_mem, proj, proj, proj, b_merge, wn, wf, wm, wo, x2, g_ffn, w_router, b_router)


def _rank_kernel(idx_ref, rank_ref, cnt_ref, carry_ref, *, tm):
    @pl.when(pl.program_id(0) == 0)
    def _():
        carry_ref[...] = jnp.zeros(carry_ref.shape, F32)

    idx = idx_ref[...]
    lane_e = lax.broadcasted_iota(jnp.int32, (tm, N_EXPERTS), 1)
    hots = [jnp.where(idx[:, k:k + 1] == lane_e, 1.0, 0.0) for k in range(TOP_K)]
    cnt = hots[0]
    for hk in hots[1:]:
        cnt = cnt + hk
    r = lax.broadcasted_iota(jnp.int32, (tm, tm), 0)
    c = lax.broadcasted_iota(jnp.int32, (tm, tm), 1)
    strict = jnp.where(c < r, 1.0, 0.0).astype(BF16)
    before = _dot(strict, cnt.astype(BF16)) + carry_ref[...]
    lane = lax.broadcasted_iota(jnp.int32, (tm, LANES), 1)
    slab = jnp.zeros((tm, LANES), F32)
    for k in range(TOP_K):
        slab = jnp.where(lane == k, jnp.sum(hots[k] * before, axis=-1, keepdims=True), slab)
    rank_ref[...] = slab.astype(jnp.int32)
    total = carry_ref[...] + jnp.sum(cnt, axis=0, keepdims=True)
    carry_ref[...] = total
    cnt_ref[...] = jnp.broadcast_to(total, cnt_ref.shape).astype(jnp.int32)


def _moe_rank(idx_slab, tm=256):
    t = idx_slab.shape[0]
    return pl.pallas_call(
        functools.partial(_rank_kernel, tm=tm),
        out_shape=(jax.ShapeDtypeStruct((t, LANES), jnp.int32),
                   jax.ShapeDtypeStruct((8, N_EXPERTS), jnp.int32)),
        grid=(t // tm,),
        in_specs=[pl.BlockSpec((tm, LANES), lambda i: (i, 0))],
        out_specs=(pl.BlockSpec((tm, LANES), lambda i: (i, 0)),
                   pl.BlockSpec((8, N_EXPERTS), lambda i: (0, 0))),
        scratch_shapes=[pltpu.VMEM((1, N_EXPERTS), F32)],
        compiler_params=_cparams(("arbitrary",)),
        name="moe_rank",
    )(idx_slab)


def _slot_tok_kernel(dest_ref, o_ref, *, unroll):
    shift = TOP_K.bit_length() - 1
    assert TOP_K == 1 << shift

    def clear(i, c):
        o_ref[i] = 0
        return c

    lax.fori_loop(0, o_ref.shape[0], clear, 0, unroll=unroll)

    def put(a, c):
        o_ref[dest_ref[a]] = lax.shift_right_logical(a, shift)
        return c

    lax.fori_loop(0, dest_ref.shape[0], put, 0, unroll=unroll)


def _moe_slot_tok(dest_flat, n_slots):
    return pl.pallas_call(
        functools.partial(_slot_tok_kernel, unroll=8),
        out_shape=jax.ShapeDtypeStruct((n_slots,), jnp.int32),
        grid_spec=pltpu.PrefetchScalarGridSpec(
            num_scalar_prefetch=1, grid=(1,), in_specs=[],
            out_specs=pl.BlockSpec(memory_space=pltpu.SMEM)),
        compiler_params=_cparams(("arbitrary",)),
        name="moe_slot_tok",
    )(dest_flat)


def _expert_kernel(be_ref, nu_ref, nx_ref, tok_ref, hn_ref, wgu_ref, wdn_ref, bg_ref, bu_ref, bd_ref, o_ref,
                   buf_ref, act_ref, sgu_ref, sdn_ref, wg_ref, wu_ref, wd_ref, sem, wsem, *, unroll, chunk):
    i = pl.program_id(0)
    n_used = nu_ref[0]
    e = be_ref[i]
    f = wg_ref.shape[1]
    d = wd_ref.shape[1]

    def row_copy(tok, r, s):
        return pltpu.make_async_copy(hn_ref.at[pl.ds(tok, 1)], buf_ref.at[s, pl.ds(r, 1)], sem.at[s])

    def weight_copies(ex):
        return (pltpu.make_async_copy(wgu_ref.at[ex], sgu_ref, wsem.at[0]),
                pltpu.make_async_copy(wdn_ref.at[ex], sdn_ref, wsem.at[1]))

    def wait_rows(s):
        def wait(r, c):
            row_copy(0, 0, s).wait()
            return c

        lax.fori_loop(0, MOE_ROWS, wait, 0, unroll=unroll)

    @pl.when((i == 0) & (n_used > 0))
    def _():
        for cp in weight_copies(e):
            cp.start(priority=1)

        def body(r2, c):
            for j in range(2):
                row_copy(tok_ref[2 * r2 + j], 2 * r2 + j, 0).start()
            return c

        lax.fori_loop(0, MOE_ROWS // 2, body, 0, unroll=unroll)

    first = (i == 0) | (e != be_ref[jnp.maximum(i - 1, 0)])

    @pl.when((i < n_used) & first)
    def _():
        for cp in weight_copies(e):
            cp.wait()
        half = chunk // 2
        pr = lax.broadcasted_iota(jnp.int32, (chunk, chunk), 0)
        pc = lax.broadcasted_iota(jnp.int32, (chunk, chunk), 1)
        perm = jnp.where(pr == jnp.where(pc < half, 2 * pc, 2 * (pc - half) + 1), 1.0, 0.0).astype(BF16)
        for c in range(2 * f // chunk):
            sep = _dot(sgu_ref[:, c * chunk:(c + 1) * chunk].astype(BF16), perm).astype(BF16)
            wg_ref[:, c * half:(c + 1) * half] = sep[:, :half]
            wu_ref[:, c * half:(c + 1) * half] = sep[:, half:]
        wd_ref[...] = sdn_ref[...].astype(BF16)

        @pl.when(nx_ref[i] >= 0)
        def _():
            for cp in weight_copies(nx_ref[i]):
                cp.start(priority=1)

    @pl.when(i < n_used)
    def _():
        cur = i % 2
        nxt = 1 - cur
        base = jnp.minimum(i + 1, n_used - 1) * MOE_ROWS
        n_groups = f // chunk + d // chunk
        per = MOE_ROWS // n_groups

        def issue_group(k):
            for r in range(k * per, (k + 1) * per):
                row_copy(tok_ref[base + r], r, nxt).start()

        wait_rows(cur)
        x = buf_ref[cur].astype(BF16)
        for c in range(f // chunk):
            cols = slice(c * chunk, (c + 1) * chunk)
            g = jnp.minimum(_dot(x, wg_ref[:, cols]) + bg_ref[0, :, cols], SWIGLU_LIMIT)
            u = jnp.clip(_dot(x, wu_ref[:, cols]) + bu_ref[0, :, cols], -SWIGLU_LIMIT, SWIGLU_LIMIT)
            act_ref[:, cols] = ((u + 1.0) * g * jax.nn.sigmoid(SWIGLU_ALPHA * g)).astype(BF16)
            issue_group(c)
        act = act_ref[...]
        for c in range(d // chunk):
            cols = slice(c * chunk, (c + 1) * chunk)
            o_ref[:, cols] = _dot(act, wd_ref[:, cols]) + bd_ref[0, :, cols]
            issue_group(f // chunk + c)

        @pl.when(i == n_used - 1)
        def _():
            wait_rows(nxt)

    @pl.when(i >= n_used)
    def _():
        o_ref[...] = jnp.zeros(o_ref.shape, o_ref.dtype)


def _moe_experts(block_exp, n_used, next_exp, slot_tok, hn, w_gate_up, w_down, bg, bu, bd):
    n_slots = slot_tok.shape[0]
    d = hn.shape[1]
    f = w_down.shape[1]
    nblk = n_slots // MOE_ROWS
    bspec = lambda w: pl.BlockSpec((1, 1, w), lambda i, be, nu, nx, tk: (be[i], 0, 0))
    return pl.pallas_call(
        functools.partial(_expert_kernel, unroll=8, chunk=256),
        out_shape=jax.ShapeDtypeStruct((n_slots, d), F32),
        grid_spec=pltpu.PrefetchScalarGridSpec(
            num_scalar_prefetch=4,
            grid=(nblk,),
            in_specs=[pl.BlockSpec(memory_space=pl.ANY), pl.BlockSpec(memory_space=pl.ANY),
                      pl.BlockSpec(memory_space=pl.ANY), bspec(f), bspec(f), bspec(d)],
            out_specs=pl.BlockSpec((MOE_ROWS, d), lambda i, be, nu, nx, tk: (i, 0)),
            scratch_shapes=[pltpu.VMEM((2, MOE_ROWS, d), hn.dtype), pltpu.VMEM((MOE_ROWS, f), BF16),
                            pltpu.VMEM((d, 2 * f), F32), pltpu.VMEM((f, d), F32),
                            pltpu.VMEM((d, f), BF16), pltpu.VMEM((d, f), BF16), pltpu.VMEM((f, d), BF16),
                            pltpu.SemaphoreType.DMA((2,)), pltpu.SemaphoreType.DMA((2,))]),
        compiler_params=_cparams(("arbitrary",)),
        name="moe_experts",
    )(block_exp, n_used, next_exp, slot_tok, hn, w_gate_up, w_down, bg, bu, bd)


def _combine_kernel(dest_ref, gate_ref, x1_ref, g_ref, ys_ref, o_ref, buf_ref, sem, *, tm, unroll):
    i = pl.program_id(0)
    n = pl.num_programs(0)

    def row_copy(slot_row, k, r, s):
        return pltpu.make_async_copy(ys_ref.at[pl.ds(slot_row, 1)], buf_ref.at[s, k, pl.ds(r, 1)], sem.at[s])

    def issue(step, s):
        base = step * (tm * TOP_K)
        for r in range(tm):
            for k in range(TOP_K):
                row_copy(dest_ref[base + r * TOP_K + k], k, r, s).start()

    @pl.when(i == 0)
    def _():
        issue(0, 0)

    @pl.when(i + 1 < n)
    def _():
        issue(i + 1, (i + 1) % 2)

    cur = i % 2

    def wait(r, c):
        row_copy(0, 0, 0, cur).wait()
        return c

    lax.fori_loop(0, tm * TOP_K, wait, 0, unroll=unroll)

    gate = gate_ref[...]
    y = x1_ref[...]
    for k in range(TOP_K):
        y = y + gate[:, k:k + 1] * buf_ref[cur, k]
    ms = jnp.mean(y * y, axis=-1, keepdims=True)
    o_ref[...] = y * lax.rsqrt(ms + RMS_EPS) * g_ref[...]


def _moe_combine(dest_flat, gate_slab, x1, g_final, ys, tm=128):
    t, d = x1.shape
    return pl.pallas_call(
        functools.partial(_combine_kernel, tm=tm, unroll=8),
        out_shape=jax.ShapeDtypeStruct((t, d), F32),
        grid_spec=pltpu.PrefetchScalarGridSpec(
            num_scalar_prefetch=1,
            grid=(t // tm,),
            in_specs=[pl.BlockSpec((tm, LANES), lambda i, dst: (i, 0)),
                      pl.BlockSpec((tm, d), lambda i, dst: (i, 0)),
                      pl.BlockSpec((1, d), lambda i, dst: (0, 0)),
                      pl.BlockSpec(memory_space=pl.ANY)],
            out_specs=pl.BlockSpec((tm, d), lambda i, dst: (i, 0)),
            scratch_shapes=[pltpu.VMEM((2, TOP_K, tm, d), F32), pltpu.SemaphoreType.DMA((2,))]),
        compiler_params=_cparams(("arbitrary",)),
        name="moe_combine",
    )(dest_flat, gate_slab, x1, g_final, ys)


def _rope_tables(s):
    inv = ROPE_THETA ** (-jnp.arange(0, HEAD_DIM, 2, dtype=F32) / HEAD_DIM)
    ang = jnp.arange(s, dtype=F32)[:, None] * inv[None, :]
    cos, sin = jnp.cos(ang), jnp.sin(ang)
    return jnp.concatenate([cos, cos], axis=-1), jnp.concatenate([-sin, sin], axis=-1)


def _permute_w_in(w):
    d = w.shape[0]
    pad = N_PROJ - (C_SMALL + 32)
    cols = [w[:, 3360:6432], w[:, 0:512], w[:, 1304:1816], w[:, 1816:2328], w[:, 2328:2840],
            w[:, 2848:3360], w[:, 512:1280], w[:, 1280:1304], w[:, 2840:2848], jnp.zeros((d, pad), w.dtype)]
    return jnp.concatenate(cols, axis=1).astype(BF16)


def _layer(x, mem, g_mix, w_in, b_forget, b_merge, pe_k, w1_k, w2_k, pe_v, w1_v, w2_v, g_mem, w_mem_kv,
           w_br_nsa, w_br_fox, w_br_mem, w_out, g_ffn, w_router, b_router, w_gate_up, b_gate_up,
           w_down, b_down, g_final):
    b, s, d = x.shape
    t = b * s
    x2 = x.reshape(t, d)
    cos64, sin64 = _rope_tables(s)
    cos2 = jnp.concatenate([cos64, cos64], axis=-1)
    sin2 = jnp.concatenate([sin64, sin64], axis=-1)

    proj = _inproj(x2, g_mix.reshape(1, d), _permute_w_in(w_in), tm=min(1024, t))
    proj3 = proj.reshape(b, s, N_PROJ)
    b_row = jnp.zeros((1, LANES), F32).at[0, FL_OFF:FL_OFF + FOX_HEADS].set(b_forget)
    cum, cum_t = _fox_cum(proj3, b_row)

    def both_groups(w):
        z = jnp.zeros_like(w)
        return jnp.concatenate([jnp.concatenate([w, z], axis=-1), jnp.concatenate([z, w], axis=-1)], axis=-2).astype(BF16)

    kcmp, vcmp_t = _nsa_compress(
        proj3, cos2, sin2, jnp.concatenate([pe_k, pe_k], axis=-1), jnp.concatenate([pe_v, pe_v], axis=-1),
        both_groups(w1_k), both_groups(w1_v), both_groups(w2_k), both_groups(w2_v))
    fox_tile = min(FOX_TILE, s)
    kf, vft, ks, vst, kw, vwt = _attn_prep(proj3, cos2, sin2, fox_tile, NSA_K_TILE)
    o_nsa = _nsa_attend(proj3, cos2, sin2, kcmp, vcmp_t, ks, vst, kw, vwt, min(NSA_Q_TILE, s), NSA_K_TILE)
    o_fox = _fox_attend(proj3, kf, vft, cum, cum_t, fox_tile)
    mem_kv = _mem_kv(mem, g_mem.reshape(1, d), w_mem_kv.astype(BF16))
    o_mem = _mem_attend(proj3, mem_kv)

    x1, hn, idx_slab, gate_slab = _merge_route(
        o_nsa.reshape(t, -1), o_fox.reshape(t, -1), o_mem.reshape(t, -1), proj, b_merge.reshape(1, -1),
        w_br_nsa.astype(BF16), w_br_fox.astype(BF16), w_br_mem.astype(BF16), w_out.astype(BF16),
        x2, g_ffn.reshape(1, d), w_router, b_router.reshape(1, -1))

    rank_slab, cnt8 = _moe_rank(idx_slab)
    counts = cnt8[0]
    padded = (counts + MOE_ROWS - 1) // MOE_ROWS * MOE_ROWS
    pad_end = jnp.cumsum(padded)
    pad_start = pad_end - padded
    n_assign = t * TOP_K
    nblk = -(-(n_assign + N_EXPERTS * (MOE_ROWS - 1)) // MOE_ROWS)
    n_slots = nblk * MOE_ROWS
    top_idx = idx_slab[:, :TOP_K]
    dest = (pad_start[top_idx] + rank_slab[:, :TOP_K]).reshape(-1).astype(jnp.int32)
    slot_tok = _moe_slot_tok(dest, n_slots)
    blk_start = jnp.arange(nblk, dtype=jnp.int32) * MOE_ROWS
    block_exp = jnp.minimum(jnp.sum(blk_start[:, None] >= pad_end[None, :], axis=1),
                            N_EXPERTS - 1).astype(jnp.int32)
    n_used = (pad_end[-1] // MOE_ROWS).astype(jnp.int32).reshape(1)

    e_ids = jnp.arange(N_EXPERTS, dtype=jnp.int32)
    later = (e_ids[None, :] > block_exp[:, None]) & (counts[None, :] > 0)
    next_exp = jnp.min(jnp.where(later, e_ids[None, :], N_EXPERTS), axis=1)
    next_exp = jnp.where(next_exp < N_EXPERTS, next_exp, -1).astype(jnp.int32)

    f = w_down.shape[1]
    bgu = b_gate_up.reshape(N_EXPERTS, 1, f, 2)
    ys = _moe_experts(block_exp, n_used, next_exp, slot_tok, hn, w_gate_up, w_down,
                      bgu[..., 0], bgu[..., 1], b_down.reshape(N_EXPERTS, 1, d))
    out = _moe_combine(dest, gate_slab, x1, g_final.reshape(1, d), ys)
    return out.reshape(b, s, d)


def kernel(x, mem, g_mix, w_in, b_forget, b_merge, nsa_pe_k, nsa_w1_k, nsa_w2_k, nsa_pe_v, nsa_w1_v, nsa_w2_v, g_mem, w_mem_kv, w_branch_nsa, w_branch_fox, w_branch_mem, w_out, g_ffn, w_router, b_router, w_gate_up, b_gate_up, w_down, b_down, g_final):
    assert g_mix.shape[0] == 1, "single-layer trunk"
    return _layer(x, mem, g_mix[0], w_in[0], b_forget[0], b_merge[0], nsa_pe_k[0], nsa_w1_k[0], nsa_w2_k[0],
                  nsa_pe_v[0], nsa_w1_v[0], nsa_w2_v[0], g_mem[0], w_mem_kv[0], w_branch_nsa[0],
                  w_branch_fox[0], w_branch_mem[0], w_out[0], g_ffn[0], w_router[0], b_router[0],
                  w_gate_up[0], b_gate_up[0], w_down[0], b_down[0], g_final)
```

```python
import functools

import jax
import jax.numpy as jnp
import numpy as np
from jax import lax
from jax.experimental import pallas as pl
from jax.experimental.pallas import tpu as pltpu

F32 = jnp.float32
BF16 = jnp.bfloat16

D_MODEL = 1024
HEAD_DIM = 64
ROPE_THETA = 10000.0
RMS_EPS = 1e-5
NEG_INF = -1e30
FORCE_SCORE = 1e4
NSA_HEADS = 8
NSA_GROUPS = 2
NSA_REP = NSA_HEADS // NSA_GROUPS
CMP_LEN = 32
CMP_STRIDE = 16
SEL_BLOCK = 64
SEL_TOPK = 16
WINDOW = 512
FOX_HEADS = 8
MEM_HEADS = 4
MEM_HEAD_DIM = 128
N_EXPERTS = 32
TOP_K = 4
SWIGLU_LIMIT = 7.0
SWIGLU_ALPHA = 1.702

LANES = 128
VMEM_LIMIT = 48 * 1024 * 1024

C_MERGE = 0
C_QA = 3072
C_QF = 3584
C_KF = 4096
C_VF = 4608
C_QM = 5120
C_NSAKV = 5632
C_SMALL = 6400
N_PROJ = 6656
GL_OFF = 0
FL_OFF = 24

MOE_ROWS = 256
NSA_Q_TILE = 512
NSA_K_TILE = 256
FOX_TILE = 512


def _cparams(sem, vmem=VMEM_LIMIT):
    return pltpu.CompilerParams(dimension_semantics=sem, vmem_limit_bytes=vmem)


def _dot(a, b):
    return jnp.dot(a, b, preferred_element_type=F32)


def _dot_nt(a, b):
    return lax.dot_general(a, b, (((1,), (1,)), ((), ())), preferred_element_type=F32)


def _rope(x, cos, sin_signed):
    w = x.shape[-1]
    lane = lax.broadcasted_iota(jnp.int32, x.shape, x.ndim - 1)
    first = (lane & (HEAD_DIM - 1)) < (HEAD_DIM // 2)
    rot = jnp.where(first, pltpu.roll(x, w - HEAD_DIM // 2, x.ndim - 1),
                    pltpu.roll(x, HEAD_DIM // 2, x.ndim - 1))
    return x * cos + rot * sin_signed


def _split3(x):
    hi = x.astype(BF16)
    r1 = x - hi.astype(F32)
    mid = r1.astype(BF16)
    lo = (r1 - mid.astype(F32)).astype(BF16)
    return hi, mid, lo


def _inproj_kernel(x_ref, g_ref, w_ref, o_ref, hn_ref):
    @pl.when(pl.program_id(1) == 0)
    def _():
        x = x_ref[...]
        ms = jnp.mean(x * x, axis=-1, keepdims=True)
        hn_ref[...] = (x * lax.rsqrt(ms + RMS_EPS) * g_ref[...]).astype(BF16)

    o_ref[...] = _dot(hn_ref[...], w_ref[...])


def _inproj(x2, g, w_bf16, tm=1024, tn=512):
    t, d = x2.shape
    n = w_bf16.shape[1]
    return pl.pallas_call(
        _inproj_kernel,
        out_shape=jax.ShapeDtypeStruct((t, n), F32),
        grid=(t // tm, n // tn),
        in_specs=[pl.BlockSpec((tm, d), lambda i, j: (i, 0)),
                  pl.BlockSpec((1, d), lambda i, j: (0, 0)),
                  pl.BlockSpec((d, tn), lambda i, j: (0, j))],
        out_specs=pl.BlockSpec((tm, tn), lambda i, j: (i, j)),
        scratch_shapes=[pltpu.VMEM((tm, d), BF16)],
        compiler_params=_cparams(("parallel", "arbitrary")),
        name="inproj",
    )(x2, g, w_bf16)


def _cum_kernel(s_ref, b_ref, c_ref, ct_ref, *, blk):
    s = s_ref.shape[1]
    z = s_ref[0] + b_ref[...]
    logf = jnp.minimum(z, 0.0) - jnp.log1p(jnp.exp(-jnp.abs(z)))
    r = lax.broadcasted_iota(jnp.int32, (blk, blk), 0)
    c = lax.broadcasted_iota(jnp.int32, (blk, blk), 1)
    tri = jnp.where(c <= r, 1.0, 0.0).astype(BF16)
    carry = jnp.zeros((1, LANES), F32)
    for i in range(s // blk):
        hi, mid, lo = _split3(logf[i * blk:(i + 1) * blk])
        loc = (_dot(tri, hi) + _dot(tri, mid)) + _dot(tri, lo)
        out = loc + carry
        c_ref[0, i * blk:(i + 1) * blk, :] = out
        carry = out[blk - 1:blk, :]
    ct_ref[0] = c_ref[0].T


def _fox_cum(proj3, b_row, blk=256):
    b, s, _ = proj3.shape
    return pl.pallas_call(
        functools.partial(_cum_kernel, blk=blk),
        out_shape=(jax.ShapeDtypeStruct((b, s, LANES), F32),
                   jax.ShapeDtypeStruct((b, LANES, s), F32)),
        grid=(b,),
        in_specs=[pl.BlockSpec((1, s, LANES), lambda i: (i, 0, C_SMALL // LANES)),
                  pl.BlockSpec((1, LANES), lambda i: (0, 0))],
        out_specs=(pl.BlockSpec((1, s, LANES), lambda i: (i, 0, 0)),
                   pl.BlockSpec((1, LANES, s), lambda i: (i, 0, 0))),
        compiler_params=_cparams(("parallel",)),
        name="fox_cum",
    )(proj3, b_row)


def _cmp_kernel(kc_ref, vc_ref, cos_ref, sin_ref, pek_ref, pev_ref, w1k_ref, w1v_ref, w2k_ref, w2v_ref,
                ko_ref, vto_ref, kr_ref):
    s = kc_ref.shape[1]
    nb = s // CMP_STRIDE
    kr_ref[...] = _rope(kc_ref[0], cos_ref[...], sin_ref[...])

    def mlp(rows, pe_ref, w1_ref, w2_ref):
        pa = jnp.zeros((nb, LANES), F32)
        pb = jnp.zeros((nb, LANES), F32)
        for l in range(CMP_STRIDE):
            x = rows(l)
            pa = pa + _dot((x + pe_ref[l:l + 1, :]).astype(BF16), w1_ref[l])
            pb = pb + _dot((x + pe_ref[CMP_STRIDE + l:CMP_STRIDE + l + 1, :]).astype(BF16), w1_ref[CMP_STRIDE + l])
        z = pa + pltpu.roll(pb, nb - 1, 0)
        h = z * jax.nn.sigmoid(z)
        return _dot(h.astype(BF16), w2_ref[...])

    ko_ref[0] = mlp(lambda l: kr_ref[pl.ds(l, nb, stride=CMP_STRIDE), :], pek_ref, w1k_ref, w2k_ref).astype(BF16)
    kr_ref[0:nb, :] = mlp(lambda l: vc_ref[0, pl.ds(l, nb, stride=CMP_STRIDE), :], pev_ref, w1v_ref, w2v_ref)
    vto_ref[0] = kr_ref[0:nb, :].T.astype(BF16)


def _nsa_compress(proj3, cos2, sin2, pek, pev, w1k, w1v, w2k, w2v):
    b, s, _ = proj3.shape
    nb = s // CMP_STRIDE
    kvb = C_NSAKV // LANES
    full = lambda a: pl.BlockSpec(a.shape, lambda i: (0,) * a.ndim)
    return pl.pallas_call(
        _cmp_kernel,
        out_shape=(jax.ShapeDtypeStruct((b, nb, LANES), BF16), jax.ShapeDtypeStruct((b, LANES, nb), BF16)),
        grid=(b,),
        in_specs=[pl.BlockSpec((1, s, LANES), lambda i: (i, 0, kvb)),
                  pl.BlockSpec((1, s, LANES), lambda i: (i, 0, kvb + 1)),
                  full(cos2), full(sin2), full(pek), full(pev), full(w1k), full(w1v), full(w2k), full(w2v)],
        out_specs=(pl.BlockSpec((1, nb, LANES), lambda i: (i, 0, 0)),
                   pl.BlockSpec((1, LANES, nb), lambda i: (i, 0, 0))),
        scratch_shapes=[pltpu.VMEM((s, LANES), F32)],
        compiler_params=_cparams(("parallel",)),
        name="nsa_compress",
    )(proj3, proj3, cos2, sin2, pek, pev, w1k, w1v, w2k, w2v)


def _prep_kernel(kf_ref, vf_ref, ks_ref, vs_ref, kw_ref, vw_ref, cos_ref, sin_ref,
                 okf_ref, ovf_ref, oks_ref, ovs_ref, okw_ref, ovw_ref):
    okf_ref[0] = kf_ref[0].astype(BF16)
    ovf_ref[0, 0] = vf_ref[0].T.astype(BF16)
    oks_ref[0] = _rope(ks_ref[0], cos_ref[...], sin_ref[...]).astype(BF16)
    okw_ref[0] = _rope(kw_ref[0], cos_ref[...], sin_ref[...]).astype(BF16)
    tn = ovs_ref.shape[3]
    for i in range(ovs_ref.shape[1]):
        ovs_ref[0, i] = vs_ref[0, i * tn:(i + 1) * tn, :].T.astype(BF16)
        ovw_ref[0, i] = vw_ref[0, i * tn:(i + 1) * tn, :].T.astype(BF16)


def _attn_prep(proj3, cos2, sin2, tk, tn):
    b, s, _ = proj3.shape
    wf = FOX_HEADS * HEAD_DIM
    kvb = C_NSAKV // LANES
    nk = s // tk
    sub = tk // tn
    col = lambda w, c: pl.BlockSpec((1, tk, w), lambda i, j: (i, j, c))
    rows = lambda w: pl.BlockSpec((1, tk, w), lambda i, j: (i, j, 0))
    ntile = pl.BlockSpec((1, sub, LANES, tn), lambda i, j: (i, j, 0, 0))
    return pl.pallas_call(
        _prep_kernel,
        out_shape=(jax.ShapeDtypeStruct((b, s, wf), BF16), jax.ShapeDtypeStruct((b, nk, wf, tk), BF16),
                   jax.ShapeDtypeStruct((b, s, LANES), BF16), jax.ShapeDtypeStruct((b, nk * sub, LANES, tn), BF16),
                   jax.ShapeDtypeStruct((b, s, LANES), BF16), jax.ShapeDtypeStruct((b, nk * sub, LANES, tn), BF16)),
        grid=(b, nk),
        in_specs=[col(wf, C_KF // wf), col(wf, C_VF // wf), col(LANES, kvb + 2), col(LANES, kvb + 3),
                  col(LANES, kvb + 4), col(LANES, kvb + 5),
                  pl.BlockSpec((tk, LANES), lambda i, j: (j, 0)), pl.BlockSpec((tk, LANES), lambda i, j: (j, 0))],
        out_specs=(rows(wf), pl.BlockSpec((1, 1, wf, tk), lambda i, j: (i, j, 0, 0)),
                   rows(LANES), ntile, rows(LANES), ntile),
        compiler_params=_cparams(("parallel", "parallel")),
        name="attn_prep",
    )(proj3, proj3, proj3, proj3, proj3, proj3, cos2, sin2)


def _online_update(s, vt, m_ref, l_ref, acc_ref, idx, mask=None):
    m_old = m_ref[idx]
    m_new = jnp.maximum(m_old, jnp.max(s, axis=0, keepdims=True))
    alpha = jnp.exp(m_old - m_new)
    p = jnp.exp(s - m_new)
    if mask is not None:
        p = jnp.where(mask, p, 0.0)
    l_ref[idx] = alpha * l_ref[idx] + jnp.sum(p, axis=0, keepdims=True)
    acc_ref[idx] = alpha * acc_ref[idx] + _dot(vt, p.astype(BF16))
    m_ref[idx] = m_new


def _padded_qt(q_ref, cos, sin, qt_ref, tq, heads_per_slot):
    qt_ref[...] = jnp.zeros(qt_ref.shape, qt_ref.dtype)
    n_heads = q_ref.shape[2] // HEAD_DIM
    for hp in range(n_heads // 2):
        q2 = q_ref[0, :, hp * LANES:(hp + 1) * LANES]
        if cos is not None:
            q2 = _rope(q2, cos, sin)
        qt = (q2 * (HEAD_DIM ** -0.5)).T.astype(BF16)
        for sub in range(2):
            h = 2 * hp + sub
            slot, r = h // heads_per_slot, h % heads_per_slot
            half = (slot % 2) if heads_per_slot > 1 else sub
            qt_ref[slot, half * HEAD_DIM:(half + 1) * HEAD_DIM, r * tq:(r + 1) * tq] = (
                qt[sub * HEAD_DIM:(sub + 1) * HEAD_DIM, :])


def _nsa_kernel(q_ref, cos_ref, sin_ref, kc_ref, vct_ref, ks_ref, vst_ref, kw_ref, vwt_ref, sm_ref, o_ref,
                qt_ref, oct_ref, m_ref, l_ref, acc_ref, *, tq, tk, n_sel):
    qi = pl.program_id(1)
    G, R = NSA_GROUPS, NSA_REP
    nb = kc_ref.shape[1]
    per_q = tq // tk
    _padded_qt(q_ref, cos_ref[...], sin_ref[...], qt_ref, tq, R)
    m_ref[...] = jnp.full(m_ref.shape, NEG_INF, F32)
    l_ref[...] = jnp.zeros(l_ref.shape, F32)
    acc_ref[...] = jnp.zeros(acc_ref.shape, F32)

    tpos = qi * tq + (lax.broadcasted_iota(jnp.int32, (nb, R * tq), 1) & (tq - 1))
    nrow = lax.broadcasted_iota(jnp.int32, (nb, R * tq), 0)
    cmask = (nrow * CMP_STRIDE + (CMP_LEN - 1) <= tpos) & (nrow < nb - 1)
    oj = lax.broadcasted_iota(jnp.int32, (n_sel, nb), 0)
    on = lax.broadcasted_iota(jnp.int32, (n_sel, nb), 1) * CMP_STRIDE
    overlap_t = jnp.where((on < (oj + 1) * SEL_BLOCK) & (on + CMP_LEN > oj * SEL_BLOCK), 1.0, 0.0).astype(BF16)
    blk = lax.broadcasted_iota(jnp.int32, (n_sel, tq), 0)
    trow = qi * tq + lax.broadcasted_iota(jnp.int32, (n_sel, tq), 1)
    cur = trow // SEL_BLOCK
    forced = (blk == 0) | (blk == cur) | (blk == cur - 1)
    future = blk * SEL_BLOCK > trow
    kc = kc_ref[0]
    for g in range(G):
        s = jnp.where(cmask, _dot(kc, qt_ref[g, 0:LANES, :]), NEG_INF)
        mx = jnp.max(s, axis=0, keepdims=True)
        e = jnp.where(cmask, jnp.exp(s - mx), 0.0)
        p = e / jnp.maximum(jnp.sum(e, axis=0, keepdims=True), 1e-30)
        oct_ref[g] = _dot(vct_ref[0, g * HEAD_DIM:(g + 1) * HEAD_DIM, :], p.astype(BF16))
        psum = p[:, 0:tq]
        for r in range(1, R):
            psum = psum + p[:, r * tq:(r + 1) * tq]
        hi, mid, lo = _split3(psum)
        imp = (_dot(overlap_t, hi) + _dot(overlap_t, mid)) + _dot(overlap_t, lo)
        imp = jnp.where(forced, FORCE_SCORE, jnp.where(future, -FORCE_SCORE, imp))
        rank = jnp.zeros((n_sel, tq), F32)
        for i in range(n_sel):
            ri = imp[i:i + 1, :]
            ahead = (ri > imp) | ((ri == imp) & (blk > i))
            rank = rank + jnp.where(ahead, 1.0, 0.0)
        bias = jnp.where(rank < float(min(SEL_TOPK, n_sel)), 0.0, NEG_INF).astype(BF16)
        for r in range(R):
            qt_ref[g, LANES:LANES + n_sel, r * tq:(r + 1) * tq] = bias

    qpos = qi * tq + (lax.broadcasted_iota(jnp.int32, (tk, R * tq), 1) & (tq - 1))
    krow = lax.broadcasted_iota(jnp.int32, (tk, R * tq), 0)
    er = lax.broadcasted_iota(jnp.int32, (tk, LANES), 0)
    ec = lax.broadcasted_iota(jnp.int32, (tk, LANES), 1)

    def sel_tile(j, diagonal):
        k = ks_ref[0, pl.ds(pl.multiple_of(j * tk, tk), tk), :]
        onehot = jnp.where((j * tk + er) // SEL_BLOCK == ec, 1.0, 0.0).astype(BF16)
        kx = jnp.concatenate([k, onehot], axis=1)
        for g in range(G):
            s = _dot(kx, qt_ref[g])
            if diagonal:
                s = jnp.where(j * tk + krow <= qpos, s, NEG_INF)
            _online_update(s, vst_ref[0, j, g * HEAD_DIM:(g + 1) * HEAD_DIM, :], m_ref, l_ref, acc_ref, g)

    def sel_body(j, c):
        sel_tile(j, False)
        return c

    lax.fori_loop(0, qi * per_q, sel_body, 0)
    for dd in range(per_q):
        sel_tile(qi * per_q + dd, True)

    for back in range(per_q + WINDOW // tk - 1, -1, -1):
        @pl.when((qi + 1) * per_q - 1 - back >= 0)
        def _(back=back):
            j = (qi + 1) * per_q - 1 - back
            k = kw_ref[0, pl.ds(pl.multiple_of(j * tk, tk), tk), :]
            diff = qpos - (j * tk + krow)
            mask = (diff >= 0) & (diff < WINDOW)
            for g in range(G):
                s = jnp.where(mask, _dot(k, qt_ref[g, 0:LANES, :]), NEG_INF)
                _online_update(s, vwt_ref[0, j, g * HEAD_DIM:(g + 1) * HEAD_DIM, :], m_ref, l_ref, acc_ref,
                               G + g, mask)

    gates_t = jax.nn.sigmoid(sm_ref[0]).T
    rows = []
    for h in range(NSA_HEADS):
        g, r = h // R, h % R
        cols = slice(r * tq, (r + 1) * tq)
        gate = lambda br: gates_t[GL_OFF + 3 * h + br:GL_OFF + 3 * h + br + 1, :]
        o_sel = acc_ref[g, :, cols] / jnp.maximum(l_ref[g, :, cols], 1e-30)
        o_win = acc_ref[G + g, :, cols] / jnp.maximum(l_ref[G + g, :, cols], 1e-30)
        rows.append(gate(0) * oct_ref[g, :, cols] + gate(1) * o_sel + gate(2) * o_win)
    o_ref[0] = jnp.concatenate(rows, axis=0).T


def _nsa_attend(proj3, cos2, sin2, kcmp, vcmp_t, ks, vst, kw, vwt, tq, tk):
    b, s, _ = proj3.shape
    n_sel = s // SEL_BLOCK
    wq = NSA_HEADS * HEAD_DIM
    whole = lambda a: pl.BlockSpec((1,) + a.shape[1:], lambda i, j: (i,) + (0,) * (a.ndim - 1))
    return pl.pallas_call(
        functools.partial(_nsa_kernel, tq=tq, tk=tk, n_sel=n_sel),
        out_shape=jax.ShapeDtypeStruct((b, s, wq), F32),
        grid=(b, s // tq),
        in_specs=[pl.BlockSpec((1, tq, wq), lambda i, j: (i, j, C_QA // wq)),
                  pl.BlockSpec((tq, LANES), lambda i, j: (j, 0)),
                  pl.BlockSpec((tq, LANES), lambda i, j: (j, 0)),
                  whole(kcmp), whole(vcmp_t), whole(ks), whole(vst), whole(kw), whole(vwt),
                  pl.BlockSpec((1, tq, LANES), lambda i, j: (i, j, C_SMALL // LANES))],
        out_specs=pl.BlockSpec((1, tq, wq), lambda i, j: (i, j, 0)),
        scratch_shapes=[pltpu.VMEM((NSA_GROUPS, 2 * LANES, NSA_REP * tq), BF16),
                        pltpu.VMEM((NSA_GROUPS, HEAD_DIM, NSA_REP * tq), F32),
                        pltpu.VMEM((2 * NSA_GROUPS, 1, NSA_REP * tq), F32),
                        pltpu.VMEM((2 * NSA_GROUPS, 1, NSA_REP * tq), F32),
                        pltpu.VMEM((2 * NSA_GROUPS, HEAD_DIM, NSA_REP * tq), F32)],
        compiler_params=_cparams(("parallel", "parallel")),
        name="nsa_attend",
    )(proj3, cos2, sin2, kcmp, vcmp_t, ks, vst, kw, vwt, proj3)


def _fox_kernel(q_ref, k_ref, vt_ref, cq_ref, ck_ref, o_ref, qt_ref, m_ref, l_ref, acc_ref, *, tq):
    qi = pl.program_id(1)
    tk = tq
    _padded_qt(q_ref, None, None, qt_ref, tq, 1)
    m_ref[...] = jnp.full(m_ref.shape, NEG_INF, F32)
    l_ref[...] = jnp.zeros(l_ref.shape, F32)
    acc_ref[...] = jnp.zeros(acc_ref.shape, F32)
    cq = cq_ref[0]
    causal = lax.broadcasted_iota(jnp.int32, (tk, tq), 0) <= lax.broadcasted_iota(jnp.int32, (tk, tq), 1)

    def tile(j, diagonal):
        rows = pl.ds(pl.multiple_of(j * tk, tk), tk)
        ck = ck_ref[0, rows, :]
        for h in range(FOX_HEADS):
            hp = h // 2
            s = _dot(k_ref[0, rows, hp * LANES:(hp + 1) * LANES], qt_ref[h])
            s = (s - ck[:, FL_OFF + h:FL_OFF + h + 1]) + cq[FL_OFF + h:FL_OFF + h + 1, :]
            if diagonal:
                s = jnp.where(causal, s, NEG_INF)
            _online_update(s, vt_ref[0, j, h * HEAD_DIM:(h + 1) * HEAD_DIM, :], m_ref, l_ref, acc_ref, h)

    def body(j, c):
        tile(j, False)
        return c

    lax.fori_loop(0, qi, body, 0)
    tile(qi, True)
    rows = [acc_ref[h] / jnp.maximum(l_ref[h], 1e-30) for h in range(FOX_HEADS)]
    o_ref[0] = jnp.concatenate(rows, axis=0).T


def _fox_attend(proj3, kf, vft, cum, cum_t, tq):
    b, s, _ = proj3.shape
    w = FOX_HEADS * HEAD_DIM
    whole = lambda a: pl.BlockSpec((1,) + a.shape[1:], lambda i, j: (i,) + (0,) * (a.ndim - 1))
    return pl.pallas_call(
        functools.partial(_fox_kernel, tq=tq),
        out_shape=jax.ShapeDtypeStruct((b, s, w), F32),
        grid=(b, s // tq),
        in_specs=[pl.BlockSpec((1, tq, w), lambda i, j: (i, j, C_QF // w)),
                  whole(kf), whole(vft),
                  pl.BlockSpec((1, LANES, tq), lambda i, j: (i, 0, j)),
                  whole(cum)],
        out_specs=pl.BlockSpec((1, tq, w), lambda i, j: (i, j, 0)),
        scratch_shapes=[pltpu.VMEM((FOX_HEADS, LANES, tq), BF16),
                        pltpu.VMEM((FOX_HEADS, 1, tq), F32),
                        pltpu.VMEM((FOX_HEADS, 1, tq), F32),
                        pltpu.VMEM((FOX_HEADS, HEAD_DIM, tq), F32)],
        compiler_params=_cparams(("parallel", "parallel")),
        name="fox_attend",
    )(proj3, kf, vft, cum_t, cum)


def _memkv_kernel(m_ref, g_ref, w_ref, o_ref):
    x = m_ref[0]
    ms = jnp.mean(x * x, axis=-1, keepdims=True)
    hn = (x * lax.rsqrt(ms + RMS_EPS) * g_ref[...]).astype(BF16)
    o_ref[0] = _dot(hn, w_ref[...]).astype(BF16)


def _mem_kv(mem, g, w_bf16):
    b, m, d = mem.shape
    n = w_bf16.shape[1]
    return pl.pallas_call(
        _memkv_kernel,
        out_shape=jax.ShapeDtypeStruct((b, m, n), BF16),
        grid=(b,),
        in_specs=[pl.BlockSpec((1, m, d), lambda i: (i, 0, 0)),
                  pl.BlockSpec((1, d), lambda i: (0, 0)),
                  pl.BlockSpec((d, n), lambda i: (0, 0))],
        out_specs=pl.BlockSpec((1, m, n), lambda i: (i, 0, 0)),
        compiler_params=_cparams(("parallel",)),
        name="mem_kv",
    )(mem, g, w_bf16)


def _memattn_kernel(q_ref, kv_ref, o_ref):
    w = MEM_HEADS * MEM_HEAD_DIM
    for h in range(MEM_HEADS):
        lo, hi = h * MEM_HEAD_DIM, (h + 1) * MEM_HEAD_DIM
        s = _dot_nt(q_ref[0, :, lo:hi].astype(BF16), kv_ref[0, :, lo:hi]) * (MEM_HEAD_DIM ** -0.5)
        e = jnp.exp(s - jnp.max(s, axis=-1, keepdims=True))
        p = e / jnp.sum(e, axis=-1, keepdims=True)
        o_ref[0, :, lo:hi] = _dot(p.astype(BF16), kv_ref[0, :, w + lo:w + hi])


def _mem_attend(proj3, kv, tq=512):
    b, s, _ = proj3.shape
    m = kv.shape[1]
    w = MEM_HEADS * MEM_HEAD_DIM
    return pl.pallas_call(
        _memattn_kernel,
        out_shape=jax.ShapeDtypeStruct((b, s, w), F32),
        grid=(b, s // tq),
        in_specs=[pl.BlockSpec((1, tq, w), lambda i, j: (i, j, C_QM // w)),
                  pl.BlockSpec((1, m, 2 * w), lambda i, j: (i, 0, 0))],
        out_specs=pl.BlockSpec((1, tq, w), lambda i, j: (i, j, 0)),
        compiler_params=_cparams(("parallel", "parallel")),
        name="mem_attend",
    )(proj3, kv)


def _merge_kernel(on_ref, of_ref, om_ref, ln_ref, lf_ref, lm_ref, bm_ref, wn_ref, wf_ref, wm_ref,
                  wo_ref, x_ref, g_ref, wr_ref, br_ref, x1_ref, hn_ref, idx_ref, gate_ref):
    d = x_ref.shape[1]

    def branch(o_ref, l_ref, w_ref, k):
        gate = jax.nn.sigmoid(l_ref[...] + bm_ref[:, k * d:(k + 1) * d])
        return gate * _dot(o_ref[...].astype(BF16), w_ref[...])

    merged = branch(on_ref, ln_ref, wn_ref, 0) + branch(of_ref, lf_ref, wf_ref, 1) + branch(om_ref, lm_ref, wm_ref, 2)
    x1 = x_ref[...] + _dot(merged.astype(BF16), wo_ref[...])
    x1_ref[...] = x1
    ms = jnp.mean(x1 * x1, axis=-1, keepdims=True)
    hn = x1 * lax.rsqrt(ms + RMS_EPS) * g_ref[...]
    hn_ref[...] = hn
    h_hi = hn.astype(BF16)
    h_lo = (hn - h_hi.astype(F32)).astype(BF16)
    wr = wr_ref[...]
    w_hi = wr.astype(BF16)
    w_lo = (wr - w_hi.astype(F32)).astype(BF16)
    logits = (_dot(h_hi, w_hi) + (_dot(h_lo, w_hi) + _dot(h_hi, w_lo))) + br_ref[...]
    tm, ne = logits.shape
    lane_e = lax.broadcasted_iota(jnp.int32, (tm, ne), 1)
    lane = lax.broadcasted_iota(jnp.int32, (tm, LANES), 1)
    idx_slab = jnp.zeros((tm, LANES), jnp.int32)
    val_slab = jnp.zeros((tm, LANES), F32)
    work = logits
    vals = []
    for k in range(TOP_K):
        m = jnp.max(work, axis=-1, keepdims=True)
        idx = jnp.min(jnp.where(work == m, lane_e, ne), axis=-1, keepdims=True)
        work = jnp.where(lane_e == idx, -jnp.inf, work)
        idx_slab = jnp.where(lane == k, idx, idx_slab)
        vals.append(m)
    es = [jnp.exp(v - vals[0]) for v in vals]
    tot = es[0]
    for e in es[1:]:
        tot = tot + e
    for k in range(TOP_K):
        val_slab = jnp.where(lane == k, es[k] / tot, val_slab)
    idx_ref[...] = idx_slab
    gate_ref[...] = val_slab


def _merge_route(o_nsa, o_fox, o_mem, proj, b_merge, wn, wf, wm, wo, x2, g_ffn, w_router, b_router, tm=512):
    t, d = x2.shape
    wb = o_nsa.shape[1]
    row = lambda w: pl.BlockSpec((tm, w), lambda i: (i, 0))
    full = lambda a: pl.BlockSpec(a.shape, lambda i: (0,) * a.ndim)
    return pl.pallas_call(
        _merge_kernel,
        out_shape=(jax.ShapeDtypeStruct((t, d), F32), jax.ShapeDtypeStruct((t, d), F32),
                   jax.ShapeDtypeStruct((t, LANES), jnp.int32), jax.ShapeDtypeStruct((t, LANES), F32)),
        grid=(t // tm,),
        in_specs=[row(wb), row(wb), row(wb),
                  pl.BlockSpec((tm, d), lambda i: (i, 0)),
                  pl.BlockSpec((tm, d), lambda i: (i, 1)),
                  pl.BlockSpec((tm, d), lambda i: (i, 2)),
                  full(b_merge), full(wn), full(wf), full(wm), full(wo),
                  row(d), full(g_ffn), full(w_router), full(b_router)],
        out_specs=(row(d), row(d), row(LANES), row(LANES)),
        compiler_params=_cparams(("parallel",)),
        name="merge_route",
    )(o_nsa, o_fox, o_mem, proj, proj, proj, b_merge, wn, wf, wm, wo, x2, g_ffn, w_router, b_router)


def _rank_kernel(idx_ref, rank_ref, cnt_ref, carry_ref, *, tm):
    @pl.when(pl.program_id(0) == 0)
    def _():
        carry_ref[...] = jnp.zeros(carry_ref.shape, F32)

    idx = idx_ref[...]
    lane_e = lax.broadcasted_iota(jnp.int32, (tm, N_EXPERTS), 1)
    hots = [jnp.where(idx[:, k:k + 1] == lane_e, 1.0, 0.0) for k in range(TOP_K)]
    cnt = hots[0]
    for hk in hots[1:]:
        cnt = cnt + hk
    r = lax.broadcasted_iota(jnp.int32, (tm, tm), 0)
    c = lax.broadcasted_iota(jnp.int32, (tm, tm), 1)
    strict = jnp.where(c < r, 1.0, 0.0).astype(BF16)
    before = _dot(strict, cnt.astype(BF16)) + carry_ref[...]
    lane = lax.broadcasted_iota(jnp.int32, (tm, LANES), 1)
    slab = jnp.zeros((tm, LANES), F32)
    for k in range(TOP_K):
        slab = jnp.where(lane == k, jnp.sum(hots[k] * before, axis=-1, keepdims=True), slab)
    rank_ref[...] = slab.astype(jnp.int32)
    total = carry_ref[...] + jnp.sum(cnt, axis=0, keepdims=True)
    carry_ref[...] = total
    cnt_ref[...] = jnp.broadcast_to(total, cnt_ref.shape).astype(jnp.int32)


def _moe_rank(idx_slab, tm=256):
    t = idx_slab.shape[0]
    return pl.pallas_call(
        functools.partial(_rank_kernel, tm=tm),
        out_shape=(jax.ShapeDtypeStruct((t, LANES), jnp.int32),
                   jax.ShapeDtypeStruct((8, N_EXPERTS), jnp.int32)),
        grid=(t // tm,),
        in_specs=[pl.BlockSpec((tm, LANES), lambda i: (i, 0))],
        out_specs=(pl.BlockSpec((tm, LANES), lambda i: (i, 0)),
                   pl.BlockSpec((8, N_EXPERTS), lambda i: (0, 0))),
        scratch_shapes=[pltpu.VMEM((1, N_EXPERTS), F32)],
        compiler_params=_cparams(("arbitrary",)),
        name="moe_rank",
    )(idx_slab)


def _slot_tok_kernel(dest_ref, o_ref, *, steps, unroll):
    i = pl.program_id(0)
    shift = TOP_K.bit_length() - 1
    assert TOP_K == 1 << shift
    n_clear = o_ref.shape[0] // steps
    n_put = dest_ref.shape[0] // steps

    @pl.when(i < steps)
    def _():
        def clear(j, c):
            o_ref[i * n_clear + j] = 0
            return c

        lax.fori_loop(0, n_clear, clear, 0, unroll=unroll)

    @pl.when(i >= steps)
    def _():
        def put(j, c):
            a = (i - steps) * n_put + j
            o_ref[dest_ref[a]] = lax.shift_right_logical(a, shift)
            return c

        lax.fori_loop(0, n_put, put, 0, unroll=unroll)


def _moe_slot_tok(dest_flat, n_slots, steps=32):
    assert n_slots % steps == 0 and dest_flat.shape[0] % steps == 0
    return pl.pallas_call(
        functools.partial(_slot_tok_kernel, steps=steps, unroll=8),
        out_shape=jax.ShapeDtypeStruct((n_slots,), jnp.int32),
        grid_spec=pltpu.PrefetchScalarGridSpec(
            num_scalar_prefetch=1, grid=(2 * steps,), in_specs=[],
            out_specs=pl.BlockSpec(memory_space=pltpu.SMEM)),
        compiler_params=_cparams(("arbitrary",)),
        name="moe_slot_tok",
    )(dest_flat)


def _expert_kernel(be_ref, nu_ref, nx_ref, tok_ref, hn_ref, wgu_ref, wdn_ref, bg_ref, bu_ref, bd_ref, o_ref,
                   buf_ref, act_ref, sgu_ref, sdn_ref, wg_ref, wu_ref, wd_ref, sem, wsem, *, unroll, chunk):
    i = pl.program_id(0)
    n_used = nu_ref[0]
    e = be_ref[i]
    f = wg_ref.shape[1]
    d = wd_ref.shape[1]

    def row_copy(tok, r, s):
        return pltpu.make_async_copy(hn_ref.at[pl.ds(tok, 1)], buf_ref.at[s, pl.ds(r, 1)], sem.at[s])

    def weight_copies(ex):
        return (pltpu.make_async_copy(wgu_ref.at[ex], sgu_ref, wsem.at[0]),
                pltpu.make_async_copy(wdn_ref.at[ex], sdn_ref, wsem.at[1]))

    def wait_rows(s):
        def wait(r, c):
            row_copy(0, 0, s).wait()
            return c

        lax.fori_loop(0, MOE_ROWS, wait, 0, unroll=unroll)

    @pl.when((i == 0) & (n_used > 0))
    def _():
        for cp in weight_copies(e):
            cp.start(priority=1)

        def body(r2, c):
            for j in range(2):
                row_copy(tok_ref[2 * r2 + j], 2 * r2 + j, 0).start()
            return c

        lax.fori_loop(0, MOE_ROWS // 2, body, 0, unroll=unroll)

    first = (i == 0) | (e != be_ref[jnp.maximum(i - 1, 0)])

    @pl.when((i < n_used) & first)
    def _():
        for cp in weight_copies(e):
            cp.wait()
        half = chunk // 2
        pr = lax.broadcasted_iota(jnp.int32, (chunk, chunk), 0)
        pc = lax.broadcasted_iota(jnp.int32, (chunk, chunk), 1)
        perm = jnp.where(pr == jnp.where(pc < half, 2 * pc, 2 * (pc - half) + 1), 1.0, 0.0).astype(BF16)
        for c in range(2 * f // chunk):
            sep = _dot(sgu_ref[:, c * chunk:(c + 1) * chunk].astype(BF16), perm).astype(BF16)
            wg_ref[:, c * half:(c + 1) * half] = sep[:, :half]
            wu_ref[:, c * half:(c + 1) * half] = sep[:, half:]
        wd_ref[...] = sdn_ref[...].astype(BF16)

        @pl.when(nx_ref[i] >= 0)
        def _():
            for cp in weight_copies(nx_ref[i]):
                cp.start(priority=1)

    @pl.when(i < n_used)
    def _():
        cur = i % 2
        nxt = 1 - cur
        base = jnp.minimum(i + 1, n_used - 1) * MOE_ROWS
        for r in range(MOE_ROWS):
            row_copy(tok_ref[base + r], r, nxt).start()
        wait_rows(cur)
        x = buf_ref[cur].astype(BF16)
        for c in range(f // chunk):
            cols = slice(c * chunk, (c + 1) * chunk)
            g = jnp.minimum(_dot(x, wg_ref[:, cols]) + bg_ref[0, :, cols], SWIGLU_LIMIT)
            u = jnp.clip(_dot(x, wu_ref[:, cols]) + bu_ref[0, :, cols], -SWIGLU_LIMIT, SWIGLU_LIMIT)
            act_ref[:, cols] = ((u + 1.0) * g * jax.nn.sigmoid(SWIGLU_ALPHA * g)).astype(BF16)
        act = act_ref[...]
        for c in range(d // chunk):
            cols = slice(c * chunk, (c + 1) * chunk)
            o_ref[:, cols] = _dot(act, wd_ref[:, cols]) + bd_ref[0, :, cols]

        @pl.when(i == n_used - 1)
        def _():
            wait_rows(nxt)

    @pl.when(i >= n_used)
    def _():
        o_ref[...] = jnp.zeros(o_ref.shape, o_ref.dtype)


def _moe_experts(block_exp, n_used, next_exp, slot_tok, hn, w_gate_up, w_down, bg, bu, bd):
    n_slots = slot_tok.shape[0]
    d = hn.shape[1]
    f = w_down.shape[1]
    nblk = n_slots // MOE_ROWS
    bspec = lambda w: pl.BlockSpec((1, 1, w), lambda i, be, nu, nx, tk: (be[i], 0, 0))
    return pl.pallas_call(
        functools.partial(_expert_kernel, unroll=8, chunk=256),
        out_shape=jax.ShapeDtypeStruct((n_slots, d), F32),
        grid_spec=pltpu.PrefetchScalarGridSpec(
            num_scalar_prefetch=4,
            grid=(nblk,),
            in_specs=[pl.BlockSpec(memory_space=pl.ANY), pl.BlockSpec(memory_space=pl.ANY),
                      pl.BlockSpec(memory_space=pl.ANY), bspec(f), bspec(f), bspec(d)],
            out_specs=pl.BlockSpec((MOE_ROWS, d), lambda i, be, nu, nx, tk: (i, 0)),
            scratch_shapes=[pltpu.VMEM((2, MOE_ROWS, d), hn.dtype), pltpu.VMEM((MOE_ROWS, f), BF16),
                            pltpu.VMEM((d, 2 * f), F32), pltpu.VMEM((f, d), F32),
                            pltpu.VMEM((d, f), BF16), pltpu.VMEM((d, f), BF16), pltpu.VMEM((f, d), BF16),
                            pltpu.SemaphoreType.DMA((2,)), pltpu.SemaphoreType.DMA((2,))]),
        compiler_params=_cparams(("arbitrary",)),
        name="moe_experts",
    )(block_exp, n_used, next_exp, slot_tok, hn, w_gate_up, w_down, bg, bu, bd)


def _combine_kernel(dest_ref, gate_ref, x1_ref, g_ref, ys_ref, o_ref, buf_ref, sem, *, tm, unroll):
    i = pl.program_id(0)
    n = pl.num_programs(0)

    def row_copy(slot_row, k, r, s):
        return pltpu.make_async_copy(ys_ref.at[pl.ds(slot_row, 1)], buf_ref.at[s, k, pl.ds(r, 1)], sem.at[s])

    def issue(step, s):
        base = step * (tm * TOP_K)
        for r in range(tm):
            for k in range(TOP_K):
                row_copy(dest_ref[base + r * TOP_K + k], k, r, s).start()

    @pl.when(i == 0)
    def _():
        issue(0, 0)

    @pl.when(i + 1 < n)
    def _():
        issue(i + 1, (i + 1) % 2)

    cur = i % 2

    def wait(r, c):
        row_copy(0, 0, 0, cur).wait()
        return c

    lax.fori_loop(0, tm * TOP_K, wait, 0, unroll=unroll)

    gate = gate_ref[...]
    y = x1_ref[...]
    for k in range(TOP_K):
        y = y + gate[:, k:k + 1] * buf_ref[cur, k]
    ms = jnp.mean(y * y, axis=-1, keepdims=True)
    o_ref[...] = y * lax.rsqrt(ms + RMS_EPS) * g_ref[...]


def _moe_combine(dest_flat, gate_slab, x1, g_final, ys, tm=128):
    t, d = x1.shape
    return pl.pallas_call(
        functools.partial(_combine_kernel, tm=tm, unroll=8),
        out_shape=jax.ShapeDtypeStruct((t, d), F32),
        grid_spec=pltpu.PrefetchScalarGridSpec(
            num_scalar_prefetch=1,
            grid=(t // tm,),
            in_specs=[pl.BlockSpec((tm, LANES), lambda i, dst: (i, 0)),
                      pl.BlockSpec((tm, d), lambda i, dst: (i, 0)),
                      pl.BlockSpec((1, d), lambda i, dst: (0, 0)),
                      pl.BlockSpec(memory_space=pl.ANY)],
            out_specs=pl.BlockSpec((tm, d), lambda i, dst: (i, 0)),
            scratch_shapes=[pltpu.VMEM((2, TOP_K, tm, d), F32), pltpu.SemaphoreType.DMA((2,))]),
        compiler_params=_cparams(("arbitrary",)),
        name="moe_combine",
    )(dest_flat, gate_slab, x1, g_final, ys)


def _rope_tables(s):
    inv = ROPE_THETA ** (-jnp.arange(0, HEAD_DIM, 2, dtype=F32) / HEAD_DIM)
    ang = jnp.arange(s, dtype=F32)[:, None] * inv[None, :]
    cos, sin = jnp.cos(ang), jnp.sin(ang)
    return jnp.concatenate([cos, cos], axis=-1), jnp.concatenate([-sin, sin], axis=-1)


def _permute_w_in(w):
    d = w.shape[0]
    pad = N_PROJ - (C_SMALL + 32)
    cols = [w[:, 3360:6432], w[:, 0:512], w[:, 1304:1816], w[:, 1816:2328], w[:, 2328:2840],
            w[:, 2848:3360], w[:, 512:1280], w[:, 1280:1304], w[:, 2840:2848], jnp.zeros((d, pad), w.dtype)]
    return jnp.concatenate(cols, axis=1).astype(BF16)


def _layer(x, mem, g_mix, w_in, b_forget, b_merge, pe_k, w1_k, w2_k, pe_v, w1_v, w2_v, g_mem, w_mem_kv,
           w_br_nsa, w_br_fox, w_br_mem, w_out, g_ffn, w_router, b_router, w_gate_up, b_gate_up,
           w_down, b_down, g_final):
    b, s, d = x.shape
    t = b * s
    x2 = x.reshape(t, d)
    cos64, sin64 = _rope_tables(s)
    cos2 = jnp.concatenate([cos64, cos64], axis=-1)
    sin2 = jnp.concatenate([sin64, sin64], axis=-1)

    proj = _inproj(x2, g_mix.reshape(1, d), _permute_w_in(w_in), tm=min(1024, t))
    proj3 = proj.reshape(b, s, N_PROJ)
    b_row = jnp.zeros((1, LANES), F32).at[0, FL_OFF:FL_OFF + FOX_HEADS].set(b_forget)
    cum, cum_t = _fox_cum(proj3, b_row)

    def both_groups(w):
        z = jnp.zeros_like(w)
        return jnp.concatenate([jnp.concatenate([w, z], axis=-1), jnp.concatenate([z, w], axis=-1)], axis=-2).astype(BF16)

    kcmp, vcmp_t = _nsa_compress(
        proj3, cos2, sin2, jnp.concatenate([pe_k, pe_k], axis=-1), jnp.concatenate([pe_v, pe_v], axis=-1),
        both_groups(w1_k), both_groups(w1_v), both_groups(w2_k), both_groups(w2_v))
    fox_tile = min(FOX_TILE, s)
    kf, vft, ks, vst, kw, vwt = _attn_prep(proj3, cos2, sin2, fox_tile, NSA_K_TILE)
    o_nsa = _nsa_attend(proj3, cos2, sin2, kcmp, vcmp_t, ks, vst, kw, vwt, min(NSA_Q_TILE, s), NSA_K_TILE)
    o_fox = _fox_attend(proj3, kf, vft, cum, cum_t, fox_tile)
    mem_kv = _mem_kv(mem, g_mem.reshape(1, d), w_mem_kv.astype(BF16))
    o_mem = _mem_attend(proj3, mem_kv)

    x1, hn, idx_slab, gate_slab = _merge_route(
        o_nsa.reshape(t, -1), o_fox.reshape(t, -1), o_mem.reshape(t, -1), proj, b_merge.reshape(1, -1),
        w_br_nsa.astype(BF16), w_br_fox.astype(BF16), w_br_mem.astype(BF16), w_out.astype(BF16),
        x2, g_ffn.reshape(1, d), w_router, b_router.reshape(1, -1))

    rank_slab, cnt8 = _moe_rank(idx_slab)
    counts = cnt8[0]
    padded = (counts + MOE_ROWS - 1) // MOE_ROWS * MOE_ROWS
    pad_end = jnp.cumsum(padded)
    pad_start = pad_end - padded
    n_assign = t * TOP_K
    nblk = -(-(n_assign + N_EXPERTS * (MOE_ROWS - 1)) // MOE_ROWS)
    n_slots = nblk * MOE_ROWS
    top_idx = idx_slab[:, :TOP_K]
    dest = (pad_start[top_idx] + rank_slab[:, :TOP_K]).reshape(-1).astype(jnp.int32)
    slot_tok = _moe_slot_tok(dest, n_slots)
    blk_start = jnp.arange(nblk, dtype=jnp.int32) * MOE_ROWS
    block_exp = jnp.minimum(jnp.sum(blk_start[:, None] >= pad_end[None, :], axis=1),
                            N_EXPERTS - 1).astype(jnp.int32)
    n_used = (pad_end[-1] // MOE_ROWS).astype(jnp.int32).reshape(1)

    e_ids = jnp.arange(N_EXPERTS, dtype=jnp.int32)
    later = (e_ids[None, :] > block_exp[:, None]) & (counts[None, :] > 0)
    next_exp = jnp.min(jnp.where(later, e_ids[None, :], N_EXPERTS), axis=1)
    next_exp = jnp.where(next_exp < N_EXPERTS, next_exp, -1).astype(jnp.int32)

    f = w_down.shape[1]
    bgu = b_gate_up.reshape(N_EXPERTS, 1, f, 2)
    ys = _moe_experts(block_exp, n_used, next_exp, slot_tok, hn, w_gate_up, w_down,
                      bgu[..., 0], bgu[..., 1], b_down.reshape(N_EXPERTS, 1, d))
    out = _moe_combine(dest, gate_slab, x1, g_final.reshape(1, d), ys)
    return out.reshape(b, s, d)


def kernel(x, mem, g_mix, w_in, b_forget, b_merge, nsa_pe_k, nsa_w1_k, nsa_w2_k, nsa_pe_v, nsa_w1_v, nsa_w2_v, g_mem, w_mem_kv, w_branch_nsa, w_branch_fox, w_branch_mem, w_out, g_ffn, w_router, b_router, w_gate_up, b_gate_up, w_down, b_down, g_final):
    assert g_mix.shape[0] == 1, "single-layer trunk"
    return _layer(x, mem, g_mix[0], w_in[0], b_forget[0], b_merge[0], nsa_pe_k[0], nsa_w1_k[0], nsa_w2_k[0],
                  nsa_pe_v[0], nsa_w1_v[0], nsa_w2_v[0], g_mem[0], w_mem_kv[0], w_branch_nsa[0],
                  w_branch_fox[0], w_branch_mem[0], w_out[0], g_ffn[0], w_router[0], b_router[0],
                  w_gate_up[0], b_gate_up[0], w_down[0], b_down[0], g_final)
```

```python
import functools

import jax
import jax.numpy as jnp
import numpy as np
from jax import lax
from jax.experimental import pallas as pl
from jax.experimental.pallas import tpu as pltpu

F32 = jnp.float32
BF16 = jnp.bfloat16

D_MODEL = 1024
HEAD_DIM = 64
ROPE_THETA = 10000.0
RMS_EPS = 1e-5
NEG_INF = -1e30
FORCE_SCORE = 1e4
NSA_HEADS = 8
NSA_GROUPS = 2
NSA_REP = NSA_HEADS // NSA_GROUPS
CMP_LEN = 32
CMP_STRIDE = 16
SEL_BLOCK = 64
SEL_TOPK = 16
WINDOW = 512
FOX_HEADS = 8
MEM_HEADS = 4
MEM_HEAD_DIM = 128
N_EXPERTS = 32
TOP_K = 4
SWIGLU_LIMIT = 7.0
SWIGLU_ALPHA = 1.702

LANES = 128
VMEM_LIMIT = 48 * 1024 * 1024

C_MERGE = 0
C_QA = 3072
C_QF = 3584
C_KF = 4096
C_VF = 4608
C_QM = 5120
C_NSAKV = 5632
C_SMALL = 6400
N_PROJ = 6656
GL_OFF = 0
FL_OFF = 24

MOE_ROWS = 256
NSA_Q_TILE = 512
NSA_K_TILE = 256
FOX_TILE = 512


def _cparams(sem, vmem=VMEM_LIMIT):
    return pltpu.CompilerParams(dimension_semantics=sem, vmem_limit_bytes=vmem)


def _dot(a, b):
    return jnp.dot(a, b, preferred_element_type=F32)


def _dot_nt(a, b):
    return lax.dot_general(a, b, (((1,), (1,)), ((), ())), preferred_element_type=F32)


def _rope(x, cos, sin_signed):
    w = x.shape[-1]
    lane = lax.broadcasted_iota(jnp.int32, x.shape, x.ndim - 1)
    first = (lane & (HEAD_DIM - 1)) < (HEAD_DIM // 2)
    rot = jnp.where(first, pltpu.roll(x, w - HEAD_DIM // 2, x.ndim - 1),
                    pltpu.roll(x, HEAD_DIM // 2, x.ndim - 1))
    return x * cos + rot * sin_signed


def _split3(x):
    hi = x.astype(BF16)
    r1 = x - hi.astype(F32)
    mid = r1.astype(BF16)
    lo = (r1 - mid.astype(F32)).astype(BF16)
    return hi, mid, lo


def _inproj_kernel(x_ref, g_ref, w_ref, o_ref, small_ref, hn_ref, *, small_tile, small_off):
    @pl.when(pl.program_id(1) == 0)
    def _():
        x = x_ref[...]
        ms = jnp.mean(x * x, axis=-1, keepdims=True)
        hn_ref[...] = (x * lax.rsqrt(ms + RMS_EPS) * g_ref[...]).astype(BF16)

    acc = _dot(hn_ref[...], w_ref[...])
    o_ref[...] = acc.astype(BF16)

    @pl.when(pl.program_id(1) == small_tile)
    def _():
        small_ref[...] = acc[:, small_off:small_off + LANES]


def _inproj(x2, g, w_bf16, tm=1024, tn=512):
    t, d = x2.shape
    n = w_bf16.shape[1]
    return pl.pallas_call(
        functools.partial(_inproj_kernel, small_tile=C_SMALL // tn, small_off=C_SMALL % tn),
        out_shape=(jax.ShapeDtypeStruct((t, n), BF16), jax.ShapeDtypeStruct((t, LANES), F32)),
        grid=(t // tm, n // tn),
        in_specs=[pl.BlockSpec((tm, d), lambda i, j: (i, 0)),
                  pl.BlockSpec((1, d), lambda i, j: (0, 0)),
                  pl.BlockSpec((d, tn), lambda i, j: (0, j))],
        out_specs=(pl.BlockSpec((tm, tn), lambda i, j: (i, j)), pl.BlockSpec((tm, LANES), lambda i, j: (i, 0))),
        scratch_shapes=[pltpu.VMEM((tm, d), BF16)],
        compiler_params=_cparams(("parallel", "arbitrary")),
        name="inproj",
    )(x2, g, w_bf16)


def _cum_kernel(s_ref, b_ref, c_ref, ct_ref, *, blk):
    s = s_ref.shape[1]
    z = s_ref[0] + b_ref[...]
    logf = jnp.minimum(z, 0.0) - jnp.log1p(jnp.exp(-jnp.abs(z)))
    r = lax.broadcasted_iota(jnp.int32, (blk, blk), 0)
    c = lax.broadcasted_iota(jnp.int32, (blk, blk), 1)
    tri = jnp.where(c <= r, 1.0, 0.0).astype(BF16)
    carry = jnp.zeros((1, LANES), F32)
    for i in range(s // blk):
        hi, mid, lo = _split3(logf[i * blk:(i + 1) * blk])
        loc = (_dot(tri, hi) + _dot(tri, mid)) + _dot(tri, lo)
        out = loc + carry
        c_ref[0, i * blk:(i + 1) * blk, :] = out
        carry = out[blk - 1:blk, :]
    ct_ref[0] = c_ref[0].T


def _fox_cum(small3, b_row, blk=256):
    b, s, _ = small3.shape
    return pl.pallas_call(
        functools.partial(_cum_kernel, blk=blk),
        out_shape=(jax.ShapeDtypeStruct((b, s, LANES), F32),
                   jax.ShapeDtypeStruct((b, LANES, s), F32)),
        grid=(b,),
        in_specs=[pl.BlockSpec((1, s, LANES), lambda i: (i, 0, 0)),
                  pl.BlockSpec((1, LANES), lambda i: (0, 0))],
        out_specs=(pl.BlockSpec((1, s, LANES), lambda i: (i, 0, 0)),
                   pl.BlockSpec((1, LANES, s), lambda i: (i, 0, 0))),
        compiler_params=_cparams(("parallel",)),
        name="fox_cum",
    )(small3, b_row)


def _cmp_kernel(kc_ref, vc_ref, cos_ref, sin_ref, pek_ref, pev_ref, w1k_ref, w1v_ref, w2k_ref, w2v_ref,
                ko_ref, vto_ref, kr_ref):
    s = kc_ref.shape[1]
    nb = s // CMP_STRIDE
    kr_ref[...] = _rope(kc_ref[0].astype(F32), cos_ref[...], sin_ref[...])

    def mlp(rows, pe_ref, w1_ref, w2_ref):
        pa = jnp.zeros((nb, LANES), F32)
        pb = jnp.zeros((nb, LANES), F32)
        for l in range(CMP_STRIDE):
            x = rows(l)
            pa = pa + _dot((x + pe_ref[l:l + 1, :]).astype(BF16), w1_ref[l])
            pb = pb + _dot((x + pe_ref[CMP_STRIDE + l:CMP_STRIDE + l + 1, :]).astype(BF16), w1_ref[CMP_STRIDE + l])
        z = pa + pltpu.roll(pb, nb - 1, 0)
        h = z * jax.nn.sigmoid(z)
        return _dot(h.astype(BF16), w2_ref[...])

    rows = lambda l: kr_ref[pl.ds(l, nb, stride=CMP_STRIDE), :]
    ko_ref[0] = mlp(rows, pek_ref, w1k_ref, w2k_ref).astype(BF16)
    kr_ref[...] = vc_ref[0].astype(F32)
    vcmp = mlp(rows, pev_ref, w1v_ref, w2v_ref)
    kr_ref[0:nb, :] = vcmp
    vto_ref[0] = kr_ref[0:nb, :].T.astype(BF16)


def _nsa_compress(proj3, cos2, sin2, pek, pev, w1k, w1v, w2k, w2v):
    b, s, _ = proj3.shape
    nb = s // CMP_STRIDE
    kvb = C_NSAKV // LANES
    full = lambda a: pl.BlockSpec(a.shape, lambda i: (0,) * a.ndim)
    return pl.pallas_call(
        _cmp_kernel,
        out_shape=(jax.ShapeDtypeStruct((b, nb, LANES), BF16), jax.ShapeDtypeStruct((b, LANES, nb), BF16)),
        grid=(b,),
        in_specs=[pl.BlockSpec((1, s, LANES), lambda i: (i, 0, kvb)),
                  pl.BlockSpec((1, s, LANES), lambda i: (i, 0, kvb + 1)),
                  full(cos2), full(sin2), full(pek), full(pev), full(w1k), full(w1v), full(w2k), full(w2v)],
        out_specs=(pl.BlockSpec((1, nb, LANES), lambda i: (i, 0, 0)),
                   pl.BlockSpec((1, LANES, nb), lambda i: (i, 0, 0))),
        scratch_shapes=[pltpu.VMEM((s, LANES), F32)],
        compiler_params=_cparams(("parallel",)),
        name="nsa_compress",
    )(proj3, proj3, cos2, sin2, pek, pev, w1k, w1v, w2k, w2v)


def _prep_kernel(kf_ref, vf_ref, ks_ref, vs_ref, kw_ref, vw_ref, cos_ref, sin_ref,
                 okf_ref, ovf_ref, oks_ref, ovs_ref, okw_ref, ovw_ref):
    okf_ref[0] = kf_ref[0]
    ovf_ref[0, 0] = vf_ref[0].astype(F32).T.astype(BF16)
    oks_ref[0] = _rope(ks_ref[0].astype(F32), cos_ref[...], sin_ref[...]).astype(BF16)
    okw_ref[0] = _rope(kw_ref[0].astype(F32), cos_ref[...], sin_ref[...]).astype(BF16)
    tn = ovs_ref.shape[3]
    for i in range(ovs_ref.shape[1]):
        ovs_ref[0, i] = vs_ref[0, i * tn:(i + 1) * tn, :].astype(F32).T.astype(BF16)
        ovw_ref[0, i] = vw_ref[0, i * tn:(i + 1) * tn, :].astype(F32).T.astype(BF16)


def _attn_prep(proj3, cos2, sin2, tk, tn):
    b, s, _ = proj3.shape
    wf = FOX_HEADS * HEAD_DIM
    kvb = C_NSAKV // LANES
    nk = s // tk
    sub = tk // tn
    col = lambda w, c: pl.BlockSpec((1, tk, w), lambda i, j: (i, j, c))
    rows = lambda w: pl.BlockSpec((1, tk, w), lambda i, j: (i, j, 0))
    ntile = pl.BlockSpec((1, sub, LANES, tn), lambda i, j: (i, j, 0, 0))
    return pl.pallas_call(
        _prep_kernel,
        out_shape=(jax.ShapeDtypeStruct((b, s, wf), BF16), jax.ShapeDtypeStruct((b, nk, wf, tk), BF16),
                   jax.ShapeDtypeStruct((b, s, LANES), BF16), jax.ShapeDtypeStruct((b, nk * sub, LANES, tn), BF16),
                   jax.ShapeDtypeStruct((b, s, LANES), BF16), jax.ShapeDtypeStruct((b, nk * sub, LANES, tn), BF16)),
        grid=(b, nk),
        in_specs=[col(wf, C_KF // wf), col(wf, C_VF // wf), col(LANES, kvb + 2), col(LANES, kvb + 3),
                  col(LANES, kvb + 4), col(LANES, kvb + 5),
                  pl.BlockSpec((tk, LANES), lambda i, j: (j, 0)), pl.BlockSpec((tk, LANES), lambda i, j: (j, 0))],
        out_specs=(rows(wf), pl.BlockSpec((1, 1, wf, tk), lambda i, j: (i, j, 0, 0)),
                   rows(LANES), ntile, rows(LANES), ntile),
        compiler_params=_cparams(("parallel", "parallel")),
        name="attn_prep",
    )(proj3, proj3, proj3, proj3, proj3, proj3, cos2, sin2)


def _online_update(s, vt, m_ref, l_ref, acc_ref, idx, mask=None):
    m_old = m_ref[idx]
    m_new = jnp.maximum(m_old, jnp.max(s, axis=0, keepdims=True))
    alpha = jnp.exp(m_old - m_new)
    p = jnp.exp(s - m_new)
    if mask is not None:
        p = jnp.where(mask, p, 0.0)
    l_ref[idx] = alpha * l_ref[idx] + jnp.sum(p, axis=0, keepdims=True)
    acc_ref[idx] = alpha * acc_ref[idx] + _dot(vt, p.astype(BF16))
    m_ref[idx] = m_new


def _padded_qt(q_ref, cos, sin, qt_ref, tq, heads_per_slot):
    qt_ref[...] = jnp.zeros(qt_ref.shape, qt_ref.dtype)
    n_heads = q_ref.shape[2] // HEAD_DIM
    for hp in range(n_heads // 2):
        q2 = q_ref[0, :, hp * LANES:(hp + 1) * LANES].astype(F32)
        if cos is not None:
            q2 = _rope(q2, cos, sin)
        qt = (q2 * (HEAD_DIM ** -0.5)).T.astype(BF16)
        for sub in range(2):
            h = 2 * hp + sub
            slot, r = h // heads_per_slot, h % heads_per_slot
            half = (slot % 2) if heads_per_slot > 1 else sub
            qt_ref[slot, half * HEAD_DIM:(half + 1) * HEAD_DIM, r * tq:(r + 1) * tq] = (
                qt[sub * HEAD_DIM:(sub + 1) * HEAD_DIM, :])


def _nsa_kernel(q_ref, cos_ref, sin_ref, kc_ref, vct_ref, ks_ref, vst_ref, kw_ref, vwt_ref, sm_ref, o_ref,
                qt_ref, oct_ref, m_ref, l_ref, acc_ref, *, tq, tk, n_sel):
    qi = pl.program_id(1)
    G, R = NSA_GROUPS, NSA_REP
    nb = kc_ref.shape[1]
    per_q = tq // tk
    _padded_qt(q_ref, cos_ref[...], sin_ref[...], qt_ref, tq, R)
    m_ref[...] = jnp.full(m_ref.shape, NEG_INF, F32)
    l_ref[...] = jnp.zeros(l_ref.shape, F32)
    acc_ref[...] = jnp.zeros(acc_ref.shape, F32)

    tpos = qi * tq + (lax.broadcasted_iota(jnp.int32, (nb, R * tq), 1) & (tq - 1))
    nrow = lax.broadcasted_iota(jnp.int32, (nb, R * tq), 0)
    cmask = (nrow * CMP_STRIDE + (CMP_LEN - 1) <= tpos) & (nrow < nb - 1)
    oj = lax.broadcasted_iota(jnp.int32, (n_sel, nb), 0)
    on = lax.broadcasted_iota(jnp.int32, (n_sel, nb), 1) * CMP_STRIDE
    overlap_t = jnp.where((on < (oj + 1) * SEL_BLOCK) & (on + CMP_LEN > oj * SEL_BLOCK), 1.0, 0.0).astype(BF16)
    blk = lax.broadcasted_iota(jnp.int32, (n_sel, tq), 0)
    trow = qi * tq + lax.broadcasted_iota(jnp.int32, (n_sel, tq), 1)
    cur = trow // SEL_BLOCK
    forced = (blk == 0) | (blk == cur) | (blk == cur - 1)
    future = blk * SEL_BLOCK > trow
    kc = kc_ref[0]
    for g in range(G):
        s = jnp.where(cmask, _dot(kc, qt_ref[g, 0:LANES, :]), NEG_INF)
        mx = jnp.max(s, axis=0, keepdims=True)
        e = jnp.where(cmask, jnp.exp(s - mx), 0.0)
        p = e / jnp.maximum(jnp.sum(e, axis=0, keepdims=True), 1e-30)
        oct_ref[g] = _dot(vct_ref[0, g * HEAD_DIM:(g + 1) * HEAD_DIM, :], p.astype(BF16))
        psum = p[:, 0:tq]
        for r in range(1, R):
            psum = psum + p[:, r * tq:(r + 1) * tq]
        hi, mid, lo = _split3(psum)
        imp = (_dot(overlap_t, hi) + _dot(overlap_t, mid)) + _dot(overlap_t, lo)
        imp = jnp.where(forced, FORCE_SCORE, jnp.where(future, -FORCE_SCORE, imp))
        rank = jnp.zeros((n_sel, tq), F32)
        for i in range(n_sel):
            ri = imp[i:i + 1, :]
            ahead = (ri > imp) | ((ri == imp) & (blk > i))
            rank = rank + jnp.where(ahead, 1.0, 0.0)
        bias = jnp.where(rank < float(min(SEL_TOPK, n_sel)), 0.0, NEG_INF).astype(BF16)
        for r in range(R):
            qt_ref[g, LANES:LANES + n_sel, r * tq:(r + 1) * tq] = bias

    qpos = qi * tq + (lax.broadcasted_iota(jnp.int32, (tk, R * tq), 1) & (tq - 1))
    krow = lax.broadcasted_iota(jnp.int32, (tk, R * tq), 0)
    er = lax.broadcasted_iota(jnp.int32, (tk, LANES), 0)
    ec = lax.broadcasted_iota(jnp.int32, (tk, LANES), 1)

    def sel_tile(j, diagonal):
        k = ks_ref[0, pl.ds(pl.multiple_of(j * tk, tk), tk), :]
        onehot = jnp.where((j * tk + er) // SEL_BLOCK == ec, 1.0, 0.0).astype(BF16)
        kx = jnp.concatenate([k, onehot], axis=1)
        for g in range(G):
            s = _dot(kx, qt_ref[g])
            if diagonal:
                s = jnp.where(j * tk + krow <= qpos, s, NEG_INF)
            _online_update(s, vst_ref[0, j, g * HEAD_DIM:(g + 1) * HEAD_DIM, :], m_ref, l_ref, acc_ref, g)

    def sel_body(j, c):
        sel_tile(j, False)
        return c

    lax.fori_loop(0, qi * per_q, sel_body, 0)
    for dd in range(per_q):
        sel_tile(qi * per_q + dd, True)

    for back in range(per_q + WINDOW // tk - 1, -1, -1):
        @pl.when((qi + 1) * per_q - 1 - back >= 0)
        def _(back=back):
            j = (qi + 1) * per_q - 1 - back
            k = kw_ref[0, pl.ds(pl.multiple_of(j * tk, tk), tk), :]
            diff = qpos - (j * tk + krow)
            mask = (diff >= 0) & (diff < WINDOW)
            for g in range(G):
                s = jnp.where(mask, _dot(k, qt_ref[g, 0:LANES, :]), NEG_INF)
                _online_update(s, vwt_ref[0, j, g * HEAD_DIM:(g + 1) * HEAD_DIM, :], m_ref, l_ref, acc_ref,
                               G + g, mask)

    gates_t = jax.nn.sigmoid(sm_ref[0]).T
    rows = []
    for h in range(NSA_HEADS):
        g, r = h // R, h % R
        cols = slice(r * tq, (r + 1) * tq)
        gate = lambda br: gates_t[GL_OFF + 3 * h + br:GL_OFF + 3 * h + br + 1, :]
        o_sel = acc_ref[g, :, cols] / jnp.maximum(l_ref[g, :, cols], 1e-30)
        o_win = acc_ref[G + g, :, cols] / jnp.maximum(l_ref[G + g, :, cols], 1e-30)
        rows.append(gate(0) * oct_ref[g, :, cols] + gate(1) * o_sel + gate(2) * o_win)
    o_ref[0] = jnp.concatenate(rows, axis=0).T


def _nsa_attend(proj3, small3, cos2, sin2, kcmp, vcmp_t, ks, vst, kw, vwt, tq, tk):
    b, s, _ = proj3.shape
    n_sel = s // SEL_BLOCK
    wq = NSA_HEADS * HEAD_DIM
    whole = lambda a: pl.BlockSpec((1,) + a.shape[1:], lambda i, j: (i,) + (0,) * (a.ndim - 1))
    return pl.pallas_call(
        functools.partial(_nsa_kernel, tq=tq, tk=tk, n_sel=n_sel),
        out_shape=jax.ShapeDtypeStruct((b, s, wq), F32),
        grid=(b, s // tq),
        in_specs=[pl.BlockSpec((1, tq, wq), lambda i, j: (i, j, C_QA // wq)),
                  pl.BlockSpec((tq, LANES), lambda i, j: (j, 0)),
                  pl.BlockSpec((tq, LANES), lambda i, j: (j, 0)),
                  whole(kcmp), whole(vcmp_t), whole(ks), whole(vst), whole(kw), whole(vwt),
                  pl.BlockSpec((1, tq, LANES), lambda i, j: (i, j, 0))],
        out_specs=pl.BlockSpec((1, tq, wq), lambda i, j: (i, j, 0)),
        scratch_shapes=[pltpu.VMEM((NSA_GROUPS, 2 * LANES, NSA_REP * tq), BF16),
                        pltpu.VMEM((NSA_GROUPS, HEAD_DIM, NSA_REP * tq), F32),
                        pltpu.VMEM((2 * NSA_GROUPS, 1, NSA_REP * tq), F32),
                        pltpu.VMEM((2 * NSA_GROUPS, 1, NSA_REP * tq), F32),
                        pltpu.VMEM((2 * NSA_GROUPS, HEAD_DIM, NSA_REP * tq), F32)],
        compiler_params=_cparams(("parallel", "parallel")),
        name="nsa_attend",
    )(proj3, cos2, sin2, kcmp, vcmp_t, ks, vst, kw, vwt, small3)


def _fox_kernel(q_ref, k_ref, vt_ref, cq_ref, ck_ref, o_ref, qt_ref, m_ref, l_ref, acc_ref, *, tq):
    qi = pl.program_id(1)
    tk = tq
    _padded_qt(q_ref, None, None, qt_ref, tq, 1)
    m_ref[...] = jnp.full(m_ref.shape, NEG_INF, F32)
    l_ref[...] = jnp.zeros(l_ref.shape, F32)
    acc_ref[...] = jnp.zeros(acc_ref.shape, F32)
    cq = cq_ref[0]
    causal = lax.broadcasted_iota(jnp.int32, (tk, tq), 0) <= lax.broadcasted_iota(jnp.int32, (tk, tq), 1)

    def tile(j, diagonal):
        rows = pl.ds(pl.multiple_of(j * tk, tk), tk)
        ck = ck_ref[0, rows, :]
        for h in range(FOX_HEADS):
            hp = h // 2
            s = _dot(k_ref[0, rows, hp * LANES:(hp + 1) * LANES], qt_ref[h])
            s = (s - ck[:, FL_OFF + h:FL_OFF + h + 1]) + cq[FL_OFF + h:FL_OFF + h + 1, :]
            if diagonal:
                s = jnp.where(causal, s, NEG_INF)
            _online_update(s, vt_ref[0, j, h * HEAD_DIM:(h + 1) * HEAD_DIM, :], m_ref, l_ref, acc_ref, h)

    def body(j, c):
        tile(j, False)
        return c

    lax.fori_loop(0, qi, body, 0)
    tile(qi, True)
    rows = [acc_ref[h] / jnp.maximum(l_ref[h], 1e-30) for h in range(FOX_HEADS)]
    o_ref[0] = jnp.concatenate(rows, axis=0).T


def _fox_attend(proj3, kf, vft, cum, cum_t, tq):
    b, s, _ = proj3.shape
    w = FOX_HEADS * HEAD_DIM
    whole = lambda a: pl.BlockSpec((1,) + a.shape[1:], lambda i, j: (i,) + (0,) * (a.ndim - 1))
    return pl.pallas_call(
        functools.partial(_fox_kernel, tq=tq),
        out_shape=jax.ShapeDtypeStruct((b, s, w), F32),
        grid=(b, s // tq),
        in_specs=[pl.BlockSpec((1, tq, w), lambda i, j: (i, j, C_QF // w)),
                  whole(kf), whole(vft),
                  pl.BlockSpec((1, LANES, tq), lambda i, j: (i, 0, j)),
                  whole(cum)],
        out_specs=pl.BlockSpec((1, tq, w), lambda i, j: (i, j, 0)),
        scratch_shapes=[pltpu.VMEM((FOX_HEADS, LANES, tq), BF16),
                        pltpu.VMEM((FOX_HEADS, 1, tq), F32),
                        pltpu.VMEM((FOX_HEADS, 1, tq), F32),
                        pltpu.VMEM((FOX_HEADS, HEAD_DIM, tq), F32)],
        compiler_params=_cparams(("parallel", "parallel")),
        name="fox_attend",
    )(proj3, kf, vft, cum_t, cum)


def _memkv_kernel(m_ref, g_ref, w_ref, o_ref):
    x = m_ref[0]
    ms = jnp.mean(x * x, axis=-1, keepdims=True)
    hn = (x * lax.rsqrt(ms + RMS_EPS) * g_ref[...]).astype(BF16)
    o_ref[0] = _dot(hn, w_ref[...]).astype(BF16)


def _mem_kv(mem, g, w_bf16):
    b, m, d = mem.shape
    n = w_bf16.shape[1]
    return pl.pallas_call(
        _memkv_kernel,
        out_shape=jax.ShapeDtypeStruct((b, m, n), BF16),
        grid=(b,),
        in_specs=[pl.BlockSpec((1, m, d), lambda i: (i, 0, 0)),
                  pl.BlockSpec((1, d), lambda i: (0, 0)),
                  pl.BlockSpec((d, n), lambda i: (0, 0))],
        out_specs=pl.BlockSpec((1, m, n), lambda i: (i, 0, 0)),
        compiler_params=_cparams(("parallel",)),
        name="mem_kv",
    )(mem, g, w_bf16)


def _memattn_kernel(q_ref, kv_ref, o_ref):
    w = MEM_HEADS * MEM_HEAD_DIM
    for h in range(MEM_HEADS):
        lo, hi = h * MEM_HEAD_DIM, (h + 1) * MEM_HEAD_DIM
        s = _dot_nt(q_ref[0, :, lo:hi].astype(BF16), kv_ref[0, :, lo:hi]) * (MEM_HEAD_DIM ** -0.5)
        e = jnp.exp(s - jnp.max(s, axis=-1, keepdims=True))
        p = e / jnp.sum(e, axis=-1, keepdims=True)
        o_ref[0, :, lo:hi] = _dot(p.astype(BF16), kv_ref[0, :, w + lo:w + hi])


def _mem_attend(proj3, kv, tq=512):
    b, s, _ = proj3.shape
    m = kv.shape[1]
    w = MEM_HEADS * MEM_HEAD_DIM
    return pl.pallas_call(
        _memattn_kernel,
        out_shape=jax.ShapeDtypeStruct((b, s, w), F32),
        grid=(b, s // tq),
        in_specs=[pl.BlockSpec((1, tq, w), lambda i, j: (i, j, C_QM // w)),
                  pl.BlockSpec((1, m, 2 * w), lambda i, j: (i, 0, 0))],
        out_specs=pl.BlockSpec((1, tq, w), lambda i, j: (i, j, 0)),
        compiler_params=_cparams(("parallel", "parallel")),
        name="mem_attend",
    )(proj3, kv)


def _merge_kernel(on_ref, of_ref, om_ref, ln_ref, lf_ref, lm_ref, bm_ref, wn_ref, wf_ref, wm_ref,
                  wo_ref, x_ref, g_ref, wr_ref, br_ref, x1_ref, hn_ref, idx_ref, gate_ref):
    d = x_ref.shape[1]

    def branch(o_ref, l_ref, w_ref, k):
        gate = jax.nn.sigmoid(l_ref[...] + bm_ref[:, k * d:(k + 1) * d])
        return gate * _dot(o_ref[...].astype(BF16), w_ref[...])

    merged = branch(on_ref, ln_ref, wn_ref, 0) + branch(of_ref, lf_ref, wf_ref, 1) + branch(om_ref, lm_ref, wm_ref, 2)
    x1 = x_ref[...] + _dot(merged.astype(BF16), wo_ref[...])
    x1_ref[...] = x1
    ms = jnp.mean(x1 * x1, axis=-1, keepdims=True)
    hn = x1 * lax.rsqrt(ms + RMS_EPS) * g_ref[...]
    hn_ref[...] = hn
    h_hi = hn.astype(BF16)
    h_lo = (hn - h_hi.astype(F32)).astype(BF16)
    wr = wr_ref[...]
    w_hi = wr.astype(BF16)
    w_lo = (wr - w_hi.astype(F32)).astype(BF16)
    logits = (_dot(h_hi, w_hi) + (_dot(h_lo, w_hi) + _dot(h_hi, w_lo))) + br_ref[...]
    tm, ne = logits.shape
    lane_e = lax.broadcasted_iota(jnp.int32, (tm, ne), 1)
    lane = lax.broadcasted_iota(jnp.int32, (tm, LANES), 1)
    idx_slab = jnp.zeros((tm, LANES), jnp.int32)
    val_slab = jnp.zeros((tm, LANES), F32)
    work = logits
    vals = []
    for k in range(TOP_K):
        m = jnp.max(work, axis=-1, keepdims=True)
        idx = jnp.min(jnp.where(work == m, lane_e, ne), axis=-1, keepdims=True)
        work = jnp.where(lane_e == idx, -jnp.inf, work)
        idx_slab = jnp.where(lane == k, idx, idx_slab)
        vals.append(m)
    es = [jnp.exp(v - vals[0]) for v in vals]
    tot = es[0]
    for e in es[1:]:
        tot = tot + e
    for k in range(TOP_K):
        val_slab = jnp.where(lane == k, es[k] / tot, val_slab)
    idx_ref[...] = idx_slab
    gate_ref[...] = val_slab


def _merge_route(o_nsa, o_fox, o_mem, proj, b_merge, wn, wf, wm, wo, x2, g_ffn, w_router, b_router, tm=512):
    t, d = x2.shape
    wb = o_nsa.shape[1]
    row = lambda w: pl.BlockSpec((tm, w), lambda i: (i, 0))
    full = lambda a: pl.BlockSpec(a.shape, lambda i: (0,) * a.ndim)
    return pl.pallas_call(
        _merge_kernel,
        out_shape=(jax.ShapeDtypeStruct((t, d), F32), jax.ShapeDtypeStruct((t, d), F32),
                   jax.ShapeDtypeStruct((t, LANES), jnp.int32), jax.ShapeDtypeStruct((t, LANES), F32)),
        grid=(t // tm,),
        in_specs=[row(wb), row(wb), row(wb),
                  pl.BlockSpec((tm, d), lambda i: (i, 0)),
                  pl.BlockSpec((tm, d), lambda i: (i, 1)),
                  pl.BlockSpec((tm, d), lambda i: (i, 2)),
                  full(b_merge), full(wn), full(wf), full(wm), full(wo),
                  row(d), full(g_ffn), full(w_router), full(b_router)],
        out_specs=(row(d), row(d), row(LANES), row(LANES)),
        compiler_params=_cparams(("parallel",)),
        name="merge_route",
    )(o_nsa, o_fox, o_mem, proj, proj, proj, b_merge, wn, wf, wm, wo, x2, g_ffn, w_router, b_router)


def _rank_kernel(idx_ref, rank_ref, cnt_ref, carry_ref, *, tm):
    @pl.when(pl.program_id(0) == 0)
    def _():
        carry_ref[...] = jnp.zeros(carry_ref.shape, F32)

    idx = idx_ref[...]
    lane_e = lax.broadcasted_iota(jnp.int32, (tm, N_EXPERTS), 1)
    hots = [jnp.where(idx[:, k:k + 1] == lane_e, 1.0, 0.0) for k in range(TOP_K)]
    cnt = hots[0]
    for hk in hots[1:]:
        cnt = cnt + hk
    r = lax.broadcasted_iota(jnp.int32, (tm, tm), 0)
    c = lax.broadcasted_iota(jnp.int32, (tm, tm), 1)
    strict = jnp.where(c < r, 1.0, 0.0).astype(BF16)
    before = _dot(strict, cnt.astype(BF16)) + carry_ref[...]
    lane = lax.broadcasted_iota(jnp.int32, (tm, LANES), 1)
    slab = jnp.zeros((tm, LANES), F32)
    for k in range(TOP_K):
        slab = jnp.where(lane == k, jnp.sum(hots[k] * before, axis=-1, keepdims=True), slab)
    rank_ref[...] = slab.astype(jnp.int32)
    total = carry_ref[...] + jnp.sum(cnt, axis=0, keepdims=True)
    carry_ref[...] = total
    cnt_ref[...] = jnp.broadcast_to(total, cnt_ref.shape).astype(jnp.int32)


def _moe_rank(idx_slab, tm=256):
    t = idx_slab.shape[0]
    return pl.pallas_call(
        functools.partial(_rank_kernel, tm=tm),
        out_shape=(jax.ShapeDtypeStruct((t, LANES), jnp.int32),
                   jax.ShapeDtypeStruct((8, N_EXPERTS), jnp.int32)),
        grid=(t // tm,),
        in_specs=[pl.BlockSpec((tm, LANES), lambda i: (i, 0))],
        out_specs=(pl.BlockSpec((tm, LANES), lambda i: (i, 0)),
                   pl.BlockSpec((8, N_EXPERTS), lambda i: (0, 0))),
        scratch_shapes=[pltpu.VMEM((1, N_EXPERTS), F32)],
        compiler_params=_cparams(("arbitrary",)),
        name="moe_rank",
    )(idx_slab)


def _slot_tok_kernel(dest_ref, fill_ref, o_ref, *, steps, unroll):
    i = pl.program_id(0)
    shift = TOP_K.bit_length() - 1
    assert TOP_K == 1 << shift
    n_put = dest_ref.shape[0] // steps
    n_fill = fill_ref.shape[0] // steps

    def put(j, c):
        a = i * n_put + j
        o_ref[dest_ref[a]] = lax.shift_right_logical(a, shift)
        return c

    lax.fori_loop(0, n_put, put, 0, unroll=unroll)

    def fill(j, c):
        o_ref[fill_ref[i * n_fill + j]] = 0
        return c

    lax.fori_loop(0, n_fill, fill, 0, unroll=unroll)


def _moe_slot_tok(dest_flat, fill_slots, steps=32):
    n_slots = dest_flat.shape[0] + fill_slots.shape[0]
    assert dest_flat.shape[0] % steps == 0 and fill_slots.shape[0] % steps == 0
    return pl.pallas_call(
        functools.partial(_slot_tok_kernel, steps=steps, unroll=8),
        out_shape=jax.ShapeDtypeStruct((n_slots,), jnp.int32),
        grid_spec=pltpu.PrefetchScalarGridSpec(
            num_scalar_prefetch=2, grid=(steps,), in_specs=[],
            out_specs=pl.BlockSpec(memory_space=pltpu.SMEM)),
        compiler_params=_cparams(("arbitrary",)),
        name="moe_slot_tok",
    )(dest_flat, fill_slots)


def _expert_kernel(be_ref, nu_ref, nx_ref, tok_ref, hn_ref, wgu_ref, wdn_ref, bg_ref, bu_ref, bd_ref, o_ref,
                   buf_ref, act_ref, sgu_ref, sdn_ref, wg_ref, wu_ref, wd_ref, sem, wsem, *, unroll, chunk):
    i = pl.program_id(0)
    n_used = nu_ref[0]
    e = be_ref[i]
    f = wg_ref.shape[1]
    d = wd_ref.shape[1]

    def row_copy(tok, r, s):
        return pltpu.make_async_copy(hn_ref.at[pl.ds(tok, 1)], buf_ref.at[s, pl.ds(r, 1)], sem.at[s])

    def weight_copies(ex):
        return (pltpu.make_async_copy(wgu_ref.at[ex], sgu_ref, wsem.at[0]),
                pltpu.make_async_copy(wdn_ref.at[ex], sdn_ref, wsem.at[1]))

    def wait_rows(s):
        def wait(r, c):
            row_copy(0, 0, s).wait()
            return c

        lax.fori_loop(0, MOE_ROWS, wait, 0, unroll=unroll)

    @pl.when((i == 0) & (n_used > 0))
    def _():
        for cp in weight_copies(e):
            cp.start(priority=1)

        def body(r2, c):
            for j in range(2):
                row_copy(tok_ref[2 * r2 + j], 2 * r2 + j, 0).start()
            return c

        lax.fori_loop(0, MOE_ROWS // 2, body, 0, unroll=unroll)

    first = (i == 0) | (e != be_ref[jnp.maximum(i - 1, 0)])

    @pl.when((i < n_used) & first)
    def _():
        for cp in weight_copies(e):
            cp.wait()
        half = chunk // 2
        pr = lax.broadcasted_iota(jnp.int32, (chunk, chunk), 0)
        pc = lax.broadcasted_iota(jnp.int32, (chunk, chunk), 1)
        perm = jnp.where(pr == jnp.where(pc < half, 2 * pc, 2 * (pc - half) + 1), 1.0, 0.0).astype(BF16)
        for c in range(2 * f // chunk):
            sep = _dot(sgu_ref[:, c * chunk:(c + 1) * chunk].astype(BF16), perm).astype(BF16)
            wg_ref[:, c * half:(c + 1) * half] = sep[:, :half]
            wu_ref[:, c * half:(c + 1) * half] = sep[:, half:]
        wd_ref[...] = sdn_ref[...].astype(BF16)

        @pl.when(nx_ref[i] >= 0)
        def _():
            for cp in weight_copies(nx_ref[i]):
                cp.start(priority=1)

    @pl.when(i < n_used)
    def _():
        cur = i % 2
        nxt = 1 - cur
        base = jnp.minimum(i + 1, n_used - 1) * MOE_ROWS
        for r in range(MOE_ROWS):
            row_copy(tok_ref[base + r], r, nxt).start()
        wait_rows(cur)
        x = buf_ref[cur].astype(BF16)
        for c in range(f // chunk):
            cols = slice(c * chunk, (c + 1) * chunk)
            g = jnp.minimum(_dot(x, wg_ref[:, cols]) + bg_ref[0, :, cols], SWIGLU_LIMIT)
            u = jnp.clip(_dot(x, wu_ref[:, cols]) + bu_ref[0, :, cols], -SWIGLU_LIMIT, SWIGLU_LIMIT)
            act_ref[:, cols] = ((u + 1.0) * g * jax.nn.sigmoid(SWIGLU_ALPHA * g)).astype(BF16)
        act = act_ref[...]
        for c in range(d // chunk):
            cols = slice(c * chunk, (c + 1) * chunk)
            o_ref[:, cols] = _dot(act, wd_ref[:, cols]) + bd_ref[0, :, cols]

        @pl.when(i == n_used - 1)
        def _():
            wait_rows(nxt)

    @pl.when(i >= n_used)
    def _():
        o_ref[...] = jnp.zeros(o_ref.shape, o_ref.dtype)


def _moe_experts(block_exp, n_used, next_exp, slot_tok, hn, w_gate_up, w_down, bg, bu, bd):
    n_slots = slot_tok.shape[0]
    d = hn.shape[1]
    f = w_down.shape[1]
    nblk = n_slots // MOE_ROWS
    bspec = lambda w: pl.BlockSpec((1, 1, w), lambda i, be, nu, nx, tk: (be[i], 0, 0))
    return pl.pallas_call(
        functools.partial(_expert_kernel, unroll=8, chunk=256),
        out_shape=jax.ShapeDtypeStruct((n_slots, d), F32),
        grid_spec=pltpu.PrefetchScalarGridSpec(
            num_scalar_prefetch=4,
            grid=(nblk,),
            in_specs=[pl.BlockSpec(memory_space=pl.ANY), pl.BlockSpec(memory_space=pl.ANY),
                      pl.BlockSpec(memory_space=pl.ANY), bspec(f), bspec(f), bspec(d)],
            out_specs=pl.BlockSpec((MOE_ROWS, d), lambda i, be, nu, nx, tk: (i, 0)),
            scratch_shapes=[pltpu.VMEM((2, MOE_ROWS, d), hn.dtype), pltpu.VMEM((MOE_ROWS, f), BF16),
                            pltpu.VMEM((d, 2 * f), F32), pltpu.VMEM((f, d), F32),
                            pltpu.VMEM((d, f), BF16), pltpu.VMEM((d, f), BF16), pltpu.VMEM((f, d), BF16),
                            pltpu.SemaphoreType.DMA((2,)), pltpu.SemaphoreType.DMA((2,))]),
        compiler_params=_cparams(("arbitrary",)),
        name="moe_experts",
    )(block_exp, n_used, next_exp, slot_tok, hn, w_gate_up, w_down, bg, bu, bd)


def _combine_kernel(dest_ref, gate_ref, x1_ref, g_ref, ys_ref, o_ref, buf_ref, sem, *, tm, unroll):
    i = pl.program_id(0)
    n = pl.num_programs(0)

    def row_copy(slot_row, k, r, s):
        return pltpu.make_async_copy(ys_ref.at[pl.ds(slot_row, 1)], buf_ref.at[s, k, pl.ds(r, 1)], sem.at[s])

    def issue(step, s):
        base = step * (tm * TOP_K)
        for r in range(tm):
            for k in range(TOP_K):
                row_copy(dest_ref[base + r * TOP_K + k], k, r, s).start()

    @pl.when(i == 0)
    def _():
        issue(0, 0)

    @pl.when(i + 1 < n)
    def _():
        issue(i + 1, (i + 1) % 2)

    cur = i % 2

    def wait(r, c):
        row_copy(0, 0, 0, cur).wait()
        return c

    lax.fori_loop(0, tm * TOP_K, wait, 0, unroll=unroll)

    gate = gate_ref[...]
    y = x1_ref[...]
    for k in range(TOP_K):
        y = y + gate[:, k:k + 1] * buf_ref[cur, k]
    ms = jnp.mean(y * y, axis=-1, keepdims=True)
    o_ref[...] = y * lax.rsqrt(ms + RMS_EPS) * g_ref[...]


def _moe_combine(dest_flat, gate_slab, x1, g_final, ys, tm=128):
    t, d = x1.shape
    return pl.pallas_call(
        functools.partial(_combine_kernel, tm=tm, unroll=8),
        out_shape=jax.ShapeDtypeStruct((t, d), F32),
        grid_spec=pltpu.PrefetchScalarGridSpec(
            num_scalar_prefetch=1,
            grid=(t // tm,),
            in_specs=[pl.BlockSpec((tm, LANES), lambda i, dst: (i, 0)),
                      pl.BlockSpec((tm, d), lambda i, dst: (i, 0)),
                      pl.BlockSpec((1, d), lambda i, dst: (0, 0)),
                      pl.BlockSpec(memory_space=pl.ANY)],
            out_specs=pl.BlockSpec((tm, d), lambda i, dst: (i, 0)),
            scratch_shapes=[pltpu.VMEM((2, TOP_K, tm, d), F32), pltpu.SemaphoreType.DMA((2,))]),
        compiler_params=_cparams(("arbitrary",)),
        name="moe_combine",
    )(dest_flat, gate_slab, x1, g_final, ys)


def _rope_tables(s):
    inv = ROPE_THETA ** (-jnp.arange(0, HEAD_DIM, 2, dtype=F32) / HEAD_DIM)
    ang = jnp.arange(s, dtype=F32)[:, None] * inv[None, :]
    cos, sin = jnp.cos(ang), jnp.sin(ang)
    return jnp.concatenate([cos, cos], axis=-1), jnp.concatenate([-sin, sin], axis=-1)


def _permute_w_in(w):
    d = w.shape[0]
    pad = N_PROJ - (C_SMALL + 32)
    cols = [w[:, 3360:6432], w[:, 0:512], w[:, 1304:1816], w[:, 1816:2328], w[:, 2328:2840],
            w[:, 2848:3360], w[:, 512:1280], w[:, 1280:1304], w[:, 2840:2848], jnp.zeros((d, pad), w.dtype)]
    return jnp.concatenate(cols, axis=1).astype(BF16)


def _layer(x, mem, g_mix, w_in, b_forget, b_merge, pe_k, w1_k, w2_k, pe_v, w1_v, w2_v, g_mem, w_mem_kv,
           w_br_nsa, w_br_fox, w_br_mem, w_out, g_ffn, w_router, b_router, w_gate_up, b_gate_up,
           w_down, b_down, g_final):
    b, s, d = x.shape
    t = b * s
    x2 = x.reshape(t, d)
    cos64, sin64 = _rope_tables(s)
    cos2 = jnp.concatenate([cos64, cos64], axis=-1)
    sin2 = jnp.concatenate([sin64, sin64], axis=-1)

    proj, small = _inproj(x2, g_mix.reshape(1, d), _permute_w_in(w_in), tm=min(1024, t))
    proj3 = proj.reshape(b, s, N_PROJ)
    small3 = small.reshape(b, s, LANES)
    b_row = jnp.zeros((1, LANES), F32).at[0, FL_OFF:FL_OFF + FOX_HEADS].set(b_forget)
    cum, cum_t = _fox_cum(small3, b_row)

    def both_groups(w):
        z = jnp.zeros_like(w)
        return jnp.concatenate([jnp.concatenate([w, z], axis=-1), jnp.concatenate([z, w], axis=-1)], axis=-2).astype(BF16)

    kcmp, vcmp_t = _nsa_compress(
        proj3, cos2, sin2, jnp.concatenate([pe_k, pe_k], axis=-1), jnp.concatenate([pe_v, pe_v], axis=-1),
        both_groups(w1_k), both_groups(w1_v), both_groups(w2_k), both_groups(w2_v))
    fox_tile = min(FOX_TILE, s)
    kf, vft, ks, vst, kw, vwt = _attn_prep(proj3, cos2, sin2, fox_tile, NSA_K_TILE)
    o_nsa = _nsa_attend(proj3, small3, cos2, sin2, kcmp, vcmp_t, ks, vst, kw, vwt, min(NSA_Q_TILE, s), NSA_K_TILE)
    o_fox = _fox_attend(proj3, kf, vft, cum, cum_t, fox_tile)
    mem_kv = _mem_kv(mem, g_mem.reshape(1, d), w_mem_kv.astype(BF16))
    o_mem = _mem_attend(proj3, mem_kv)

    x1, hn, idx_slab, gate_slab = _merge_route(
        o_nsa.reshape(t, -1), o_fox.reshape(t, -1), o_mem.reshape(t, -1), proj, b_merge.reshape(1, -1),
        w_br_nsa.astype(BF16), w_br_fox.astype(BF16), w_br_mem.astype(BF16), w_out.astype(BF16),
        x2, g_ffn.reshape(1, d), w_router, b_router.reshape(1, -1))

    rank_slab, cnt8 = _moe_rank(idx_slab)
    counts = cnt8[0]
    padded = (counts + MOE_ROWS - 1) // MOE_ROWS * MOE_ROWS
    pad_end = jnp.cumsum(padded)
    pad_start = pad_end - padded
    n_assign = t * TOP_K
    nblk = -(-(n_assign + N_EXPERTS * (MOE_ROWS - 1)) // MOE_ROWS)
    n_slots = nblk * MOE_ROWS
    top_idx = idx_slab[:, :TOP_K]
    dest = (pad_start[top_idx] + rank_slab[:, :TOP_K]).reshape(-1).astype(jnp.int32)
    gap = padded - counts
    gap_end = jnp.cumsum(gap)
    jf = jnp.arange(n_slots - n_assign, dtype=jnp.int32)
    e_of = jnp.minimum(jnp.sum(jf[:, None] >= gap_end[None, :], axis=1), N_EXPERTS - 1)
    fill = jnp.where(jf < gap_end[-1], (pad_start + counts)[e_of] + (jf - (gap_end - gap)[e_of]),
                     pad_end[-1] + (jf - gap_end[-1])).astype(jnp.int32)
    slot_tok = _moe_slot_tok(dest, fill)
    blk_start = jnp.arange(nblk, dtype=jnp.int32) * MOE_ROWS
    block_exp = jnp.minimum(jnp.sum(blk_start[:, None] >= pad_end[None, :], axis=1),
                            N_EXPERTS - 1).astype(jnp.int32)
    n_used = (pad_end[-1] // MOE_ROWS).astype(jnp.int32).reshape(1)

    e_ids = jnp.arange(N_EXPERTS, dtype=jnp.int32)
    later = (e_ids[None, :] > block_exp[:, None]) & (counts[None, :] > 0)
    next_exp = jnp.min(jnp.where(later, e_ids[None, :], N_EXPERTS), axis=1)
    next_exp = jnp.where(next_exp < N_EXPERTS, next_exp, -1).astype(jnp.int32)

    f = w_down.shape[1]
    bgu = b_gate_up.reshape(N_EXPERTS, 1, f, 2)
    ys = _moe_experts(block_exp, n_used, next_exp, slot_tok, hn, w_gate_up, w_down,
                      bgu[..., 0], bgu[..., 1], b_down.reshape(N_EXPERTS, 1, d))
    out = _moe_combine(dest, gate_slab, x1, g_final.reshape(1, d), ys)
    return out.reshape(b, s, d)


def kernel(x, mem, g_mix, w_in, b_forget, b_merge, nsa_pe_k, nsa_w1_k, nsa_w2_k, nsa_pe_v, nsa_w1_v, nsa_w2_v, g_mem, w_mem_kv, w_branch_nsa, w_branch_fox, w_branch_mem, w_out, g_ffn, w_router, b_router, w_gate_up, b_gate_up, w_down, b_down, g_final):
    assert g_mix.shape[0] == 1, "single-layer trunk"
    return _layer(x, mem, g_mix[0], w_in[0], b_forget[0], b_merge[0], nsa_pe_k[0], nsa_w1_k[0], nsa_w2_k[0],
                  nsa_pe_v[0], nsa_w1_v[0], nsa_w2_v[0], g_mem[0], w_mem_kv[0], w_branch_nsa[0],
                  w_branch_fox[0], w_branch_mem[0], w_out[0], g_ffn[0], w_router[0], b_router[0],
                  w_gate_up[0], b_gate_up[0], w_down[0], b_down[0], g_final)
```

```python
import functools

import jax
import jax.numpy as jnp
import numpy as np
from jax import lax
from jax.experimental import pallas as pl
from jax.experimental.pallas import tpu as pltpu

F32 = jnp.float32
BF16 = jnp.bfloat16

D_MODEL = 1024
HEAD_DIM = 64
ROPE_THETA = 10000.0
RMS_EPS = 1e-5
NEG_INF = -1e30
FORCE_SCORE = 1e4
NSA_HEADS = 8
NSA_GROUPS = 2
NSA_REP = NSA_HEADS // NSA_GROUPS
CMP_LEN = 32
CMP_STRIDE = 16
SEL_BLOCK = 64
SEL_TOPK = 16
WINDOW = 512
FOX_HEADS = 8
MEM_HEADS = 4
MEM_HEAD_DIM = 128
N_EXPERTS = 32
TOP_K = 4
SWIGLU_LIMIT = 7.0
SWIGLU_ALPHA = 1.702

LANES = 128
VMEM_LIMIT = 48 * 1024 * 1024

C_MERGE = 0
C_QA = 3072
C_QF = 3584
C_KF = 4096
C_VF = 4608
C_QM = 5120
C_NSAKV = 5632
C_SMALL = 6400
N_PROJ = 6656
GL_OFF = 0
FL_OFF = 24

MOE_ROWS = 256
NSA_Q_TILE = 512
NSA_K_TILE = 256
FOX_TILE = 512


def _cparams(sem, vmem=VMEM_LIMIT):
    return pltpu.CompilerParams(dimension_semantics=sem, vmem_limit_bytes=vmem)


def _dot(a, b):
    return jnp.dot(a, b, preferred_element_type=F32)


def _dot_nt(a, b):
    return lax.dot_general(a, b, (((1,), (1,)), ((), ())), preferred_element_type=F32)


def _rope(x, cos, sin_signed):
    w = x.shape[-1]
    lane = lax.broadcasted_iota(jnp.int32, x.shape, x.ndim - 1)
    first = (lane & (HEAD_DIM - 1)) < (HEAD_DIM // 2)
    rot = jnp.where(first, pltpu.roll(x, w - HEAD_DIM // 2, x.ndim - 1),
                    pltpu.roll(x, HEAD_DIM // 2, x.ndim - 1))
    return x * cos + rot * sin_signed


def _split3(x):
    hi = x.astype(BF16)
    r1 = x - hi.astype(F32)
    mid = r1.astype(BF16)
    lo = (r1 - mid.astype(F32)).astype(BF16)
    return hi, mid, lo


def _inproj_kernel(x_ref, g_ref, w_ref, o_ref, small_ref, hn_ref, *, small_tile, small_off):
    @pl.when(pl.program_id(1) == 0)
    def _():
        x = x_ref[...]
        ms = jnp.mean(x * x, axis=-1, keepdims=True)
        hn_ref[...] = (x * lax.rsqrt(ms + RMS_EPS) * g_ref[...]).astype(BF16)

    acc = _dot(hn_ref[...], w_ref[...])
    o_ref[...] = acc.astype(BF16)

    @pl.when(pl.program_id(1) == small_tile)
    def _():
        small_ref[...] = acc[:, small_off:small_off + LANES]


def _inproj(x2, g, w_bf16, tm=2048, tn=512):
    t, d = x2.shape
    n = w_bf16.shape[1]
    return pl.pallas_call(
        functools.partial(_inproj_kernel, small_tile=C_SMALL // tn, small_off=C_SMALL % tn),
        out_shape=(jax.ShapeDtypeStruct((t, n), BF16), jax.ShapeDtypeStruct((t, LANES), F32)),
        grid=(t // tm, n // tn),
        in_specs=[pl.BlockSpec((tm, d), lambda i, j: (i, 0)),
                  pl.BlockSpec((1, d), lambda i, j: (0, 0)),
                  pl.BlockSpec((d, tn), lambda i, j: (0, j))],
        out_specs=(pl.BlockSpec((tm, tn), lambda i, j: (i, j)), pl.BlockSpec((tm, LANES), lambda i, j: (i, 0))),
        scratch_shapes=[pltpu.VMEM((tm, d), BF16)],
        compiler_params=_cparams(("parallel", "arbitrary")),
        name="inproj",
    )(x2, g, w_bf16)


def _cum_kernel(s_ref, b_ref, c_ref, ct_ref, *, blk):
    s = s_ref.shape[1]
    z = s_ref[0] + b_ref[...]
    logf = jnp.minimum(z, 0.0) - jnp.log1p(jnp.exp(-jnp.abs(z)))
    r = lax.broadcasted_iota(jnp.int32, (blk, blk), 0)
    c = lax.broadcasted_iota(jnp.int32, (blk, blk), 1)
    tri = jnp.where(c <= r, 1.0, 0.0).astype(BF16)
    carry = jnp.zeros((1, LANES), F32)
    for i in range(s // blk):
        hi, mid, lo = _split3(logf[i * blk:(i + 1) * blk])
        loc = (_dot(tri, hi) + _dot(tri, mid)) + _dot(tri, lo)
        out = loc + carry
        c_ref[0, i * blk:(i + 1) * blk, :] = out
        carry = out[blk - 1:blk, :]
    ct_ref[0] = c_ref[0].T


def _fox_cum(small3, b_row, blk=256):
    b, s, _ = small3.shape
    return pl.pallas_call(
        functools.partial(_cum_kernel, blk=blk),
        out_shape=(jax.ShapeDtypeStruct((b, s, LANES), F32),
                   jax.ShapeDtypeStruct((b, LANES, s), F32)),
        grid=(b,),
        in_specs=[pl.BlockSpec((1, s, LANES), lambda i: (i, 0, 0)),
                  pl.BlockSpec((1, LANES), lambda i: (0, 0))],
        out_specs=(pl.BlockSpec((1, s, LANES), lambda i: (i, 0, 0)),
                   pl.BlockSpec((1, LANES, s), lambda i: (i, 0, 0))),
        compiler_params=_cparams(("parallel",)),
        name="fox_cum",
    )(small3, b_row)


def _cmp_kernel(kc_ref, vc_ref, cos_ref, sin_ref, pek_ref, pev_ref, w1k_ref, w1v_ref, w2k_ref, w2v_ref,
                ko_ref, vto_ref, kr_ref):
    s = kc_ref.shape[1]
    nb = s // CMP_STRIDE
    kr_ref[...] = _rope(kc_ref[0].astype(F32), cos_ref[...], sin_ref[...])

    def mlp(rows, pe_ref, w1_ref, w2_ref):
        pa = jnp.zeros((nb, LANES), F32)
        pb = jnp.zeros((nb, LANES), F32)
        for l in range(CMP_STRIDE):
            x = rows(l)
            pa = pa + _dot((x + pe_ref[l:l + 1, :]).astype(BF16), w1_ref[l])
            pb = pb + _dot((x + pe_ref[CMP_STRIDE + l:CMP_STRIDE + l + 1, :]).astype(BF16), w1_ref[CMP_STRIDE + l])
        z = pa + pltpu.roll(pb, nb - 1, 0)
        h = z * jax.nn.sigmoid(z)
        return _dot(h.astype(BF16), w2_ref[...])

    rows = lambda l: kr_ref[pl.ds(l, nb, stride=CMP_STRIDE), :]
    ko_ref[0] = mlp(rows, pek_ref, w1k_ref, w2k_ref).astype(BF16)
    kr_ref[...] = vc_ref[0].astype(F32)
    vcmp = mlp(rows, pev_ref, w1v_ref, w2v_ref)
    kr_ref[0:nb, :] = vcmp
    vto_ref[0] = kr_ref[0:nb, :].T.astype(BF16)


def _nsa_compress(proj3, cos2, sin2, pek, pev, w1k, w1v, w2k, w2v):
    b, s, _ = proj3.shape
    nb = s // CMP_STRIDE
    kvb = C_NSAKV // LANES
    full = lambda a: pl.BlockSpec(a.shape, lambda i: (0,) * a.ndim)
    return pl.pallas_call(
        _cmp_kernel,
        out_shape=(jax.ShapeDtypeStruct((b, nb, LANES), BF16), jax.ShapeDtypeStruct((b, LANES, nb), BF16)),
        grid=(b,),
        in_specs=[pl.BlockSpec((1, s, LANES), lambda i: (i, 0, kvb)),
                  pl.BlockSpec((1, s, LANES), lambda i: (i, 0, kvb + 1)),
                  full(cos2), full(sin2), full(pek), full(pev), full(w1k), full(w1v), full(w2k), full(w2v)],
        out_specs=(pl.BlockSpec((1, nb, LANES), lambda i: (i, 0, 0)),
                   pl.BlockSpec((1, LANES, nb), lambda i: (i, 0, 0))),
        scratch_shapes=[pltpu.VMEM((s, LANES), F32)],
        compiler_params=_cparams(("parallel",)),
        name="nsa_compress",
    )(proj3, proj3, cos2, sin2, pek, pev, w1k, w1v, w2k, w2v)


def _prep_kernel(kf_ref, vf_ref, ks_ref, vs_ref, kw_ref, vw_ref, cos_ref, sin_ref,
                 okf_ref, ovf_ref, oks_ref, ovs_ref, okw_ref, ovw_ref):
    okf_ref[0] = kf_ref[0]
    ovf_ref[0, 0] = vf_ref[0].astype(F32).T.astype(BF16)
    oks_ref[0] = _rope(ks_ref[0].astype(F32), cos_ref[...], sin_ref[...]).astype(BF16)
    okw_ref[0] = _rope(kw_ref[0].astype(F32), cos_ref[...], sin_ref[...]).astype(BF16)
    tn = ovs_ref.shape[3]
    for i in range(ovs_ref.shape[1]):
        ovs_ref[0, i] = vs_ref[0, i * tn:(i + 1) * tn, :].astype(F32).T.astype(BF16)
        ovw_ref[0, i] = vw_ref[0, i * tn:(i + 1) * tn, :].astype(F32).T.astype(BF16)


def _attn_prep(proj3, cos2, sin2, tk, tn):
    b, s, _ = proj3.shape
    wf = FOX_HEADS * HEAD_DIM
    kvb = C_NSAKV // LANES
    nk = s // tk
    sub = tk // tn
    col = lambda w, c: pl.BlockSpec((1, tk, w), lambda i, j: (i, j, c))
    rows = lambda w: pl.BlockSpec((1, tk, w), lambda i, j: (i, j, 0))
    ntile = pl.BlockSpec((1, sub, LANES, tn), lambda i, j: (i, j, 0, 0))
    return pl.pallas_call(
        _prep_kernel,
        out_shape=(jax.ShapeDtypeStruct((b, s, wf), BF16), jax.ShapeDtypeStruct((b, nk, wf, tk), BF16),
                   jax.ShapeDtypeStruct((b, s, LANES), BF16), jax.ShapeDtypeStruct((b, nk * sub, LANES, tn), BF16),
                   jax.ShapeDtypeStruct((b, s, LANES), BF16), jax.ShapeDtypeStruct((b, nk * sub, LANES, tn), BF16)),
        grid=(b, nk),
        in_specs=[col(wf, C_KF // wf), col(wf, C_VF // wf), col(LANES, kvb + 2), col(LANES, kvb + 3),
                  col(LANES, kvb + 4), col(LANES, kvb + 5),
                  pl.BlockSpec((tk, LANES), lambda i, j: (j, 0)), pl.BlockSpec((tk, LANES), lambda i, j: (j, 0))],
        out_specs=(rows(wf), pl.BlockSpec((1, 1, wf, tk), lambda i, j: (i, j, 0, 0)),
                   rows(LANES), ntile, rows(LANES), ntile),
        compiler_params=_cparams(("parallel", "parallel")),
        name="attn_prep",
    )(proj3, proj3, proj3, proj3, proj3, proj3, cos2, sin2)


def _online_update(s, vt, m_ref, l_ref, acc_ref, idx, mask=None):
    m_old = m_ref[idx]
    m_new = jnp.maximum(m_old, jnp.max(s, axis=0, keepdims=True))
    alpha = jnp.exp(m_old - m_new)
    p = jnp.exp(s - m_new)
    if mask is not None:
        p = jnp.where(mask, p, 0.0)
    l_ref[idx] = alpha * l_ref[idx] + jnp.sum(p, axis=0, keepdims=True)
    acc_ref[idx] = alpha * acc_ref[idx] + _dot(vt, p.astype(BF16))
    m_ref[idx] = m_new


def _padded_qt(q_ref, cos, sin, qt_ref, tq, heads_per_slot):
    qt_ref[...] = jnp.zeros(qt_ref.shape, qt_ref.dtype)
    n_heads = q_ref.shape[2] // HEAD_DIM
    for hp in range(n_heads // 2):
        q2 = q_ref[0, :, hp * LANES:(hp + 1) * LANES].astype(F32)
        if cos is not None:
            q2 = _rope(q2, cos, sin)
        qt = (q2 * (HEAD_DIM ** -0.5)).T.astype(BF16)
        for sub in range(2):
            h = 2 * hp + sub
            slot, r = h // heads_per_slot, h % heads_per_slot
            half = (slot % 2) if heads_per_slot > 1 else sub
            qt_ref[slot, half * HEAD_DIM:(half + 1) * HEAD_DIM, r * tq:(r + 1) * tq] = (
                qt[sub * HEAD_DIM:(sub + 1) * HEAD_DIM, :])


def _nsa_kernel(q_ref, cos_ref, sin_ref, kc_ref, vct_ref, ks_ref, vst_ref, kw_ref, vwt_ref, sm_ref, o_ref,
                qt_ref, oct_ref, m_ref, l_ref, acc_ref, *, tq, tk, n_sel):
    qi = pl.program_id(1)
    G, R = NSA_GROUPS, NSA_REP
    nb = kc_ref.shape[1]
    per_q = tq // tk
    _padded_qt(q_ref, cos_ref[...], sin_ref[...], qt_ref, tq, R)
    m_ref[...] = jnp.full(m_ref.shape, NEG_INF, F32)
    l_ref[...] = jnp.zeros(l_ref.shape, F32)
    acc_ref[...] = jnp.zeros(acc_ref.shape, F32)

    tpos = qi * tq + (lax.broadcasted_iota(jnp.int32, (nb, R * tq), 1) & (tq - 1))
    nrow = lax.broadcasted_iota(jnp.int32, (nb, R * tq), 0)
    cmask = (nrow * CMP_STRIDE + (CMP_LEN - 1) <= tpos) & (nrow < nb - 1)
    oj = lax.broadcasted_iota(jnp.int32, (n_sel, nb), 0)
    on = lax.broadcasted_iota(jnp.int32, (n_sel, nb), 1) * CMP_STRIDE
    overlap_t = jnp.where((on < (oj + 1) * SEL_BLOCK) & (on + CMP_LEN > oj * SEL_BLOCK), 1.0, 0.0).astype(BF16)
    blk = lax.broadcasted_iota(jnp.int32, (n_sel, tq), 0)
    trow = qi * tq + lax.broadcasted_iota(jnp.int32, (n_sel, tq), 1)
    cur = trow // SEL_BLOCK
    forced = (blk == 0) | (blk == cur) | (blk == cur - 1)
    future = blk * SEL_BLOCK > trow
    kc = kc_ref[0]
    for g in range(G):
        s = jnp.where(cmask, _dot(kc, qt_ref[g, 0:LANES, :]), NEG_INF)
        mx = jnp.max(s, axis=0, keepdims=True)
        e = jnp.where(cmask, jnp.exp(s - mx), 0.0)
        p = e / jnp.maximum(jnp.sum(e, axis=0, keepdims=True), 1e-30)
        oct_ref[g] = _dot(vct_ref[0, g * HEAD_DIM:(g + 1) * HEAD_DIM, :], p.astype(BF16))
        psum = p[:, 0:tq]
        for r in range(1, R):
            psum = psum + p[:, r * tq:(r + 1) * tq]
        hi, mid, lo = _split3(psum)
        imp = (_dot(overlap_t, hi) + _dot(overlap_t, mid)) + _dot(overlap_t, lo)
        imp = jnp.where(forced, FORCE_SCORE, jnp.where(future, -FORCE_SCORE, imp))
        rank = jnp.zeros((n_sel, tq), F32)
        for i in range(n_sel):
            ri = imp[i:i + 1, :]
            ahead = (ri > imp) | ((ri == imp) & (blk > i))
            rank = rank + jnp.where(ahead, 1.0, 0.0)
        bias = jnp.where(rank < float(min(SEL_TOPK, n_sel)), 0.0, NEG_INF).astype(BF16)
        for r in range(R):
            qt_ref[g, LANES:LANES + n_sel, r * tq:(r + 1) * tq] = bias

    qpos = qi * tq + (lax.broadcasted_iota(jnp.int32, (tk, R * tq), 1) & (tq - 1))
    krow = lax.broadcasted_iota(jnp.int32, (tk, R * tq), 0)
    er = lax.broadcasted_iota(jnp.int32, (tk, LANES), 0)
    ec = lax.broadcasted_iota(jnp.int32, (tk, LANES), 1)

    def sel_tile(j, diagonal):
        k = ks_ref[0, pl.ds(pl.multiple_of(j * tk, tk), tk), :]
        onehot = jnp.where((j * tk + er) // SEL_BLOCK == ec, 1.0, 0.0).astype(BF16)
        kx = jnp.concatenate([k, onehot], axis=1)
        for g in range(G):
            s = _dot(kx, qt_ref[g])
            if diagonal:
                s = jnp.where(j * tk + krow <= qpos, s, NEG_INF)
            _online_update(s, vst_ref[0, j, g * HEAD_DIM:(g + 1) * HEAD_DIM, :], m_ref, l_ref, acc_ref, g)

    def sel_body(j, c):
        sel_tile(j, False)
        return c

    lax.fori_loop(0, qi * per_q, sel_body, 0)
    for dd in range(per_q):
        sel_tile(qi * per_q + dd, True)

    for back in range(per_q + WINDOW // tk - 1, -1, -1):
        @pl.when((qi + 1) * per_q - 1 - back >= 0)
        def _(back=back):
            j = (qi + 1) * per_q - 1 - back
            k = kw_ref[0, pl.ds(pl.multiple_of(j * tk, tk), tk), :]
            diff = qpos - (j * tk + krow)
            mask = (diff >= 0) & (diff < WINDOW)
            for g in range(G):
                s = jnp.where(mask, _dot(k, qt_ref[g, 0:LANES, :]), NEG_INF)
                _online_update(s, vwt_ref[0, j, g * HEAD_DIM:(g + 1) * HEAD_DIM, :], m_ref, l_ref, acc_ref,
                               G + g, mask)

    gates_t = jax.nn.sigmoid(sm_ref[0]).T
    rows = []
    for h in range(NSA_HEADS):
        g, r = h // R, h % R
        cols = slice(r * tq, (r + 1) * tq)
        gate = lambda br: gates_t[GL_OFF + 3 * h + br:GL_OFF + 3 * h + br + 1, :]
        o_sel = acc_ref[g, :, cols] / jnp.maximum(l_ref[g, :, cols], 1e-30)
        o_win = acc_ref[G + g, :, cols] / jnp.maximum(l_ref[G + g, :, cols], 1e-30)
        rows.append(gate(0) * oct_ref[g, :, cols] + gate(1) * o_sel + gate(2) * o_win)
    o_ref[0] = jnp.concatenate(rows, axis=0).T.astype(o_ref.dtype)


def _nsa_attend(proj3, small3, cos2, sin2, kcmp, vcmp_t, ks, vst, kw, vwt, tq, tk):
    b, s, _ = proj3.shape
    n_sel = s // SEL_BLOCK
    wq = NSA_HEADS * HEAD_DIM
    whole = lambda a: pl.BlockSpec((1,) + a.shape[1:], lambda i, j: (i,) + (0,) * (a.ndim - 1))
    return pl.pallas_call(
        functools.partial(_nsa_kernel, tq=tq, tk=tk, n_sel=n_sel),
        out_shape=jax.ShapeDtypeStruct((b, s, wq), BF16),
        grid=(b, s // tq),
        in_specs=[pl.BlockSpec((1, tq, wq), lambda i, j: (i, j, C_QA // wq)),
                  pl.BlockSpec((tq, LANES), lambda i, j: (j, 0)),
                  pl.BlockSpec((tq, LANES), lambda i, j: (j, 0)),
                  whole(kcmp), whole(vcmp_t), whole(ks), whole(vst), whole(kw), whole(vwt),
                  pl.BlockSpec((1, tq, LANES), lambda i, j: (i, j, 0))],
        out_specs=pl.BlockSpec((1, tq, wq), lambda i, j: (i, j, 0)),
        scratch_shapes=[pltpu.VMEM((NSA_GROUPS, 2 * LANES, NSA_REP * tq), BF16),
                        pltpu.VMEM((NSA_GROUPS, HEAD_DIM, NSA_REP * tq), F32),
                        pltpu.VMEM((2 * NSA_GROUPS, 1, NSA_REP * tq), F32),
                        pltpu.VMEM((2 * NSA_GROUPS, 1, NSA_REP * tq), F32),
                        pltpu.VMEM((2 * NSA_GROUPS, HEAD_DIM, NSA_REP * tq), F32)],
        compiler_params=_cparams(("parallel", "parallel")),
        name="nsa_attend",
    )(proj3, cos2, sin2, kcmp, vcmp_t, ks, vst, kw, vwt, small3)


def _fox_kernel(q_ref, k_ref, vt_ref, cq_ref, ck_ref, o_ref, qt_ref, m_ref, l_ref, acc_ref, *, tq):
    qi = pl.program_id(1)
    tk = tq
    _padded_qt(q_ref, None, None, qt_ref, tq, 1)
    m_ref[...] = jnp.full(m_ref.shape, NEG_INF, F32)
    l_ref[...] = jnp.zeros(l_ref.shape, F32)
    acc_ref[...] = jnp.zeros(acc_ref.shape, F32)
    cq = cq_ref[0]
    causal = lax.broadcasted_iota(jnp.int32, (tk, tq), 0) <= lax.broadcasted_iota(jnp.int32, (tk, tq), 1)

    def tile(j, diagonal):
        rows = pl.ds(pl.multiple_of(j * tk, tk), tk)
        ck = ck_ref[0, rows, :]
        for h in range(FOX_HEADS):
            hp = h // 2
            s = _dot(k_ref[0, rows, hp * LANES:(hp + 1) * LANES], qt_ref[h])
            s = (s - ck[:, FL_OFF + h:FL_OFF + h + 1]) + cq[FL_OFF + h:FL_OFF + h + 1, :]
            if diagonal:
                s = jnp.where(causal, s, NEG_INF)
            _online_update(s, vt_ref[0, j, h * HEAD_DIM:(h + 1) * HEAD_DIM, :], m_ref, l_ref, acc_ref, h)

    def body(j, c):
        tile(j, False)
        return c

    lax.fori_loop(0, qi, body, 0)
    tile(qi, True)
    rows = [acc_ref[h] / jnp.maximum(l_ref[h], 1e-30) for h in range(FOX_HEADS)]
    o_ref[0] = jnp.concatenate(rows, axis=0).T.astype(o_ref.dtype)


def _fox_attend(proj3, kf, vft, cum, cum_t, tq):
    b, s, _ = proj3.shape
    w = FOX_HEADS * HEAD_DIM
    whole = lambda a: pl.BlockSpec((1,) + a.shape[1:], lambda i, j: (i,) + (0,) * (a.ndim - 1))
    return pl.pallas_call(
        functools.partial(_fox_kernel, tq=tq),
        out_shape=jax.ShapeDtypeStruct((b, s, w), BF16),
        grid=(b, s // tq),
        in_specs=[pl.BlockSpec((1, tq, w), lambda i, j: (i, j, C_QF // w)),
                  whole(kf), whole(vft),
                  pl.BlockSpec((1, LANES, tq), lambda i, j: (i, 0, j)),
                  whole(cum)],
        out_specs=pl.BlockSpec((1, tq, w), lambda i, j: (i, j, 0)),
        scratch_shapes=[pltpu.VMEM((FOX_HEADS, LANES, tq), BF16),
                        pltpu.VMEM((FOX_HEADS, 1, tq), F32),
                        pltpu.VMEM((FOX_HEADS, 1, tq), F32),
                        pltpu.VMEM((FOX_HEADS, HEAD_DIM, tq), F32)],
        compiler_params=_cparams(("parallel", "parallel")),
        name="fox_attend",
    )(proj3, kf, vft, cum_t, cum)


def _memkv_kernel(m_ref, g_ref, w_ref, o_ref):
    x = m_ref[0]
    ms = jnp.mean(x * x, axis=-1, keepdims=True)
    hn = (x * lax.rsqrt(ms + RMS_EPS) * g_ref[...]).astype(BF16)
    o_ref[0] = _dot(hn, w_ref[...]).astype(BF16)


def _mem_kv(mem, g, w_bf16):
    b, m, d = mem.shape
    n = w_bf16.shape[1]
    return pl.pallas_call(
        _memkv_kernel,
        out_shape=jax.ShapeDtypeStruct((b, m, n), BF16),
        grid=(b,),
        in_specs=[pl.BlockSpec((1, m, d), lambda i: (i, 0, 0)),
                  pl.BlockSpec((1, d), lambda i: (0, 0)),
                  pl.BlockSpec((d, n), lambda i: (0, 0))],
        out_specs=pl.BlockSpec((1, m, n), lambda i: (i, 0, 0)),
        compiler_params=_cparams(("parallel",)),
        name="mem_kv",
    )(mem, g, w_bf16)


def _memattn_kernel(q_ref, kv_ref, o_ref):
    w = MEM_HEADS * MEM_HEAD_DIM
    for h in range(MEM_HEADS):
        lo, hi = h * MEM_HEAD_DIM, (h + 1) * MEM_HEAD_DIM
        s = _dot_nt(q_ref[0, :, lo:hi].astype(BF16), kv_ref[0, :, lo:hi]) * (MEM_HEAD_DIM ** -0.5)
        e = jnp.exp(s - jnp.max(s, axis=-1, keepdims=True))
        p = e / jnp.sum(e, axis=-1, keepdims=True)
        o_ref[0, :, lo:hi] = _dot(p.astype(BF16), kv_ref[0, :, w + lo:w + hi]).astype(o_ref.dtype)


def _mem_attend(proj3, kv, tq=512):
    b, s, _ = proj3.shape
    m = kv.shape[1]
    w = MEM_HEADS * MEM_HEAD_DIM
    return pl.pallas_call(
        _memattn_kernel,
        out_shape=jax.ShapeDtypeStruct((b, s, w), BF16),
        grid=(b, s // tq),
        in_specs=[pl.BlockSpec((1, tq, w), lambda i, j: (i, j, C_QM // w)),
                  pl.BlockSpec((1, m, 2 * w), lambda i, j: (i, 0, 0))],
        out_specs=pl.BlockSpec((1, tq, w), lambda i, j: (i, j, 0)),
        compiler_params=_cparams(("parallel", "parallel")),
        name="mem_attend",
    )(proj3, kv)


def _merge_kernel(on_ref, of_ref, om_ref, ln_ref, lf_ref, lm_ref, bm_ref, wn_ref, wf_ref, wm_ref,
                  wo_ref, x_ref, g_ref, wr_ref, br_ref, x1_ref, hn_ref, idx_ref, gate_ref):
    d = x_ref.shape[1]

    def branch(o_ref, l_ref, w_ref, k):
        gate = jax.nn.sigmoid(l_ref[...] + bm_ref[:, k * d:(k + 1) * d])
        return gate * _dot(o_ref[...].astype(BF16), w_ref[...])

    merged = branch(on_ref, ln_ref, wn_ref, 0) + branch(of_ref, lf_ref, wf_ref, 1) + branch(om_ref, lm_ref, wm_ref, 2)
    x1 = x_ref[...] + _dot(merged.astype(BF16), wo_ref[...])
    x1_ref[...] = x1
    ms = jnp.mean(x1 * x1, axis=-1, keepdims=True)
    hn = x1 * lax.rsqrt(ms + RMS_EPS) * g_ref[...]
    hn_ref[...] = hn
    h_hi = hn.astype(BF16)
    h_lo = (hn - h_hi.astype(F32)).astype(BF16)
    wr = wr_ref[...]
    w_hi = wr.astype(BF16)
    w_lo = (wr - w_hi.astype(F32)).astype(BF16)
    logits = (_dot(h_hi, w_hi) + (_dot(h_lo, w_hi) + _dot(h_hi, w_lo))) + br_ref[...]
    tm, ne = logits.shape
    lane_e = lax.broadcasted_iota(jnp.int32, (tm, ne), 1)
    lane = lax.broadcasted_iota(jnp.int32, (tm, LANES), 1)
    idx_slab = jnp.zeros((tm, LANES), jnp.int32)
    val_slab = jnp.zeros((tm, LANES), F32)
    work = logits
    vals = []
    for k in range(TOP_K):
        m = jnp.max(work, axis=-1, keepdims=True)
        idx = jnp.min(jnp.where(work == m, lane_e, ne), axis=-1, keepdims=True)
        work = jnp.where(lane_e == idx, -jnp.inf, work)
        idx_slab = jnp.where(lane == k, idx, idx_slab)
        vals.append(m)
    es = [jnp.exp(v - vals[0]) for v in vals]
    tot = es[0]
    for e in es[1:]:
        tot = tot + e
    for k in range(TOP_K):
        val_slab = jnp.where(lane == k, es[k] / tot, val_slab)
    idx_ref[...] = idx_slab
    gate_ref[...] = val_slab


def _merge_route(o_nsa, o_fox, o_mem, proj, b_merge, wn, wf, wm, wo, x2, g_ffn, w_router, b_router, tm=512):
    t, d = x2.shape
    wb = o_nsa.shape[1]
    row = lambda w: pl.BlockSpec((tm, w), lambda i: (i, 0))
    full = lambda a: pl.BlockSpec(a.shape, lambda i: (0,) * a.ndim)
    return pl.pallas_call(
        _merge_kernel,
        out_shape=(jax.ShapeDtypeStruct((t, d), F32), jax.ShapeDtypeStruct((t, d), F32),
                   jax.ShapeDtypeStruct((t, LANES), jnp.int32), jax.ShapeDtypeStruct((t, LANES), F32)),
        grid=(t // tm,),
        in_specs=[row(wb), row(wb), row(wb),
                  pl.BlockSpec((tm, d), lambda i: (i, 0)),
                  pl.BlockSpec((tm, d), lambda i: (i, 1)),
                  pl.BlockSpec((tm, d), lambda i: (i, 2)),
                  full(b_merge), full(wn), full(wf), full(wm), full(wo),
                  row(d), full(g_ffn), full(w_router), full(b_router)],
        out_specs=(row(d), row(d), row(LANES), row(LANES)),
        compiler_params=_cparams(("parallel",)),
        name="merge_route",
    )(o_nsa, o_fox, o_mem, proj, proj, proj, b_merge, wn, wf, wm, wo, x2, g_ffn, w_router, b_router)


def _rank_kernel(idx_ref, rank_ref, cnt_ref, carry_ref, *, tm):
    @pl.when(pl.program_id(0) == 0)
    def _():
        carry_ref[...] = jnp.zeros(carry_ref.shape, F32)

    idx = idx_ref[...]
    lane_e = lax.broadcasted_iota(jnp.int32, (tm, N_EXPERTS), 1)
    hots = [jnp.where(idx[:, k:k + 1] == lane_e, 1.0, 0.0) for k in range(TOP_K)]
    cnt = hots[0]
    for hk in hots[1:]:
        cnt = cnt + hk
    r = lax.broadcasted_iota(jnp.int32, (tm, tm), 0)
    c = lax.broadcasted_iota(jnp.int32, (tm, tm), 1)
    strict = jnp.where(c < r, 1.0, 0.0).astype(BF16)
    before = _dot(strict, cnt.astype(BF16)) + carry_ref[...]
    lane = lax.broadcasted_iota(jnp.int32, (tm, LANES), 1)
    slab = jnp.zeros((tm, LANES), F32)
    for k in range(TOP_K):
        slab = jnp.where(lane == k, jnp.sum(hots[k] * before, axis=-1, keepdims=True), slab)
    rank_ref[...] = slab.astype(jnp.int32)
    total = carry_ref[...] + jnp.sum(cnt, axis=0, keepdims=True)
    carry_ref[...] = total
    cnt_ref[...] = jnp.broadcast_to(total, cnt_ref.shape).astype(jnp.int32)


def _moe_rank(idx_slab, tm=256):
    t = idx_slab.shape[0]
    return pl.pallas_call(
        functools.partial(_rank_kernel, tm=tm),
        out_shape=(jax.ShapeDtypeStruct((t, LANES), jnp.int32),
                   jax.ShapeDtypeStruct((8, N_EXPERTS), jnp.int32)),
        grid=(t // tm,),
        in_specs=[pl.BlockSpec((tm, LANES), lambda i: (i, 0))],
        out_specs=(pl.BlockSpec((tm, LANES), lambda i: (i, 0)),
                   pl.BlockSpec((8, N_EXPERTS), lambda i: (0, 0))),
        scratch_shapes=[pltpu.VMEM((1, N_EXPERTS), F32)],
        compiler_params=_cparams(("arbitrary",)),
        name="moe_rank",
    )(idx_slab)


def _slot_tok_kernel(dest_ref, fill_ref, o_ref, *, steps, unroll):
    i = pl.program_id(0)
    shift = TOP_K.bit_length() - 1
    assert TOP_K == 1 << shift
    n_put = dest_ref.shape[0] // steps
    n_fill = fill_ref.shape[0] // steps

    def put(j, c):
        a = i * n_put + j
        o_ref[dest_ref[a]] = lax.shift_right_logical(a, shift)
        return c

    lax.fori_loop(0, n_put, put, 0, unroll=unroll)

    def fill(j, c):
        o_ref[fill_ref[i * n_fill + j]] = 0
        return c

    lax.fori_loop(0, n_fill, fill, 0, unroll=unroll)


def _moe_slot_tok(dest_flat, fill_slots, steps=32):
    n_slots = dest_flat.shape[0] + fill_slots.shape[0]
    assert dest_flat.shape[0] % steps == 0 and fill_slots.shape[0] % steps == 0
    return pl.pallas_call(
        functools.partial(_slot_tok_kernel, steps=steps, unroll=8),
        out_shape=jax.ShapeDtypeStruct((n_slots,), jnp.int32),
        grid_spec=pltpu.PrefetchScalarGridSpec(
            num_scalar_prefetch=2, grid=(steps,), in_specs=[],
            out_specs=pl.BlockSpec(memory_space=pltpu.SMEM)),
        compiler_params=_cparams(("arbitrary",)),
        name="moe_slot_tok",
    )(dest_flat, fill_slots)


def _expert_kernel(be_ref, nu_ref, nx_ref, tok_ref, hn_ref, wgu_ref, wdn_ref, bg_ref, bu_ref, bd_ref, o_ref,
                   buf_ref, act_ref, sgu_ref, sdn_ref, wg_ref, wu_ref, wd_ref, sem, wsem, *, unroll, chunk):
    i = pl.program_id(0)
    n_used = nu_ref[0]
    e = be_ref[i]
    f = wg_ref.shape[1]
    d = wd_ref.shape[1]

    def row_copy(tok, r, s):
        return pltpu.make_async_copy(hn_ref.at[pl.ds(tok, 1)], buf_ref.at[s, pl.ds(r, 1)], sem.at[s])

    def weight_copies(ex):
        return (pltpu.make_async_copy(wgu_ref.at[ex], sgu_ref, wsem.at[0]),
                pltpu.make_async_copy(wdn_ref.at[ex], sdn_ref, wsem.at[1]))

    def wait_rows(s):
        def wait(r, c):
            row_copy(0, 0, s).wait()
            return c

        lax.fori_loop(0, MOE_ROWS, wait, 0, unroll=unroll)

    @pl.when((i == 0) & (n_used > 0))
    def _():
        for cp in weight_copies(e):
            cp.start(priority=1)

        def body(r2, c):
            for j in range(2):
                row_copy(tok_ref[2 * r2 + j], 2 * r2 + j, 0).start()
            return c

        lax.fori_loop(0, MOE_ROWS // 2, body, 0, unroll=unroll)

    first = (i == 0) | (e != be_ref[jnp.maximum(i - 1, 0)])

    @pl.when((i < n_used) & first)
    def _():
        for cp in weight_copies(e):
            cp.wait()
        half = chunk // 2
        pr = lax.broadcasted_iota(jnp.int32, (chunk, chunk), 0)
        pc = lax.broadcasted_iota(jnp.int32, (chunk, chunk), 1)
        perm = jnp.where(pr == jnp.where(pc < half, 2 * pc, 2 * (pc - half) + 1), 1.0, 0.0).astype(BF16)
        for c in range(2 * f // chunk):
            sep = _dot(sgu_ref[:, c * chunk:(c + 1) * chunk].astype(BF16), perm).astype(BF16)
            wg_ref[:, c * half:(c + 1) * half] = sep[:, :half]
            wu_ref[:, c * half:(c + 1) * half] = sep[:, half:]
        wd_ref[...] = sdn_ref[...].astype(BF16)

        @pl.when(nx_ref[i] >= 0)
        def _():
            for cp in weight_copies(nx_ref[i]):
                cp.start(priority=1)

    @pl.when(i < n_used)
    def _():
        cur = i % 2
        nxt = 1 - cur
        base = jnp.minimum(i + 1, n_used - 1) * MOE_ROWS
        for r in range(MOE_ROWS):
            row_copy(tok_ref[base + r], r, nxt).start()
        wait_rows(cur)
        x = buf_ref[cur].astype(BF16)
        for c in range(f // chunk):
            cols = slice(c * chunk, (c + 1) * chunk)
            g = jnp.minimum(_dot(x, wg_ref[:, cols]) + bg_ref[0, :, cols], SWIGLU_LIMIT)
            u = jnp.clip(_dot(x, wu_ref[:, cols]) + bu_ref[0, :, cols], -SWIGLU_LIMIT, SWIGLU_LIMIT)
            act_ref[:, cols] = ((u + 1.0) * g * jax.nn.sigmoid(SWIGLU_ALPHA * g)).astype(BF16)
        act = act_ref[...]
        for c in range(d // chunk):
            cols = slice(c * chunk, (c + 1) * chunk)
            o_ref[:, cols] = _dot(act, wd_ref[:, cols]) + bd_ref[0, :, cols]

        @pl.when(i == n_used - 1)
        def _():
            wait_rows(nxt)

    @pl.when(i >= n_used)
    def _():
        o_ref[...] = jnp.zeros(o_ref.shape, o_ref.dtype)


def _moe_experts(block_exp, n_used, next_exp, slot_tok, hn, w_gate_up, w_down, bg, bu, bd):
    n_slots = slot_tok.shape[0]
    d = hn.shape[1]
    f = w_down.shape[1]
    nblk = n_slots // MOE_ROWS
    bspec = lambda w: pl.BlockSpec((1, 1, w), lambda i, be, nu, nx, tk: (be[i], 0, 0))
    return pl.pallas_call(
        functools.partial(_expert_kernel, unroll=8, chunk=256),
        out_shape=jax.ShapeDtypeStruct((n_slots, d), F32),
        grid_spec=pltpu.PrefetchScalarGridSpec(
            num_scalar_prefetch=4,
            grid=(nblk,),
            in_specs=[pl.BlockSpec(memory_space=pl.ANY), pl.BlockSpec(memory_space=pl.ANY),
                      pl.BlockSpec(memory_space=pl.ANY), bspec(f), bspec(f), bspec(d)],
            out_specs=pl.BlockSpec((MOE_ROWS, d), lambda i, be, nu, nx, tk: (i, 0)),
            scratch_shapes=[pltpu.VMEM((2, MOE_ROWS, d), hn.dtype), pltpu.VMEM((MOE_ROWS, f), BF16),
                            pltpu.VMEM((d, 2 * f), F32), pltpu.VMEM((f, d), F32),
                            pltpu.VMEM((d, f), BF16), pltpu.VMEM((d, f), BF16), pltpu.VMEM((f, d), BF16),
                            pltpu.SemaphoreType.DMA((2,)), pltpu.SemaphoreType.DMA((2,))]),
        compiler_params=_cparams(("arbitrary",)),
        name="moe_experts",
    )(block_exp, n_used, next_exp, slot_tok, hn, w_gate_up, w_down, bg, bu, bd)


def _combine_kernel(dest_ref, gate_ref, x1_ref, g_ref, ys_ref, o_ref, buf_ref, sem, *, tm, unroll):
    i = pl.program_id(0)
    n = pl.num_programs(0)

    def row_copy(slot_row, k, r, s):
        return pltpu.make_async_copy(ys_ref.at[pl.ds(slot_row, 1)], buf_ref.at[s, k, pl.ds(r, 1)], sem.at[s])

    def issue(step, s):
        base = step * (tm * TOP_K)
        for r in range(tm):
            for k in range(TOP_K):
                row_copy(dest_ref[base + r * TOP_K + k], k, r, s).start()

    @pl.when(i == 0)
    def _():
        issue(0, 0)

    @pl.when(i + 1 < n)
    def _():
        issue(i + 1, (i + 1) % 2)

    cur = i % 2

    def wait(r, c):
        row_copy(0, 0, 0, cur).wait()
        return c

    lax.fori_loop(0, tm * TOP_K, wait, 0, unroll=unroll)

    gate = gate_ref[...]
    y = x1_ref[...]
    for k in range(TOP_K):
        y = y + gate[:, k:k + 1] * buf_ref[cur, k]
    ms = jnp.mean(y * y, axis=-1, keepdims=True)
    o_ref[...] = y * lax.rsqrt(ms + RMS_EPS) * g_ref[...]


def _moe_combine(dest_flat, gate_slab, x1, g_final, ys, tm=128):
    t, d = x1.shape
    return pl.pallas_call(
        functools.partial(_combine_kernel, tm=tm, unroll=8),
        out_shape=jax.ShapeDtypeStruct((t, d), F32),
        grid_spec=pltpu.PrefetchScalarGridSpec(
            num_scalar_prefetch=1,
            grid=(t // tm,),
            in_specs=[pl.BlockSpec((tm, LANES), lambda i, dst: (i, 0)),
                      pl.BlockSpec((tm, d), lambda i, dst: (i, 0)),
                      pl.BlockSpec((1, d), lambda i, dst: (0, 0)),
                      pl.BlockSpec(memory_space=pl.ANY)],
            out_specs=pl.BlockSpec((tm, d), lambda i, dst: (i, 0)),
            scratch_shapes=[pltpu.VMEM((2, TOP_K, tm, d), F32), pltpu.SemaphoreType.DMA((2,))]),
        compiler_params=_cparams(("arbitrary",)),
        name="moe_combine",
    )(dest_flat, gate_slab, x1, g_final, ys)


def _rope_tables(s):
    inv = ROPE_THETA ** (-jnp.arange(0, HEAD_DIM, 2, dtype=F32) / HEAD_DIM)
    ang = jnp.arange(s, dtype=F32)[:, None] * inv[None, :]
    cos, sin = jnp.cos(ang), jnp.sin(ang)
    return jnp.concatenate([cos, cos], axis=-1), jnp.concatenate([-sin, sin], axis=-1)


def _permute_w_in(w):
    d = w.shape[0]
    pad = N_PROJ - (C_SMALL + 32)
    cols = [w[:, 3360:6432], w[:, 0:512], w[:, 1304:1816], w[:, 1816:2328], w[:, 2328:2840],
            w[:, 2848:3360], w[:, 512:1280], w[:, 1280:1304], w[:, 2840:2848], jnp.zeros((d, pad), w.dtype)]
    return jnp.concatenate(cols, axis=1).astype(BF16)


def _layer(x, mem, g_mix, w_in, b_forget, b_merge, pe_k, w1_k, w2_k, pe_v, w1_v, w2_v, g_mem, w_mem_kv,
           w_br_nsa, w_br_fox, w_br_mem, w_out, g_ffn, w_router, b_router, w_gate_up, b_gate_up,
           w_down, b_down, g_final):
    b, s, d = x.shape
    t = b * s
    x2 = x.reshape(t, d)
    cos64, sin64 = _rope_tables(s)
    cos2 = jnp.concatenate([cos64, cos64], axis=-1)
    sin2 = jnp.concatenate([sin64, sin64], axis=-1)

    proj, small = _inproj(x2, g_mix.reshape(1, d), _permute_w_in(w_in), tm=min(2048, t))
    proj3 = proj.reshape(b, s, N_PROJ)
    small3 = small.reshape(b, s, LANES)
    b_row = jnp.zeros((1, LANES), F32).at[0, FL_OFF:FL_OFF + FOX_HEADS].set(b_forget)
    cum, cum_t = _fox_cum(small3, b_row)

    def both_groups(w):
        z = jnp.zeros_like(w)
        return jnp.concatenate([jnp.concatenate([w, z], axis=-1), jnp.concatenate([z, w], axis=-1)], axis=-2).astype(BF16)

    kcmp, vcmp_t = _nsa_compress(
        proj3, cos2, sin2, jnp.concatenate([pe_k, pe_k], axis=-1), jnp.concatenate([pe_v, pe_v], axis=-1),
        both_groups(w1_k), both_groups(w1_v), both_groups(w2_k), both_groups(w2_v))
    fox_tile = min(FOX_TILE, s)
    kf, vft, ks, vst, kw, vwt = _attn_prep(proj3, cos2, sin2, fox_tile, NSA_K_TILE)
    o_nsa = _nsa_attend(proj3, small3, cos2, sin2, kcmp, vcmp_t, ks, vst, kw, vwt, min(NSA_Q_TILE, s), NSA_K_TILE)
    o_fox = _fox_attend(proj3, kf, vft, cum, cum_t, fox_tile)
    mem_kv = _mem_kv(mem, g_mem.reshape(1, d), w_mem_kv.astype(BF16))
    o_mem = _mem_attend(proj3, mem_kv)

    x1, hn, idx_slab, gate_slab = _merge_route(
        o_nsa.reshape(t, -1), o_fox.reshape(t, -1), o_mem.reshape(t, -1), proj, b_merge.reshape(1, -1),
        w_br_nsa.astype(BF16), w_br_fox.astype(BF16), w_br_mem.astype(BF16), w_out.astype(BF16),
        x2, g_ffn.reshape(1, d), w_router, b_router.reshape(1, -1))

    rank_slab, cnt8 = _moe_rank(idx_slab)
    counts = cnt8[0]
    padded = (counts + MOE_ROWS - 1) // MOE_ROWS * MOE_ROWS
    pad_end = jnp.cumsum(padded)
    pad_start = pad_end - padded
    n_assign = t * TOP_K
    nblk = -(-(n_assign + N_EXPERTS * (MOE_ROWS - 1)) // MOE_ROWS)
    n_slots = nblk * MOE_ROWS
    top_idx = idx_slab[:, :TOP_K]
    dest = (pad_start[top_idx] + rank_slab[:, :TOP_K]).reshape(-1).astype(jnp.int32)
    gap = padded - counts
    gap_end = jnp.cumsum(gap)
    jf = jnp.arange(n_slots - n_assign, dtype=jnp.int32)
    e_of = jnp.minimum(jnp.sum(jf[:, None] >= gap_end[None, :], axis=1), N_EXPERTS - 1)
    fill = jnp.where(jf < gap_end[-1], (pad_start + counts)[e_of] + (jf - (gap_end - gap)[e_of]),
                     pad_end[-1] + (jf - gap_end[-1])).astype(jnp.int32)
    slot_tok = _moe_slot_tok(dest, fill)
    blk_start = jnp.arange(nblk, dtype=jnp.int32) * MOE_ROWS
    block_exp = jnp.minimum(jnp.sum(blk_start[:, None] >= pad_end[None, :], axis=1),
                            N_EXPERTS - 1).astype(jnp.int32)
    n_used = (pad_end[-1] // MOE_ROWS).astype(jnp.int32).reshape(1)

    e_ids = jnp.arange(N_EXPERTS, dtype=jnp.int32)
    later = (e_ids[None, :] > block_exp[:, None]) & (counts[None, :] > 0)
    next_exp = jnp.min(jnp.where(later, e_ids[None, :], N_EXPERTS), axis=1)
    next_exp = jnp.where(next_exp < N_EXPERTS, next_exp, -1).astype(jnp.int32)

    f = w_down.shape[1]
    bgu = b_gate_up.reshape(N_EXPERTS, 1, f, 2)
    ys = _moe_experts(block_exp, n_used, next_exp, slot_tok, hn, w_gate_up, w_down,
                      bgu[..., 0], bgu[..., 1], b_down.reshape(N_EXPERTS, 1, d))
    out = _moe_combine(dest, gate_slab, x1, g_final.reshape(1, d), ys)
    return out.reshape(b, s, d)


def kernel(x, mem, g_mix, w_in, b_forget, b_merge, nsa_pe_k, nsa_w1_k, nsa_w2_k, nsa_pe_v, nsa_w1_v, nsa_w2_v, g_mem, w_mem_kv, w_branch_nsa, w_branch_fox, w_branch_mem, w_out, g_ffn, w_router, b_router, w_gate_up, b_gate_up, w_down, b_down, g_final):
    assert g_mix.shape[0] == 1, "single-layer trunk"
    return _layer(x, mem, g_mix[0], w_in[0], b_forget[0], b_merge[0], nsa_pe_k[0], nsa_w1_k[0], nsa_w2_k[0],
                  nsa_pe_v[0], nsa_w1_v[0], nsa_w2_v[0], g_mem[0], w_mem_kv[0], w_branch_nsa[0],
                  w_branch_fox[0], w_branch_mem[0], w_out[0], g_ffn[0], w_router[0], b_router[0],
                  w_gate_up[0], b_gate_up[0], w_down[0], b_down[0], g_final)
```
